```python
import math
import jax
import jax.numpy as jnp
from jax import lax
import numpy as np

D_MODEL = 1024
BATCH = 4
SEQ = 4096
DEPTH = 4
DEC_BATCH = 2
DEC_SEQ = 8192
PAST_LEN = 128

GRID_W = 64
N_HEADS = 8
N_KV_HEADS = 4
HEAD_DIM = 64
Q_PER_KV = N_HEADS // N_KV_HEADS
ATTN_WIDTH = N_HEADS * HEAD_DIM
KV_WIDTH = N_KV_HEADS * HEAD_DIM
Q_BLOCK = 128
ROPE_THETA = 10000.0
ROPE_PAIRS_PER_AXIS = HEAD_DIM // 4
D_HYENA = D_MODEL // 2
SHORT_CONV = 3
FILTER_EMB = 33
FILTER_BANDS = (FILTER_EMB - 1) // 2
FILTER_ORDER = 64
DECAY_TARGET = 1e-2
FAST_DECAY_PCT = 0.3
SLOW_DECAY_PCT = 1.5
MAX_DECAY = math.log(DECAY_TARGET) / FAST_DECAY_PCT
MIN_DECAY = math.log(DECAY_TARGET) / SLOW_DECAY_PCT
MOD_SHIFT = 0.05
N_BRANCHES = 2
SPLIT_POINTS = (ATTN_WIDTH, ATTN_WIDTH + KV_WIDTH, ATTN_WIDTH + 2 * KV_WIDTH,
                ATTN_WIDTH + 2 * KV_WIDTH + 3 * D_HYENA)
IN_COLS = ATTN_WIDTH + 2 * KV_WIDTH + 3 * D_HYENA + N_BRANCHES * D_MODEL
N_GROUPS = 4
EXPERTS_PER_GROUP = 8
N_EXPERTS = N_GROUPS * EXPERTS_PER_GROUP
TOP_K = 2
D_EXPERT = D_MODEL // 4
MOE_BLOCK = 128
RMS_EPS = 1e-6

kernel_name = 'hybrid_hyena_gqa_hmoe_encoder'


def rms_norm(x, gain):
    xf = x.astype(jnp.float32)
    xf = xf * lax.rsqrt(jnp.mean(xf * xf, axis=-1, keepdims=True) + RMS_EPS)
    return xf * gain.astype(jnp.float32)


def modulate(normed, shift, scale, dtype):
    return (normed * (1.0 + scale[:, None, :].astype(jnp.float32))
            + shift[:, None, :].astype(jnp.float32)).astype(dtype)


def axial_rope_tables(seq_len):
    rows = seq_len // GRID_W
    row = jnp.broadcast_to(jnp.arange(rows, dtype=jnp.float32)[:, None], (rows, GRID_W)).reshape(-1)
    col = jnp.broadcast_to(jnp.arange(GRID_W, dtype=jnp.float32)[None, :], (rows, GRID_W)).reshape(-1)
    inv_freq = ROPE_THETA ** (-jnp.arange(ROPE_PAIRS_PER_AXIS, dtype=jnp.float32) / ROPE_PAIRS_PER_AXIS)
    ang = jnp.concatenate([row[:, None] * inv_freq, col[:, None] * inv_freq], axis=-1)
    return jnp.cos(ang), jnp.sin(ang)


def apply_axial_rope(x, cos, sin):
    xf = x.astype(jnp.float32).reshape(x.shape[:-1] + (HEAD_DIM // 2, 2))
    x0, x1 = xf[..., 0], xf[..., 1]
    c = cos[None, :, None, :]
    s = sin[None, :, None, :]
    return jnp.stack([x0 * c - x1 * s, x0 * s + x1 * c], axis=-1).reshape(x.shape)


def blocked_bidirectional_gqa(q, k, v):
    b, seq_len = q.shape[0], q.shape[1]
    n_blocks = seq_len // Q_BLOCK
    qb = q.reshape(b, n_blocks, Q_BLOCK, N_KV_HEADS, Q_PER_KV, HEAD_DIM).transpose(1, 0, 2, 3, 4, 5)
    scale = HEAD_DIM ** -0.5
    kf = k.astype(jnp.float32)
    vf = v.astype(jnp.float32)

    def one_block(q_blk):
        s = jnp.einsum('bqhgd,bkhd->bhgqk', q_blk, kf) * scale
        p = jax.nn.softmax(s, axis=-1)
        return jnp.einsum('bhgqk,bkhd->bqhgd', p, vf)

    o = lax.map(one_block, qb)
    return o.transpose(1, 0, 2, 3, 4, 5).reshape(b, seq_len, ATTN_WIDTH)


def short_conv_centred(u, w, bias):
    seq_len = u.shape[1]
    half = SHORT_CONV // 2
    up = jnp.pad(u, ((0, 0), (half, half), (0, 0)))
    out = bias
    for j in range(SHORT_CONV):
        out = out + up[:, j:j + seq_len] * w[j]
    return out


def hyena_implicit_filters(seq_len, w1, b1, w2, b2, w3, b3, freq, w4):
    f32 = jnp.float32
    t = jnp.linspace(0.0, 1.0, seq_len, dtype=f32)[:, None]
    w = (2.0 * math.pi / seq_len) * jnp.arange(seq_len, dtype=f32)[:, None]
    bands = jnp.linspace(1e-4, FILTER_BANDS - 1, FILTER_BANDS, dtype=f32)[None, :]
    z = jnp.concatenate([t, jnp.cos(bands * w), -jnp.sin(bands * w)], axis=-1)
    fr = freq.astype(f32)
    h = jnp.sin(fr * (z @ w1.astype(f32) + b1.astype(f32)))
    h = jnp.sin(fr * (h @ w2.astype(f32) + b2.astype(f32)))
    h = jnp.sin(fr * (h @ w3.astype(f32) + b3.astype(f32)))
    h = h @ w4.astype(f32)
    deltas = jnp.abs(jnp.linspace(MIN_DECAY, MAX_DECAY, D_HYENA, dtype=f32))
    deltas = jnp.concatenate([deltas, deltas])
    h = h * (jnp.exp(-t * deltas) + MOD_SHIFT)
    return h[:, :D_HYENA], h[:, D_HYENA:]


def bidirectional_long_conv(u, h_fwd, h_bwd, skip):
    seq_len, ch = h_fwd.shape
    n_fft = 2 * seq_len
    kern = jnp.concatenate([h_fwd, jnp.zeros((1, ch), jnp.float32), h_bwd[:0:-1]], axis=0)
    k_f = jnp.fft.rfft(kern, n=n_fft, axis=0)
    uf = u.astype(jnp.float32)
    u_f = jnp.fft.rfft(uf, n=n_fft, axis=1)
    y = jnp.fft.irfft(u_f * k_f[None], n=n_fft, axis=1)[:, :seq_len]
    return y + uf * skip.astype(jnp.float32)


def token_mixer(h, cos, sin, w_in, q_gain, k_gain, conv_w, conv_b, filt_w1, filt_b1, filt_w2, filt_b2,
                filt_w3, filt_b3, filt_freq, filt_w4, hyena_skip, w_br_attn, w_br_hyena, w_out):
    b, seq_len, _ = h.shape
    proj = jnp.einsum('bld,de->ble', h, w_in)
    q, k, v, u_hy, gates = jnp.split(proj, SPLIT_POINTS, axis=-1)
    q = apply_axial_rope(rms_norm(q.reshape(b, seq_len, N_HEADS, HEAD_DIM), q_gain), cos, sin)
    k = apply_axial_rope(rms_norm(k.reshape(b, seq_len, N_KV_HEADS, HEAD_DIM), k_gain), cos, sin)
    v = v.reshape(b, seq_len, N_KV_HEADS, HEAD_DIM)
    attn = blocked_bidirectional_gqa(q, k, v).astype(h.dtype)
    u = short_conv_centred(u_hy, conv_w, conv_b)
    x1, x2, vh = jnp.split(u, 3, axis=-1)
    h_fwd, h_bwd = hyena_implicit_filters(seq_len, filt_w1, filt_b1, filt_w2, filt_b2,
                                          filt_w3, filt_b3, filt_freq, filt_w4)
    hy = (x2.astype(jnp.float32) * bidirectional_long_conv(vh * x1, h_fwd, h_bwd, hyena_skip)).astype(h.dtype)
    g_attn, g_hy = jnp.split(jax.nn.sigmoid(gates), N_BRANCHES, axis=-1)
    merged = (g_attn * jnp.einsum('bla,ad->bld', attn, w_br_attn)
              + g_hy * jnp.einsum('blc,cd->bld', hy, w_br_hyena))
    return jnp.einsum('bld,de->ble', merged, w_out)


def expert_dispatch(ht, expert_idx, weights, w_e_gate, w_e_up, w_e_down):
    n_tok, d = ht.shape
    n_assign = n_tok * TOP_K
    cap = n_assign + N_EXPERTS * MOE_BLOCK
    n_blk = cap // MOE_BLOCK
    flat_e = expert_idx.reshape(-1).astype(jnp.int32)
    flat_w = weights.reshape(-1)
    order = jnp.argsort(flat_e, stable=True)
    sorted_e = flat_e[order]
    tok_sorted = order // TOP_K
    counts = jnp.bincount(flat_e, length=N_EXPERTS)
    padded = ((counts + MOE_BLOCK - 1) // MOE_BLOCK) * MOE_BLOCK
    pad_end = jnp.cumsum(padded)
    pad_start = pad_end - padded
    start = jnp.cumsum(counts) - counts
    rank = jnp.arange(n_assign, dtype=jnp.int32) - start[sorted_e]
    dest = pad_start[sorted_e] + rank
    slot_token = jnp.full((cap,), n_tok, jnp.int32).at[dest].set(tok_sorted)
    ht_pad = jnp.concatenate([ht, jnp.zeros((1, d), ht.dtype)], axis=0)
    xs = ht_pad[slot_token].reshape(n_blk, MOE_BLOCK, d)
    blk_start = jnp.arange(n_blk, dtype=jnp.int32) * MOE_BLOCK
    blk_expert = jnp.minimum(jnp.searchsorted(pad_end, blk_start, side='right'), N_EXPERTS - 1)

    def expert_block(args):
        xb, e = args
        return (jax.nn.silu(xb @ w_e_gate[e]) * (xb @ w_e_up[e])) @ w_e_down[e]

    ys = lax.map(expert_block, (xs, blk_expert)).reshape(cap, d)
    contrib = ys[dest].astype(jnp.float32) * flat_w[order][:, None]
    return jnp.zeros((n_tok, d), jnp.float32).at[tok_sorted].add(contrib)


def hierarchical_moe(h, w_group, b_group, w_router, b_router, w_e_gate, w_e_up, w_e_down):
    b, seq_len, d = h.shape
    n_tok = b * seq_len
    ht = h.reshape(n_tok, d)
    hf = ht.astype(jnp.float32)
    group_prob = jax.nn.softmax(hf @ w_group.astype(jnp.float32) + b_group.astype(jnp.float32), axis=-1)
    p_group, g_idx = lax.top_k(group_prob, 1)
    exp_logits = (hf @ w_router.astype(jnp.float32) + b_router.astype(jnp.float32)
                  ).reshape(n_tok, N_GROUPS, EXPERTS_PER_GROUP)
    sel = jnp.broadcast_to(g_idx[:, :, None], (n_tok, 1, EXPERTS_PER_GROUP))
    in_group = jnp.take_along_axis(exp_logits, sel, axis=1)[:, 0]
    top_logits, local_idx = lax.top_k(in_group, TOP_K)
    weights = p_group * jax.nn.softmax(top_logits, axis=-1)
    expert_idx = g_idx * EXPERTS_PER_GROUP + local_idx
    y = expert_dispatch(ht, expert_idx, weights, w_e_gate, w_e_up, w_e_down)
    return y.reshape(b, seq_len, d).astype(h.dtype)


def setup_inputs(seed: int = 0) -> dict:
    key = jax.random.key(seed)
    ks = jax.random.split(key, 40)
    f32 = jnp.float32

    def nrm(k, shape, scale):
        return jax.random.normal(k, shape, f32) * scale

    return {
        'x_prompt': nrm(ks[0], (BATCH, SEQ, D_MODEL), 1.0),
        'x_sample': nrm(ks[1], (DEC_BATCH, DEC_SEQ, D_MODEL), 1.0),
        'c_prompt': nrm(ks[2], (BATCH, D_MODEL), 1.0),
        'c_sample': nrm(ks[3], (DEC_BATCH, D_MODEL), 1.0),
        'w_ada': nrm(ks[4], (DEPTH, D_MODEL, 6 * D_MODEL), 0.5 * D_MODEL ** -0.5),
        'b_ada': nrm(ks[5], (DEPTH, 6 * D_MODEL), 0.02),
        'norm_mix': 1.0 + nrm(ks[6], (DEPTH, D_MODEL), 0.1),
        'norm_ffn': 1.0 + nrm(ks[7], (DEPTH, D_MODEL), 0.1),
        'w_in': nrm(ks[8], (DEPTH, D_MODEL, IN_COLS), D_MODEL ** -0.5),
        'q_gain': 1.0 + nrm(ks[9], (DEPTH, HEAD_DIM), 0.1),
        'k_gain': 1.0 + nrm(ks[10], (DEPTH, HEAD_DIM), 0.1),
        'conv_w': nrm(ks[11], (DEPTH, SHORT_CONV, 3 * D_HYENA), 0.5),
        'conv_b': nrm(ks[12], (DEPTH, 3 * D_HYENA), 0.02),
        'filt_w1': nrm(ks[13], (DEPTH, FILTER_EMB, FILTER_ORDER), FILTER_EMB ** -0.5),
        'filt_b1': nrm(ks[14], (DEPTH, FILTER_ORDER), 0.1),
        'filt_w2': nrm(ks[15], (DEPTH, FILTER_ORDER, FILTER_ORDER), FILTER_ORDER ** -0.5),
        'filt_b2': nrm(ks[16], (DEPTH, FILTER_ORDER), 0.1),
        'filt_w3': nrm(ks[17], (DEPTH, FILTER_ORDER, FILTER_ORDER), FILTER_ORDER ** -0.5),
        'filt_b3': nrm(ks[18], (DEPTH, FILTER_ORDER), 0.1),
        'filt_freq': 1.0 + nrm(ks[19], (DEPTH, FILTER_ORDER), 0.1),
        'filt_w4': nrm(ks[20], (DEPTH, FILTER_ORDER, 2 * D_HYENA), 0.05 * FILTER_ORDER ** -0.5),
        'hyena_skip': nrm(ks[21], (DEPTH, D_HYENA), 1.0),
        'w_br_attn': nrm(ks[22], (DEPTH, ATTN_WIDTH, D_MODEL), ATTN_WIDTH ** -0.5),
        'w_br_hyena': nrm(ks[23], (DEPTH, D_HYENA, D_MODEL), D_HYENA ** -0.5),
        'w_out': nrm(ks[24], (DEPTH, D_MODEL, D_MODEL), D_MODEL ** -0.5),
        'w_group': nrm(ks[25], (DEPTH, D_MODEL, N_GROUPS), D_MODEL ** -0.5),
        'b_group': nrm(ks[26], (DEPTH, N_GROUPS), 0.01),
        'w_router': nrm(ks[27], (DEPTH, D_MODEL, N_EXPERTS), D_MODEL ** -0.5),
        'b_router': nrm(ks[28], (DEPTH, N_EXPERTS), 0.01),
        'w_e_gate': nrm(ks[29], (DEPTH, N_EXPERTS, D_MODEL, D_EXPERT), D_MODEL ** -0.5),
        'w_e_up': nrm(ks[30], (DEPTH, N_EXPERTS, D_MODEL, D_EXPERT), D_MODEL ** -0.5),
        'w_e_down': nrm(ks[31], (DEPTH, N_EXPERTS, D_EXPERT, D_MODEL), D_EXPERT ** -0.5),
    }


def reference(x_prompt, x_sample, c_prompt, c_sample, w_ada, b_ada, norm_mix, norm_ffn, w_in,
              q_gain, k_gain, conv_w, conv_b, filt_w1, filt_b1, filt_w2, filt_b2, filt_w3, filt_b3,
              filt_freq, filt_w4, hyena_skip, w_br_attn, w_br_hyena, w_out, w_group, b_group,
              w_router, b_router, w_e_gate, w_e_up, w_e_down):
    def run_trunk(x, c):
        seq_len = x.shape[1]
        cos, sin = axial_rope_tables(seq_len)
        c_act = jax.nn.silu(c)
        for l in range(DEPTH):
            mod = c_act @ w_ada[l] + b_ada[l]
            shift1, scale1, gate1, shift2, scale2, gate2 = jnp.split(mod, 6, axis=-1)
            h = modulate(rms_norm(x, norm_mix[l]), shift1, scale1, x.dtype)
            mix = token_mixer(h, cos, sin, w_in[l], q_gain[l], k_gain[l], conv_w[l], conv_b[l],
                              filt_w1[l], filt_b1[l], filt_w2[l], filt_b2[l], filt_w3[l], filt_b3[l],
                              filt_freq[l], filt_w4[l], hyena_skip[l], w_br_attn[l], w_br_hyena[l],
                              w_out[l])
            x = x + (gate1[:, None, :] * mix).astype(x.dtype)
            h = modulate(rms_norm(x, norm_ffn[l]), shift2, scale2, x.dtype)
            ffn = hierarchical_moe(h, w_group[l], b_group[l], w_router[l], b_router[l],
                                   w_e_gate[l], w_e_up[l], w_e_down[l])
            x = x + (gate2[:, None, :] * ffn).astype(x.dtype)
        return x

    y_prompt = run_trunk(x_prompt, c_prompt)
    y_sample = run_trunk(x_sample, c_sample)
    return (y_prompt, y_sample)
```

```python
import functools
import math

import numpy as np
import jax
import jax.numpy as jnp
from jax import lax
from jax.experimental import pallas as pl
from jax.experimental.pallas import tpu as pltpu

F32 = jnp.float32
BF16 = jnp.bfloat16

D_MODEL = 1024
GRID_W = 64
N_HEADS = 8
N_KV_HEADS = 4
HEAD_DIM = 64
Q_PER_KV = N_HEADS // N_KV_HEADS
ATTN_WIDTH = N_HEADS * HEAD_DIM
KV_WIDTH = N_KV_HEADS * HEAD_DIM
ROPE_THETA = 10000.0
ROPE_PAIRS_PER_AXIS = HEAD_DIM // 4
D_HYENA = D_MODEL // 2
FILTER_EMB = 33
FILTER_BANDS = (FILTER_EMB - 1) // 2
FILTER_ORDER = 64
DECAY_TARGET = 1e-2
MAX_DECAY = math.log(DECAY_TARGET) / 0.3
MIN_DECAY = math.log(DECAY_TARGET) / 1.5
MOD_SHIFT = 0.05
N_GROUPS = 4
EXPERTS_PER_GROUP = 8
N_EXPERTS = N_GROUPS * EXPERTS_PER_GROUP
TOP_K = 2
D_EXPERT = D_MODEL // 4
RMS_EPS = 1e-6
IN_COLS = ATTN_WIDTH + 2 * KV_WIDTH + 3 * D_HYENA + 2 * D_MODEL
COL_K = ATTN_WIDTH
COL_V = ATTN_WIDTH + KV_WIDTH
COL_U = ATTN_WIDTH + 2 * KV_WIDTH
COL_G = COL_U + 3 * D_HYENA

LANES = 128
SUBLANES = 8
VMEM_LIMIT = 48 * 1024 * 1024

FFT_N2 = 128
EXPERT_BLOCK = 256
NEG_BIG = -1e30

_HI = lax.Precision.HIGHEST


def _dot32(a, b):
    return jnp.dot(a, b, precision=_HI, preferred_element_type=F32)


def _dotbf(a, b):
    return jnp.dot(a, b, preferred_element_type=F32)


def _cparams(sem):
    return pltpu.CompilerParams(dimension_semantics=sem, vmem_limit_bytes=VMEM_LIMIT)


def _ada_kernel(c_ref, w_ref, b_ref, o_ref):
    c = c_ref[...]
    act = c * jax.nn.sigmoid(c)
    o_ref[0] = _dot32(act, w_ref[0]) + b_ref[0]


def _ada(c_pad, w_ada, b_ada):
    depth, d, n = w_ada.shape
    tn = 1536
    return pl.pallas_call(
        _ada_kernel,
        out_shape=jax.ShapeDtypeStruct((depth, c_pad.shape[0], n), F32),
        grid=(depth, n // tn),
        in_specs=[pl.BlockSpec(c_pad.shape, lambda l, j: (0, 0)),
                  pl.BlockSpec((1, d, tn), lambda l, j: (l, 0, j)),
                  pl.BlockSpec((1, 1, tn), lambda l, j: (l, 0, j))],
        out_specs=pl.BlockSpec((1, c_pad.shape[0], tn), lambda l, j: (l, 0, j)),
        compiler_params=_cparams(("arbitrary", "arbitrary")),
        name="ada_mod",
    )(c_pad, w_ada, b_ada.reshape(depth, 1, n))


def _swap_pairs(x):
    n = x.shape[-1]
    lane = lax.broadcasted_iota(jnp.int32, x.shape, 1)
    nxt = pltpu.roll(x, n - 1, 1)
    prv = pltpu.roll(x, 1, 1)
    return jnp.where(lane % 2 == 0, nxt, prv)


def _head_norm_rope(p, gain, bd, cos, sin_signed):
    sq = (p * p).astype(BF16)
    ms = _dotbf(sq, bd) * (1.0 / HEAD_DIM)
    pn = p * lax.rsqrt(ms + RMS_EPS) * gain
    return pn * cos + _swap_pairs(pn) * sin_signed


def _inproj_kernel(x_ref, shift_ref, scale_ref, gain_ref, w_ref, qg_ref, kg_ref, bd_ref,
                   cos_ref, sin_ref, q_ref, k_ref, v_ref, u_ref, g_ref):
    x = x_ref[...]
    ms = jnp.mean(x * x, axis=-1, keepdims=True)
    h = x * lax.rsqrt(ms + RMS_EPS) * gain_ref[...]
    h = h * (1.0 + scale_ref[0]) + shift_ref[0]
    hb = h.astype(BF16)
    cos = cos_ref[...]
    sin = sin_ref[...]
    bd = bd_ref[...]

    q = _dotbf(hb, w_ref[:, 0:COL_K])
    q = _head_norm_rope(q, qg_ref[...], bd, cos, sin)
    for hd in range(N_HEADS):
        q_ref[0, hd] = q[:, hd * HEAD_DIM:(hd + 1) * HEAD_DIM].astype(BF16)

    k = _dotbf(hb, w_ref[:, COL_K:COL_V])
    k = _head_norm_rope(k, kg_ref[...], bd[:KV_WIDTH, :KV_WIDTH], cos[:, :KV_WIDTH], sin[:, :KV_WIDTH])
    v = _dotbf(hb, w_ref[:, COL_V:COL_U])
    for hd in range(N_KV_HEADS):
        k_ref[0, hd] = k[:, hd * HEAD_DIM:(hd + 1) * HEAD_DIM].astype(BF16)
        v_ref[0, hd] = v[:, hd * HEAD_DIM:(hd + 1) * HEAD_DIM].astype(BF16)

    u_ref[...] = _dotbf(hb, w_ref[:, COL_U:COL_G])
    g_ref[...] = jax.nn.sigmoid(_dotbf(hb, w_ref[:, COL_G:IN_COLS])).astype(BF16)


def _inproj(x, shift, scale, gain, w_in, qg, kg, bd, cos, sin, nb, seq):
    t = nb * seq
    tm = min(512, seq)
    tps = seq // tm
    row = lambda i: (i, 0)
    per_b = lambda i: (i // tps, 0, 0)
    const2 = lambda i: (0, 0)
    pos = lambda i: (i % tps, 0)
    head_out = lambda i: (i // tps, 0, i % tps, 0)
    return pl.pallas_call(
        _inproj_kernel,
        out_shape=(jax.ShapeDtypeStruct((nb, N_HEADS, seq, HEAD_DIM), BF16),
                   jax.ShapeDtypeStruct((nb, N_KV_HEADS, seq, HEAD_DIM), BF16),
                   jax.ShapeDtypeStruct((nb, N_KV_HEADS, seq, HEAD_DIM), BF16),
                   jax.ShapeDtypeStruct((t, 3 * D_HYENA), F32),
                   jax.ShapeDtypeStruct((t, 2 * D_MODEL), BF16)),
        grid=(t // tm,),
        in_specs=[pl.BlockSpec((tm, D_MODEL), row),
                  pl.BlockSpec((1, 1, D_MODEL), per_b),
                  pl.BlockSpec((1, 1, D_MODEL), per_b),
                  pl.BlockSpec((1, D_MODEL), const2),
                  pl.BlockSpec((D_MODEL, IN_COLS), const2),
                  pl.BlockSpec((1, ATTN_WIDTH), const2),
                  pl.BlockSpec((1, KV_WIDTH), const2),
                  pl.BlockSpec((ATTN_WIDTH, ATTN_WIDTH), const2),
                  pl.BlockSpec((tm, ATTN_WIDTH), pos),
                  pl.BlockSpec((tm, ATTN_WIDTH), pos)],
        out_specs=(pl.BlockSpec((1, N_HEADS, tm, HEAD_DIM), head_out),
                   pl.BlockSpec((1, N_KV_HEADS, tm, HEAD_DIM), head_out),
                   pl.BlockSpec((1, N_KV_HEADS, tm, HEAD_DIM), head_out),
                   pl.BlockSpec((tm, 3 * D_HYENA), row),
                   pl.BlockSpec((tm, 2 * D_MODEL), row)),
        compiler_params=_cparams(("arbitrary",)),
        name="in_proj",
    )(x, shift, scale, gain, w_in, qg, kg, bd, cos, sin)


def _attn_kernel(q_ref, k_ref, v_ref, o_ref, m_ref, l_ref, acc_ref):
    ki = pl.program_id(3)
    tq = q_ref.shape[2]

    @pl.when(ki == 0)
    def _():
        m_ref[...] = jnp.full(m_ref.shape, NEG_BIG, F32)
        l_ref[...] = jnp.zeros(l_ref.shape, F32)
        acc_ref[...] = jnp.zeros(acc_ref.shape, F32)

    q = q_ref[0].reshape(Q_PER_KV * tq, HEAD_DIM)
    s = lax.dot_general(q, k_ref[0, 0], (((1,), (1,)), ((), ())), preferred_element_type=F32)
    m_prev = m_ref[...]
    m_new = jnp.maximum(m_prev, jnp.max(s, axis=-1, keepdims=True))
    alpha = jnp.exp(m_prev - m_new)
    p = jnp.exp(s - m_new)
    l_ref[...] = alpha * l_ref[...] + jnp.sum(p, axis=-1, keepdims=True)
    acc_ref[...] = alpha * acc_ref[...] + _dotbf(p.astype(BF16), v_ref[0, 0])
    m_ref[...] = m_new

    @pl.when(ki == pl.num_programs(3) - 1)
    def _():
        o = acc_ref[...] / l_ref[...]
        o_ref[0] = jnp.concatenate([o[g * tq:(g + 1) * tq] for g in range(Q_PER_KV)],
                                   axis=1).astype(BF16)


def _attention(q, k, v):
    nb, _, seq, _ = q.shape
    tq = min(512, seq)
    tk = min(1024, seq)
    return pl.pallas_call(
        _attn_kernel,
        out_shape=jax.ShapeDtypeStruct((nb, seq, ATTN_WIDTH), BF16),
        grid=(nb, N_KV_HEADS, seq // tq, seq // tk),
        in_specs=[pl.BlockSpec((1, Q_PER_KV, tq, HEAD_DIM), lambda b, g, i, j: (b, g, i, 0)),
                  pl.BlockSpec((1, 1, tk, HEAD_DIM), lambda b, g, i, j: (b, g, j, 0)),
                  pl.BlockSpec((1, 1, tk, HEAD_DIM), lambda b, g, i, j: (b, g, j, 0))],
        out_specs=pl.BlockSpec((1, tq, Q_PER_KV * HEAD_DIM), lambda b, g, i, j: (b, i, g)),
        scratch_shapes=[pltpu.VMEM((Q_PER_KV * tq, 1), F32),
                        pltpu.VMEM((Q_PER_KV * tq, 1), F32),
                        pltpu.VMEM((Q_PER_KV * tq, HEAD_DIM), F32)],
        compiler_params=_cparams(("arbitrary", "arbitrary", "arbitrary", "arbitrary")),
        name="flash_attn",
    )(q, k, v)


def _conv_kernel(u_ref, up_ref, un_ref, w_ref, b_ref, z_ref, x2_ref):
    i = pl.program_id(1)
    last = pl.num_programs(1) - 1
    u = u_ref[0]
    tr = u.shape[0]
    prev_row = up_ref[0][SUBLANES - 1:SUBLANES] * (i > 0).astype(F32)
    next_row = un_ref[0][0:1] * (i < last).astype(F32)
    ridx = lax.broadcasted_iota(jnp.int32, u.shape, 0)
    u_prev = jnp.where(ridx == 0, prev_row, pltpu.roll(u, 1, 0))
    u_next = jnp.where(ridx == tr - 1, next_row, pltpu.roll(u, tr - 1, 0))
    w = w_ref[...]
    c = b_ref[...] + u_prev * w[0:1] + u * w[1:2] + u_next * w[2:3]
    x1 = c[:, 0:D_HYENA]
    x2 = c[:, D_HYENA:2 * D_HYENA]
    vh = c[:, 2 * D_HYENA:3 * D_HYENA]
    z_ref[0] = vh * x1
    x2_ref[0] = x2


def _short_conv(u, conv_w, conv_b):
    nb, seq, ch = u.shape
    tr = min(512, seq)
    nblk8 = seq // SUBLANES
    rpb = tr // SUBLANES
    return pl.pallas_call(
        _conv_kernel,
        out_shape=(jax.ShapeDtypeStruct((nb, seq, D_HYENA), F32),
                   jax.ShapeDtypeStruct((nb, seq, D_HYENA), F32)),
        grid=(nb, seq // tr),
        in_specs=[pl.BlockSpec((1, tr, ch), lambda b, i: (b, i, 0)),
                  pl.BlockSpec((1, SUBLANES, ch), lambda b, i: (b, jnp.maximum(i * rpb - 1, 0), 0)),
                  pl.BlockSpec((1, SUBLANES, ch),
                               lambda b, i: (b, jnp.minimum((i + 1) * rpb, nblk8 - 1), 0)),
                  pl.BlockSpec((3, ch), lambda b, i: (0, 0)),
                  pl.BlockSpec((1, ch), lambda b, i: (0, 0))],
        out_specs=(pl.BlockSpec((1, tr, D_HYENA), lambda b, i: (b, i, 0)),
                   pl.BlockSpec((1, tr, D_HYENA), lambda b, i: (b, i, 0))),
        compiler_params=_cparams(("arbitrary", "arbitrary")),
        name="short_conv",
    )(u, u, u, conv_w, conv_b.reshape(1, ch))


def _filter_kernel(z_ref, w1_ref, b1_ref, w2_ref, b2_ref, w3_ref, b3_ref, fr_ref, w4_ref,
                   dl_ref, o_ref):
    z = z_ref[0]
    fr = fr_ref[...]
    h = jnp.sin(fr * (_dot32(z, w1_ref[...]) + b1_ref[...]))
    h = jnp.sin(fr * (_dot32(h, w2_ref[...]) + b2_ref[...]))
    h = jnp.sin(fr * (_dot32(h, w3_ref[...]) + b3_ref[...]))
    h = _dot32(h, w4_ref[...])
    t = z[:, 0:1]
    sign = z[:, FILTER_EMB:FILTER_EMB + 1]
    o_ref[0] = h * (jnp.exp(-t * dl_ref[...]) + MOD_SHIFT) * sign


def _filters(zfeat, w1p, b1, w2, b2, w3, b3, fr, w4, deltas):
    _, seq, fe = zfeat.shape
    tm = min(512, seq)
    c2 = lambda d, i: (0, 0)
    return pl.pallas_call(
        _filter_kernel,
        out_shape=jax.ShapeDtypeStruct((2, seq, D_HYENA), F32),
        grid=(2, seq // tm),
        in_specs=[pl.BlockSpec((1, tm, fe), lambda d, i: (d, i, 0)),
                  pl.BlockSpec(w1p.shape, c2), pl.BlockSpec(b1.shape, c2),
                  pl.BlockSpec(w2.shape, c2), pl.BlockSpec(b2.shape, c2),
                  pl.BlockSpec(w3.shape, c2), pl.BlockSpec(b3.shape, c2),
                  pl.BlockSpec(fr.shape, c2),
                  pl.BlockSpec((FILTER_ORDER, D_HYENA), lambda d, i: (0, d)),
                  pl.BlockSpec(deltas.shape, c2)],
        out_specs=pl.BlockSpec((1, tm, D_HYENA), lambda d, i: (d, i, 0)),
        compiler_params=_cparams(("arbitrary", "arbitrary")),
        name="hyena_filters",
    )(zfeat, w1p, b1, w2, b2, w3, b3, fr, w4, deltas)


def _fft_outer_kernel(g_ref, x_ref, o_ref):
    o_ref[0] = _dot32(g_ref[...], x_ref[0])


def _fft_outer(gmat, x):
    nb, kk, cols = x.shape
    m = gmat.shape[0]
    tc = min(4096, cols)
    return pl.pallas_call(
        _fft_outer_kernel,
        out_shape=jax.ShapeDtypeStruct((nb, m, cols), F32),
        grid=(nb, cols // tc),
        in_specs=[pl.BlockSpec((m, kk), lambda b, j: (0, 0)),
                  pl.BlockSpec((1, kk, tc), lambda b, j: (b, 0, j))],
        out_specs=pl.BlockSpec((1, m, tc), lambda b, j: (b, 0, j)),
        compiler_params=_cparams(("arbitrary", "arbitrary")),
        name="fft_outer",
    )(gmat, x)


def _fft_filter_inner_kernel(mf_ref, a_ref, o_ref):
    n2, ch = a_ref.shape[3], a_ref.shape[4]
    a = a_ref[0, :, 0].reshape(2 * n2, ch)
    o_ref[0] = _dot32(mf_ref[0], a).reshape(2, n2, ch)


def _fft_filter_inner(mf, a5):
    _, _, h1, n2, ch = a5.shape
    return pl.pallas_call(
        _fft_filter_inner_kernel,
        out_shape=jax.ShapeDtypeStruct((h1, 2, n2, ch), F32),
        grid=(h1,),
        in_specs=[pl.BlockSpec((1, 2 * n2, 2 * n2), lambda k: (k, 0, 0)),
                  pl.BlockSpec((1, 2, 1, n2, ch), lambda k: (0, 0, k, 0, 0))],
        out_specs=pl.BlockSpec((1, 2, n2, ch), lambda k: (k, 0, 0, 0)),
        compiler_params=_cparams(("arbitrary",)),
        name="fft_filter_inner",
    )(mf, a5)


def _fft_inner_kernel(mf_ref, mi_ref, kf_ref, a_ref, o_ref):
    n2, ch = a_ref.shape[3], a_ref.shape[4]
    a = a_ref[0, :, 0].reshape(2 * n2, ch)
    xs = _dot32(mf_ref[0], a)
    xr, xi = xs[:n2], xs[n2:]
    kr, ki = kf_ref[0, 0], kf_ref[0, 1]
    p = jnp.concatenate([xr * kr - xi * ki, xr * ki + xi * kr], axis=0)
    o_ref[0, :, 0] = _dot32(mi_ref[0], p).reshape(2, n2, ch)


def _fft_inner(mf, mi, kf, a5):
    nb, _, h1, n2, ch = a5.shape
    return pl.pallas_call(
        _fft_inner_kernel,
        out_shape=jax.ShapeDtypeStruct(a5.shape, F32),
        grid=(h1, nb),
        in_specs=[pl.BlockSpec((1, 2 * n2, 2 * n2), lambda k, b: (k, 0, 0)),
                  pl.BlockSpec((1, 2 * n2, 2 * n2), lambda k, b: (k, 0, 0)),
                  pl.BlockSpec((1, 2, n2, ch), lambda k, b: (k, 0, 0, 0)),
                  pl.BlockSpec((1, 2, 1, n2, ch), lambda k, b: (b, 0, k, 0, 0))],
        out_specs=pl.BlockSpec((1, 2, 1, n2, ch), lambda k, b: (b, 0, k, 0, 0)),
        compiler_params=_cparams(("arbitrary", "arbitrary")),
        name="fft_inner",
    )(mf, mi, kf, a5)


def _fft_final_kernel(g_ref, b_ref, z_ref, x2_ref, skip_ref, o_ref):
    y = _dot32(g_ref[...], b_ref[0])
    o_ref[0] = (x2_ref[0] * (y + z_ref[0] * skip_ref[...])).astype(BF16)


def _fft_final(gc, bm, z, x2, skip_row):
    nb, h1, cols = z.shape
    tc = min(4096, cols)
    return pl.pallas_call(
        _fft_final_kernel,
        out_shape=jax.ShapeDtypeStruct((nb, h1, cols), BF16),
        grid=(nb, cols // tc),
        in_specs=[pl.BlockSpec((h1, 2 * h1), lambda b, j: (0, 0)),
                  pl.BlockSpec((1, 2 * h1, tc), lambda b, j: (b, 0, j)),
                  pl.BlockSpec((1, h1, tc), lambda b, j: (b, 0, j)),
                  pl.BlockSpec((1, h1, tc), lambda b, j: (b, 0, j)),
                  pl.BlockSpec((1, tc), lambda b, j: (0, 0))],
        out_specs=pl.BlockSpec((1, h1, tc), lambda b, j: (b, 0, j)),
        compiler_params=_cparams(("arbitrary", "arbitrary")),
        name="fft_final",
    )(gc, bm, z, x2, skip_row)


def _dft_tables(seq):
    n = 2 * seq
    n2 = FFT_N2
    n1 = n // n2
    h1 = n1 // 2
    i32 = jnp.int32
    k1 = jnp.arange(h1, dtype=i32)[:, None]
    a1 = jnp.arange(n1, dtype=i32)[None, :]
    th = (math.pi / n1) * ((a1 * (2 * k1 + 1)) % (2 * n1)).astype(F32)
    g_re, g_im = jnp.cos(th), -jnp.sin(th)
    g_full = jnp.concatenate([g_re, g_im], axis=0)
    g_half = g_full[:, :h1]
    g_out = (2.0 / n) * jnp.concatenate([g_re[:, :h1].T, g_im[:, :h1].T], axis=1)
    k2 = jnp.arange(n2, dtype=i32)[None, :, None]
    b2 = jnp.arange(n2, dtype=i32)[None, None, :]
    kk = jnp.arange(h1, dtype=i32)[:, None, None]
    ph = (math.pi / n) * ((b2 * (2 * kk + 1 + 2 * n1 * k2)) % (2 * n)).astype(F32)
    m_re, m_im = jnp.cos(ph), -jnp.sin(ph)
    mf = jnp.concatenate([jnp.concatenate([m_re, -m_im], axis=2),
                          jnp.concatenate([m_im, m_re], axis=2)], axis=1)
    mt_re, mt_im = jnp.transpose(m_re, (0, 2, 1)), jnp.transpose(m_im, (0, 2, 1))
    mi = jnp.concatenate([jnp.concatenate([mt_re, mt_im], axis=2),
                          jnp.concatenate([-mt_im, mt_re], axis=2)], axis=1)
    return dict(n1=n1, h1=h1, g_full=g_full, g_half=g_half, g_out=g_out, mf=mf, mi=mi)


def _merge_kernel(attn_ref, hy_ref, g_ref, x_ref, gate_ref, shift_ref, scale_ref, gain_ref,
                  wa_ref, wh_ref, wo_ref, wrh_ref, wrl_ref, br_ref,
                  x1_ref, h2_ref, route_ref, cnt_ref):
    g = g_ref[...].astype(F32)
    merged = (g[:, :D_MODEL] * _dotbf(attn_ref[...], wa_ref[...])
              + g[:, D_MODEL:] * _dotbf(hy_ref[...], wh_ref[...]))
    mix = _dotbf(merged.astype(BF16), wo_ref[...])
    x1 = x_ref[...] + gate_ref[0] * mix
    x1_ref[...] = x1

    ms = jnp.mean(x1 * x1, axis=-1, keepdims=True)
    h2 = x1 * lax.rsqrt(ms + RMS_EPS) * gain_ref[...]
    h2 = h2 * (1.0 + scale_ref[0]) + shift_ref[0]
    h2_ref[...] = h2

    hi = h2.astype(BF16)
    lo = (h2 - hi.astype(F32)).astype(BF16)
    wrh = wrh_ref[...]
    lg = _dotbf(hi, wrh) + _dotbf(lo, wrh) + _dotbf(hi, wrl_ref[...]) + br_ref[...]

    lane = lax.broadcasted_iota(jnp.int32, lg.shape, 1).astype(F32)
    is_grp = jnp.logical_and(lane >= N_EXPERTS, lane < N_EXPERTS + N_GROUPS)
    gm = jnp.where(is_grp, lg, NEG_BIG)
    gmax = jnp.max(gm, axis=-1, keepdims=True)
    gidx = jnp.min(jnp.where(gm == gmax, lane, 1e9), axis=-1, keepdims=True) - N_EXPERTS
    p_group = 1.0 / jnp.sum(jnp.where(is_grp, jnp.exp(gm - gmax), 0.0), axis=-1, keepdims=True)
    lo_lane = gidx * EXPERTS_PER_GROUP
    in_grp = jnp.logical_and(lane >= lo_lane, lane < lo_lane + EXPERTS_PER_GROUP)
    e1v = jnp.where(in_grp, lg, NEG_BIG)
    t1 = jnp.max(e1v, axis=-1, keepdims=True)
    i1 = jnp.min(jnp.where(e1v == t1, lane, 1e9), axis=-1, keepdims=True)
    e2v = jnp.where(lane == i1, NEG_BIG, e1v)
    t2 = jnp.max(e2v, axis=-1, keepdims=True)
    i2 = jnp.min(jnp.where(e2v == t2, lane, 1e9), axis=-1, keepdims=True)
    d = jnp.exp(t2 - t1)
    w1 = p_group / (1.0 + d)
    w2 = p_group * d / (1.0 + d)
    route_ref[...] = jnp.where(lane == 0, i1, jnp.where(lane == 1, i2,
                               jnp.where(lane == 2, w1, jnp.where(lane == 3, w2, 0.0))))

    onehot = (lane == i1).astype(F32) + (lane == i2).astype(F32)

    @pl.when(pl.program_id(0) == 0)
    def _():
        cnt_ref[...] = jnp.zeros(cnt_ref.shape, F32)

    cnt_ref[...] += jnp.sum(onehot, axis=0, keepdims=True)


def _merge(attn, hy, gates, x, gate1, shift2, scale2, gain, wa, wh, wo, wrh, wrl, br, nb, seq):
    t = nb * seq
    tm = min(512, seq)
    tps = seq // tm
    row = lambda i: (i, 0)
    per_b = lambda i: (i // tps, 0, 0)
    c2 = lambda i: (0, 0)
    return pl.pallas_call(
        _merge_kernel,
        out_shape=(jax.ShapeDtypeStruct((t, D_MODEL), F32),
                   jax.ShapeDtypeStruct((t, D_MODEL), F32),
                   jax.ShapeDtypeStruct((t, LANES), F32),
                   jax.ShapeDtypeStruct((1, LANES), F32)),
        grid=(t // tm,),
        in_specs=[pl.BlockSpec((tm, ATTN_WIDTH), row),
                  pl.BlockSpec((tm, D_HYENA), row),
                  pl.BlockSpec((tm, 2 * D_MODEL), row),
                  pl.BlockSpec((tm, D_MODEL), row),
                  pl.BlockSpec((1, 1, D_MODEL), per_b),
                  pl.BlockSpec((1, 1, D_MODEL), per_b),
                  pl.BlockSpec((1, 1, D_MODEL), per_b),
                  pl.BlockSpec((1, D_MODEL), c2),
                  pl.BlockSpec(wa.shape, c2), pl.BlockSpec(wh.shape, c2), pl.BlockSpec(wo.shape, c2),
                  pl.BlockSpec(wrh.shape, c2), pl.BlockSpec(wrl.shape, c2), pl.BlockSpec(br.shape, c2)],
        out_specs=(pl.BlockSpec((tm, D_MODEL), row),
                   pl.BlockSpec((tm, D_MODEL), row),
                   pl.BlockSpec((tm, LANES), row),
                   pl.BlockSpec((1, LANES), c2)),
        compiler_params=_cparams(("arbitrary",)),
        name="merge_router",
    )(attn, hy, gates, x, gate1, shift2, scale2, gain, wa, wh, wo, wrh, wrl, br)


def _rank_kernel(route_ref, pstart_ref, tri_ref, dest_ref, carry_ref):
    @pl.when(pl.program_id(0) == 0)
    def _():
        carry_ref[...] = jnp.zeros(carry_ref.shape, F32)

    r = route_ref[...]
    lane = lax.broadcasted_iota(jnp.int32, r.shape, 1).astype(F32)
    oh1 = (lane == r[:, 0:1]).astype(F32)
    oh2 = (lane == r[:, 1:2]).astype(F32)
    tri = tri_ref[...]
    before1 = _dotbf(tri, oh1.astype(BF16))
    before2 = _dotbf(tri, oh2.astype(BF16))
    base1 = pstart_ref[...] + carry_ref[...]
    d1 = jnp.sum(oh1 * (base1 + before1), axis=-1, keepdims=True)
    base2 = base1 + jnp.sum(oh1, axis=0, keepdims=True)
    d2 = jnp.sum(oh2 * (base2 + before2), axis=-1, keepdims=True)
    carry_ref[...] = base2 + jnp.sum(oh2, axis=0, keepdims=True) - pstart_ref[...]
    dest_ref[...] = jnp.where(lane == 0, d1, jnp.where(lane == 1, d2, 0.0))


def _rank(route, pstart, tri):
    t = route.shape[0]
    tm = tri.shape[0]
    return pl.pallas_call(
        _rank_kernel,
        out_shape=jax.ShapeDtypeStruct((t, LANES), F32),
        grid=(t // tm,),
        in_specs=[pl.BlockSpec((tm, LANES), lambda i: (i, 0)),
                  pl.BlockSpec((1, LANES), lambda i: (0, 0)),
                  pl.BlockSpec((tm, tm), lambda i: (0, 0))],
        out_specs=pl.BlockSpec((tm, LANES), lambda i: (i, 0)),
        scratch_shapes=[pltpu.VMEM((1, LANES), F32)],
        compiler_params=_cparams(("arbitrary",)),
        name="moe_rank",
    )(route, pstart, tri)


def _dispatch_kernel(d1_ref, d2_ref, h_ref, init_ref, xs_ref, sem):
    del init_ref
    tm = h_ref.shape[0]
    base = pl.program_id(0) * tm

    def copies(r):
        src = h_ref.at[pl.ds(r, 1)]
        return (pltpu.make_async_copy(src, xs_ref.at[pl.ds(d1_ref[base + r], 1)], sem),
                pltpu.make_async_copy(src, xs_ref.at[pl.ds(d2_ref[base + r], 1)], sem))

    def issue(r, c):
        a, b = copies(r)
        a.start()
        b.start()
        return c

    def drain(r, c):
        a, b = copies(r)
        a.wait()
        b.wait()
        return c

    lax.fori_loop(0, tm, issue, 0)
    lax.fori_loop(0, tm, drain, 0)


def _dispatch(d1, d2, h2, cap):
    t, d = h2.shape
    tm = min(256, t)
    init = jnp.zeros((cap, d), h2.dtype)
    return pl.pallas_call(
        _dispatch_kernel,
        out_shape=jax.ShapeDtypeStruct((cap, d), h2.dtype),
        grid_spec=pltpu.PrefetchScalarGridSpec(
            num_scalar_prefetch=2,
            grid=(t // tm,),
            in_specs=[pl.BlockSpec((tm, d), lambda i, a, b: (i, 0)),
                      pl.BlockSpec(memory_space=pl.ANY)],
            out_specs=pl.BlockSpec(memory_space=pl.ANY),
            scratch_shapes=[pltpu.SemaphoreType.DMA(())]),
        input_output_aliases={3: 0},
        compiler_params=_cparams(("arbitrary",)),
        name="moe_dispatch",
    )(d1, d2, h2, init)


def _expert_kernel(be_ref, xs_ref, wgu_ref, wd_ref, ys_ref):
    del be_ref
    x = xs_ref[...].astype(BF16)
    gu = _dotbf(x, wgu_ref[0])
    g = gu[:, :D_EXPERT]
    u = gu[:, D_EXPERT:]
    a = (g * jax.nn.sigmoid(g) * u).astype(BF16)
    ys_ref[...] = _dotbf(a, wd_ref[0])


def _experts(blk_expert, xs, wgu, wd):
    cap, d = xs.shape
    return pl.pallas_call(
        _expert_kernel,
        out_shape=jax.ShapeDtypeStruct((cap, d), F32),
        grid_spec=pltpu.PrefetchScalarGridSpec(
            num_scalar_prefetch=1,
            grid=(cap // EXPERT_BLOCK,),
            in_specs=[pl.BlockSpec((EXPERT_BLOCK, d), lambda i, be: (i, 0)),
                      pl.BlockSpec((1, d, 2 * D_EXPERT), lambda i, be: (be[i], 0, 0)),
                      pl.BlockSpec((1, D_EXPERT, d), lambda i, be: (be[i], 0, 0))],
            out_specs=pl.BlockSpec((EXPERT_BLOCK, d), lambda i, be: (i, 0))),
        compiler_params=_cparams(("arbitrary",)),
        name="moe_experts",
    )(blk_expert, xs, wgu, wd)


def _combine_kernel(d1_ref, d2_ref, ys_ref, x_ref, route_ref, gate_ref, o_ref, y1_ref, y2_ref, sem):
    tm = x_ref.shape[0]
    base = pl.program_id(0) * tm

    def copies(r):
        return (pltpu.make_async_copy(ys_ref.at[pl.ds(d1_ref[base + r], 1)], y1_ref.at[pl.ds(r, 1)], sem),
                pltpu.make_async_copy(ys_ref.at[pl.ds(d2_ref[base + r], 1)], y2_ref.at[pl.ds(r, 1)], sem))

    def issue(r, c):
        a, b = copies(r)
        a.start()
        b.start()
        return c

    def drain(r, c):
        a, b = copies(r)
        a.wait()
        b.wait()
        return c

    lax.fori_loop(0, tm, issue, 0)
    lax.fori_loop(0, tm, drain, 0)
    r = route_ref[...]
    ffn = y1_ref[...] * r[:, 2:3] + y2_ref[...] * r[:, 3:4]
    o_ref[...] = x_ref[...] + gate_ref[0] * ffn


def _combine(d1, d2, ys, x1, route, gate2, nb, seq):
    t, d = x1.shape
    tm = min(256, seq)
    tps = seq // tm
    return pl.pallas_call(
        _combine_kernel,
        out_shape=jax.ShapeDtypeStruct((t, d), F32),
        grid_spec=pltpu.PrefetchScalarGridSpec(
            num_scalar_prefetch=2,
            grid=(t // tm,),
            in_specs=[pl.BlockSpec(memory_space=pl.ANY),
                      pl.BlockSpec((tm, d), lambda i, a, b: (i, 0)),
                      pl.BlockSpec((tm, LANES), lambda i, a, b: (i, 0)),
                      pl.BlockSpec((1, 1, d), lambda i, a, b: (i // tps, 0, 0))],
            out_specs=pl.BlockSpec((tm, d), lambda i, a, b: (i, 0)),
            scratch_shapes=[pltpu.VMEM((tm, d), F32), pltpu.VMEM((tm, d), F32),
                            pltpu.SemaphoreType.DMA(())]),
        compiler_params=_cparams(("arbitrary",)),
        name="moe_combine",
    )(d1, d2, ys, x1, route, gate2)


def _rope_tables(seq):
    rows = seq // GRID_W
    row = jnp.broadcast_to(jnp.arange(rows, dtype=F32)[:, None], (rows, GRID_W)).reshape(-1)
    col = jnp.broadcast_to(jnp.arange(GRID_W, dtype=F32)[None, :], (rows, GRID_W)).reshape(-1)
    inv_freq = ROPE_THETA ** (-jnp.arange(ROPE_PAIRS_PER_AXIS, dtype=F32) / ROPE_PAIRS_PER_AXIS)
    ang = jnp.concatenate([row[:, None] * inv_freq, col[:, None] * inv_freq], axis=-1)
    cos = jnp.repeat(jnp.cos(ang), 2, axis=-1)
    sin = jnp.repeat(jnp.sin(ang), 2, axis=-1)
    sign = jnp.tile(jnp.array([-1.0, 1.0], F32), HEAD_DIM // 2)
    return jnp.tile(cos, (1, N_HEADS)), jnp.tile(sin * sign, (1, N_HEADS))


def _filter_features(seq):
    t = jnp.linspace(0.0, 1.0, seq, dtype=F32)[:, None]
    w = (2.0 * math.pi / seq) * jnp.arange(seq, dtype=F32)[:, None]
    bands = jnp.linspace(1e-4, FILTER_BANDS - 1, FILTER_BANDS, dtype=F32)[None, :]
    z = jnp.concatenate([t, jnp.cos(bands * w), -jnp.sin(bands * w)], axis=-1)
    pad = jnp.zeros((seq, FILTER_ORDER - FILTER_EMB), F32)
    fwd = jnp.concatenate([z, pad], axis=-1).at[:, FILTER_EMB].set(1.0)
    rev = jnp.concatenate([z[:1], z[:0:-1]], axis=0)
    sign = jnp.concatenate([jnp.zeros((1,), F32), -jnp.ones((seq - 1,), F32)])
    bwd = jnp.concatenate([rev, pad], axis=-1).at[:, FILTER_EMB].set(sign)
    deltas = jnp.abs(jnp.linspace(MIN_DECAY, MAX_DECAY, D_HYENA, dtype=F32))[None, :]
    return jnp.stack([fwd, bwd]), deltas


def _run_trunk(x, mod, p):
    nb, seq, d = x.shape
    t = nb * seq
    depth = mod.shape[0]
    cos, sin = _rope_tables(seq)
    zfeat, deltas = _filter_features(seq)
    tabs = _dft_tables(seq)
    h1, n1 = tabs["h1"], tabs["n1"]
    cols = FFT_N2 * D_HYENA
    tm_rank = min(512, t)
    tri = jnp.tril(jnp.ones((tm_rank, tm_rank), BF16), -1)
    cap = t * TOP_K + N_EXPERTS * EXPERT_BLOCK
    n_blk = cap // EXPERT_BLOCK
    lane_e = jnp.arange(LANES)

    xf = x.reshape(t, d)
    for l in range(depth):
        m6 = mod[l].reshape(nb, 6, 1, d)
        shift1, scale1, gate1, shift2, scale2, gate2 = (m6[:, j] for j in range(6))

        q, k, v, u, gates = _inproj(xf, shift1, scale1, p["norm_mix"][l], p["w_in"][l], p["qg"][l],
                                    p["kg"][l], p["bd"], cos, sin, nb, seq)
        attn = _attention(q, k, v).reshape(t, ATTN_WIDTH)

        z, x2 = _short_conv(u.reshape(nb, seq, 3 * D_HYENA), p["conv_w"][l], p["conv_b"][l])
        kern = _filters(zfeat, p["filt_w1"][l], p["filt_b1"][l], p["filt_w2"][l], p["filt_b2"][l],
                        p["filt_w3"][l], p["filt_b3"][l], p["filt_freq"][l], p["filt_w4"][l], deltas)
        ka = _fft_outer(tabs["g_full"], kern.reshape(1, n1, cols))
        kf = _fft_filter_inner(tabs["mf"], ka.reshape(1, 2, h1, FFT_N2, D_HYENA))
        za = _fft_outer(tabs["g_half"], z.reshape(nb, h1, cols))
        zb = _fft_inner(tabs["mf"], tabs["mi"], kf, za.reshape(nb, 2, h1, FFT_N2, D_HYENA))
        hy = _fft_final(tabs["g_out"], zb.reshape(nb, 2 * h1, cols), z.reshape(nb, h1, cols),
                        x2.reshape(nb, h1, cols), p["skip_row"][l]).reshape(t, D_HYENA)

        x1, h2, route, counts = _merge(attn, hy, gates, xf, gate1, shift2, scale2, p["norm_ffn"][l],
                                       p["w_br_attn"][l], p["w_br_hyena"][l], p["w_out"][l],
                                       p["wr_hi"][l], p["wr_lo"][l], p["b_route"][l], nb, seq)

        cnt = counts[0].astype(jnp.int32)
        padded = jnp.where(lane_e < N_EXPERTS, (cnt + EXPERT_BLOCK - 1) // EXPERT_BLOCK * EXPERT_BLOCK, 0)
        pad_end = jnp.cumsum(padded)
        pstart = (pad_end - padded).astype(F32)[None, :]
        blk_start = jnp.arange(n_blk, dtype=jnp.int32) * EXPERT_BLOCK
        blk_expert = jnp.minimum(
            jnp.sum(blk_start[:, None] >= pad_end[None, :N_EXPERTS], axis=1), N_EXPERTS - 1).astype(jnp.int32)

        dest = _rank(route, pstart, tri)
        d1 = dest[:, 0].astype(jnp.int32)
        d2 = dest[:, 1].astype(jnp.int32)
        xs = _dispatch(d1, d2, h2, cap)
        ys = _experts(blk_expert, xs, p["w_gu"][l], p["w_down"][l])
        xf = _combine(d1, d2, ys, x1, route, gate2, nb, seq)
    return xf.reshape(nb, seq, d)


def kernel(x_prompt, x_sample, c_prompt, c_sample, w_ada, b_ada, norm_mix, norm_ffn, w_in, q_gain, k_gain, conv_w, conv_b, filt_w1, filt_b1, filt_w2, filt_b2, filt_w3, filt_b3, filt_freq, filt_w4, hyena_skip, w_br_attn, w_br_hyena, w_out, w_group, b_group, w_router, b_router, w_e_gate, w_e_up, w_e_down):
    depth = w_ada.shape[0]
    bp, bs = c_prompt.shape[0], c_sample.shape[0]
    rows = -(-(bp + bs) // SUBLANES) * SUBLANES
    c_pad = jnp.zeros((rows, D_MODEL), F32).at[:bp].set(c_prompt).at[bp:bp + bs].set(c_sample)
    mod = _ada(c_pad, w_ada, b_ada)

    scale = HEAD_DIM ** -0.5
    head_id = np.arange(ATTN_WIDTH) // HEAD_DIM
    route_w = jnp.concatenate([w_router, w_group], axis=-1)
    route_w = jnp.pad(route_w, ((0, 0), (0, 0), (0, LANES - route_w.shape[-1])))
    wr_hi = route_w.astype(BF16)
    p = dict(
        norm_mix=norm_mix.reshape(depth, 1, D_MODEL),
        norm_ffn=norm_ffn.reshape(depth, 1, D_MODEL),
        w_in=w_in.astype(BF16),
        qg=(jnp.tile(q_gain, (1, N_HEADS)) * scale).reshape(depth, 1, ATTN_WIDTH),
        kg=jnp.tile(k_gain, (1, N_KV_HEADS)).reshape(depth, 1, KV_WIDTH),
        bd=jnp.asarray(head_id[:, None] == head_id[None, :], dtype=BF16),
        conv_w=conv_w, conv_b=conv_b,
        filt_w1=jnp.pad(filt_w1, ((0, 0), (0, FILTER_ORDER - FILTER_EMB), (0, 0))),
        filt_b1=filt_b1.reshape(depth, 1, FILTER_ORDER),
        filt_w2=filt_w2, filt_b2=filt_b2.reshape(depth, 1, FILTER_ORDER),
        filt_w3=filt_w3, filt_b3=filt_b3.reshape(depth, 1, FILTER_ORDER),
        filt_freq=filt_freq.reshape(depth, 1, FILTER_ORDER),
        filt_w4=filt_w4,
        skip_row=jnp.tile(hyena_skip, (1, 4096 // D_HYENA)).reshape(depth, 1, 4096),
        w_br_attn=w_br_attn.astype(BF16), w_br_hyena=w_br_hyena.astype(BF16), w_out=w_out.astype(BF16),
        wr_hi=wr_hi, wr_lo=(route_w - wr_hi.astype(F32)).astype(BF16),
        b_route=jnp.pad(jnp.concatenate([b_router, b_group], axis=-1),
                        ((0, 0), (0, LANES - N_EXPERTS - N_GROUPS))).reshape(depth, 1, LANES),
        w_gu=jnp.concatenate([w_e_gate, w_e_up], axis=-1).astype(BF16),
        w_down=w_e_down.astype(BF16),
    )
    y_prompt = _run_trunk(x_prompt, mod[:, :bp], p)
    y_sample = _run_trunk(x_sample, mod[:, bp:bp + bs], p)
    return (y_prompt, y_sample)
```

```python
import functools
import math

import numpy as np
import jax
import jax.numpy as jnp
from jax import lax
from jax.experimental import pallas as pl
from jax.experimental.pallas import tpu as pltpu

F32 = jnp.float32
BF16 = jnp.bfloat16

D_MODEL = 1024
GRID_W = 64
N_HEADS = 8
N_KV_HEADS = 4
HEAD_DIM = 64
Q_PER_KV = N_HEADS // N_KV_HEADS
ATTN_WIDTH = N_HEADS * HEAD_DIM
KV_WIDTH = N_KV_HEADS * HEAD_DIM
ROPE_THETA = 10000.0
ROPE_PAIRS_PER_AXIS = HEAD_DIM // 4
D_HYENA = D_MODEL // 2
FILTER_EMB = 33
FILTER_BANDS = (FILTER_EMB - 1) // 2
FILTER_ORDER = 64
DECAY_TARGET = 1e-2
MAX_DECAY = math.log(DECAY_TARGET) / 0.3
MIN_DECAY = math.log(DECAY_TARGET) / 1.5
MOD_SHIFT = 0.05
N_GROUPS = 4
EXPERTS_PER_GROUP = 8
N_EXPERTS = N_GROUPS * EXPERTS_PER_GROUP
TOP_K = 2
D_EXPERT = D_MODEL // 4
RMS_EPS = 1e-6
IN_COLS = ATTN_WIDTH + 2 * KV_WIDTH + 3 * D_HYENA + 2 * D_MODEL
COL_K = ATTN_WIDTH
COL_V = ATTN_WIDTH + KV_WIDTH
COL_U = ATTN_WIDTH + 2 * KV_WIDTH
COL_G = COL_U + 3 * D_HYENA

LANES = 128
SUBLANES = 8
VMEM_LIMIT = 48 * 1024 * 1024

V_ROWS = HEAD_DIM + 16
SCORE_BOUND = 60.0
FFT_N2 = 128
EXPERT_BLOCK = 256
NEG_BIG = -1e30

_HI = lax.Precision.HIGHEST


def _dot32(a, b):
    return jnp.dot(a, b, precision=_HI, preferred_element_type=F32)


def _dotbf(a, b):
    return jnp.dot(a, b, preferred_element_type=F32)


def _split_bf16(a):
    hi = a.astype(BF16)
    return hi, (a - hi.astype(F32)).astype(BF16)


def _dot3(w_hi, w_lo, x):
    x_hi, x_lo = _split_bf16(x)
    return _dotbf(w_hi, x_hi) + _dotbf(w_lo, x_hi) + _dotbf(w_hi, x_lo)


def _cparams(sem):
    return pltpu.CompilerParams(dimension_semantics=sem, vmem_limit_bytes=VMEM_LIMIT)


def _ada_kernel(c_ref, w_ref, b_ref, o_ref):
    c = c_ref[...]
    act = c * jax.nn.sigmoid(c)
    o_ref[0] = _dot32(act, w_ref[0]) + b_ref[0]


def _ada(c_pad, w_ada, b_ada):
    depth, d, n = w_ada.shape
    tn = 1536
    return pl.pallas_call(
        _ada_kernel,
        out_shape=jax.ShapeDtypeStruct((depth, c_pad.shape[0], n), F32),
        grid=(depth, n // tn),
        in_specs=[pl.BlockSpec(c_pad.shape, lambda l, j: (0, 0)),
                  pl.BlockSpec((1, d, tn), lambda l, j: (l, 0, j)),
                  pl.BlockSpec((1, 1, tn), lambda l, j: (l, 0, j))],
        out_specs=pl.BlockSpec((1, c_pad.shape[0], tn), lambda l, j: (l, 0, j)),
        compiler_params=_cparams(("arbitrary", "arbitrary")),
        name="ada_mod",
    )(c_pad, w_ada, b_ada.reshape(depth, 1, n))


def _swap_pairs(x):
    n = x.shape[-1]
    lane = lax.broadcasted_iota(jnp.int32, x.shape, 1)
    nxt = pltpu.roll(x, n - 1, 1)
    prv = pltpu.roll(x, 1, 1)
    return jnp.where(lane % 2 == 0, nxt, prv)


def _head_norm_rope(p, gain, bd, cos, sin_signed):
    sq = (p * p).astype(BF16)
    ms = _dotbf(sq, bd) * (1.0 / HEAD_DIM)
    pn = p * lax.rsqrt(ms + RMS_EPS) * gain
    return pn * cos + _swap_pairs(pn) * sin_signed


def _inproj_kernel(x_ref, shift_ref, scale_ref, gain_ref, w_ref, wvt_ref, qg_ref, kg_ref, bd_ref,
                   cos_ref, sin_ref, q_ref, k_ref, vt_ref, u_ref, g_ref):
    x = x_ref[...]
    ms = jnp.mean(x * x, axis=-1, keepdims=True)
    h = x * lax.rsqrt(ms + RMS_EPS) * gain_ref[...]
    h = h * (1.0 + scale_ref[0]) + shift_ref[0]
    hb = h.astype(BF16)
    cos = cos_ref[...]
    sin = sin_ref[...]
    bd = bd_ref[...]

    q = _dotbf(hb, w_ref[:, 0:COL_K])
    q = _head_norm_rope(q, qg_ref[...], bd, cos, sin)
    for hd in range(N_HEADS):
        q_ref[0, hd] = q[:, hd * HEAD_DIM:(hd + 1) * HEAD_DIM].astype(BF16)

    k = _dotbf(hb, w_ref[:, COL_K:COL_V])
    k = _head_norm_rope(k, kg_ref[...], bd[:KV_WIDTH, :KV_WIDTH], cos[:, :KV_WIDTH], sin[:, :KV_WIDTH])
    vt = lax.dot_general(wvt_ref[...], hb, (((1,), (1,)), ((), ())), preferred_element_type=F32)
    ones = jnp.ones((V_ROWS - HEAD_DIM, vt.shape[1]), BF16)
    for hd in range(N_KV_HEADS):
        k_ref[0, hd] = k[:, hd * HEAD_DIM:(hd + 1) * HEAD_DIM].astype(BF16)
        vt_ref[0, hd, 0:HEAD_DIM, :] = vt[hd * HEAD_DIM:(hd + 1) * HEAD_DIM].astype(BF16)
        vt_ref[0, hd, HEAD_DIM:V_ROWS, :] = ones

    u_ref[...] = _dotbf(hb, w_ref[:, COL_U:COL_G])
    g_ref[...] = jax.nn.sigmoid(_dotbf(hb, w_ref[:, COL_G:IN_COLS])).astype(BF16)


def _inproj(x, shift, scale, gain, w_in, w_vt, qg, kg, bd, cos, sin, nb, seq):
    t = nb * seq
    tm = min(512, seq)
    tps = seq // tm
    row = lambda i: (i, 0)
    per_b = lambda i: (i // tps, 0, 0)
    const2 = lambda i: (0, 0)
    pos = lambda i: (i % tps, 0)
    head_out = lambda i: (i // tps, 0, i % tps, 0)
    return pl.pallas_call(
        _inproj_kernel,
        out_shape=(jax.ShapeDtypeStruct((nb, N_HEADS, seq, HEAD_DIM), BF16),
                   jax.ShapeDtypeStruct((nb, N_KV_HEADS, seq, HEAD_DIM), BF16),
                   jax.ShapeDtypeStruct((nb, N_KV_HEADS, V_ROWS, seq), BF16),
                   jax.ShapeDtypeStruct((t, 3 * D_HYENA), F32),
                   jax.ShapeDtypeStruct((t, 2 * D_MODEL), BF16)),
        grid=(t // tm,),
        in_specs=[pl.BlockSpec((tm, D_MODEL), row),
                  pl.BlockSpec((1, 1, D_MODEL), per_b),
                  pl.BlockSpec((1, 1, D_MODEL), per_b),
                  pl.BlockSpec((1, D_MODEL), const2),
                  pl.BlockSpec((D_MODEL, IN_COLS), const2),
                  pl.BlockSpec((KV_WIDTH, D_MODEL), const2),
                  pl.BlockSpec((1, ATTN_WIDTH), const2),
                  pl.BlockSpec((1, KV_WIDTH), const2),
                  pl.BlockSpec((ATTN_WIDTH, ATTN_WIDTH), const2),
                  pl.BlockSpec((tm, ATTN_WIDTH), pos),
                  pl.BlockSpec((tm, ATTN_WIDTH), pos)],
        out_specs=(pl.BlockSpec((1, N_HEADS, tm, HEAD_DIM), head_out),
                   pl.BlockSpec((1, N_KV_HEADS, tm, HEAD_DIM), head_out),
                   pl.BlockSpec((1, N_KV_HEADS, V_ROWS, tm), lambda i: (i // tps, 0, 0, i % tps)),
                   pl.BlockSpec((tm, 3 * D_HYENA), row),
                   pl.BlockSpec((tm, 2 * D_MODEL), row)),
        compiler_params=_cparams(("arbitrary",)),
        name="in_proj",
    )(x, shift, scale, gain, w_in, w_vt, qg, kg, bd, cos, sin)


def _attn_kernel(q_ref, k_ref, vt_ref, o_ref, m_ref, acc_ref, *, tkc, online):
    seq = k_ref.shape[2]
    nchunk = seq // tkc
    acc_ref[...] = jnp.zeros(acc_ref.shape, F32)
    if online:
        m_ref[...] = jnp.full(m_ref.shape, NEG_BIG, F32)

    def body(c, carry):
        off = pl.multiple_of(c * tkc, tkc)
        kc = k_ref[0, 0, pl.ds(off, tkc), :]
        vc = vt_ref[0, 0, :, pl.ds(off, tkc)]
        for h in range(Q_PER_KV):
            st = lax.dot_general(kc, q_ref[0, h], (((1,), (1,)), ((), ())), preferred_element_type=F32)
            if online:
                m_prev = m_ref[h]
                m_new = jnp.maximum(m_prev, jnp.max(st, axis=0, keepdims=True))
                p = jnp.exp2(st - m_new).astype(BF16)
                acc_ref[h] = jnp.exp2(m_prev - m_new) * acc_ref[h] + _dotbf(vc, p)
                m_ref[h] = m_new
            else:
                acc_ref[h] += _dotbf(vc, jnp.exp2(st).astype(BF16))
        return carry

    lax.fori_loop(0, nchunk, body, 0, unroll=2 if nchunk % 2 == 0 else 1)
    outs = []
    for h in range(Q_PER_KV):
        a = acc_ref[h]
        outs.append((a[:HEAD_DIM] / a[HEAD_DIM:HEAD_DIM + 1]).T)
    o_ref[0] = jnp.concatenate(outs, axis=1).astype(BF16)


def _attention(q, k, vt, online):
    nb, _, seq, _ = q.shape
    tq = min(512, seq)
    tkc = min(512 if online else 1024, seq)
    return pl.pallas_call(
        functools.partial(_attn_kernel, tkc=tkc, online=online),
        out_shape=jax.ShapeDtypeStruct((nb, seq, ATTN_WIDTH), BF16),
        grid=(nb, N_KV_HEADS, seq // tq),
        in_specs=[pl.BlockSpec((1, Q_PER_KV, tq, HEAD_DIM), lambda b, g, i: (b, g, i, 0)),
                  pl.BlockSpec((1, 1, seq, HEAD_DIM), lambda b, g, i: (b, g, 0, 0)),
                  pl.BlockSpec((1, 1, V_ROWS, seq), lambda b, g, i: (b, g, 0, 0))],
        out_specs=pl.BlockSpec((1, tq, Q_PER_KV * HEAD_DIM), lambda b, g, i: (b, i, g)),
        scratch_shapes=[pltpu.VMEM((Q_PER_KV, 1, tq), F32),
                        pltpu.VMEM((Q_PER_KV, V_ROWS, tq), F32)],
        compiler_params=_cparams(("arbitrary", "arbitrary", "arbitrary")),
        name="attn_online" if online else "attn_bounded",
    )(q, k, vt)


def _conv_kernel(u_ref, up_ref, un_ref, w_ref, b_ref, z_ref, x2_ref):
    i = pl.program_id(1)
    last = pl.num_programs(1) - 1
    u = u_ref[0]
    tr = u.shape[0]
    prev_row = up_ref[0][SUBLANES - 1:SUBLANES] * (i > 0).astype(F32)
    next_row = un_ref[0][0:1] * (i < last).astype(F32)
    ridx = lax.broadcasted_iota(jnp.int32, u.shape, 0)
    u_prev = jnp.where(ridx == 0, prev_row, pltpu.roll(u, 1, 0))
    u_next = jnp.where(ridx == tr - 1, next_row, pltpu.roll(u, tr - 1, 0))
    w = w_ref[...]
    c = b_ref[...] + u_prev * w[0:1] + u * w[1:2] + u_next * w[2:3]
    x1 = c[:, 0:D_HYENA]
    x2 = c[:, D_HYENA:2 * D_HYENA]
    vh = c[:, 2 * D_HYENA:3 * D_HYENA]
    z_ref[0] = vh * x1
    x2_ref[0] = x2


def _short_conv(u, conv_w, conv_b):
    nb, seq, ch = u.shape
    tr = min(512, seq)
    nblk8 = seq // SUBLANES
    rpb = tr // SUBLANES
    return pl.pallas_call(
        _conv_kernel,
        out_shape=(jax.ShapeDtypeStruct((nb, seq, D_HYENA), F32),
                   jax.ShapeDtypeStruct((nb, seq, D_HYENA), F32)),
        grid=(nb, seq // tr),
        in_specs=[pl.BlockSpec((1, tr, ch), lambda b, i: (b, i, 0)),
                  pl.BlockSpec((1, SUBLANES, ch), lambda b, i: (b, jnp.maximum(i * rpb - 1, 0), 0)),
                  pl.BlockSpec((1, SUBLANES, ch),
                               lambda b, i: (b, jnp.minimum((i + 1) * rpb, nblk8 - 1), 0)),
                  pl.BlockSpec((3, ch), lambda b, i: (0, 0)),
                  pl.BlockSpec((1, ch), lambda b, i: (0, 0))],
        out_specs=(pl.BlockSpec((1, tr, D_HYENA), lambda b, i: (b, i, 0)),
                   pl.BlockSpec((1, tr, D_HYENA), lambda b, i: (b, i, 0))),
        compiler_params=_cparams(("arbitrary", "arbitrary")),
        name="short_conv",
    )(u, u, u, conv_w, conv_b.reshape(1, ch))


def _filter_kernel(z_ref, w1_ref, b1_ref, w2_ref, b2_ref, w3_ref, b3_ref, fr_ref, w4_ref,
                   dl_ref, o_ref):
    z = z_ref[0]
    fr = fr_ref[...]
    h = jnp.sin(fr * (_dot32(z, w1_ref[...]) + b1_ref[...]))
    h = jnp.sin(fr * (_dot32(h, w2_ref[...]) + b2_ref[...]))
    h = jnp.sin(fr * (_dot32(h, w3_ref[...]) + b3_ref[...]))
    h = _dot32(h, w4_ref[...])
    t = z[:, 0:1]
    sign = z[:, FILTER_EMB:FILTER_EMB + 1]
    o_ref[0] = h * (jnp.exp(-t * dl_ref[...]) + MOD_SHIFT) * sign


def _filters(zfeat, w1p, b1, w2, b2, w3, b3, fr, w4, deltas):
    _, seq, fe = zfeat.shape
    tm = min(512, seq)
    c2 = lambda d, i: (0, 0)
    return pl.pallas_call(
        _filter_kernel,
        out_shape=jax.ShapeDtypeStruct((2, seq, D_HYENA), F32),
        grid=(2, seq // tm),
        in_specs=[pl.BlockSpec((1, tm, fe), lambda d, i: (d, i, 0)),
                  pl.BlockSpec(w1p.shape, c2), pl.BlockSpec(b1.shape, c2),
                  pl.BlockSpec(w2.shape, c2), pl.BlockSpec(b2.shape, c2),
                  pl.BlockSpec(w3.shape, c2), pl.BlockSpec(b3.shape, c2),
                  pl.BlockSpec(fr.shape, c2),
                  pl.BlockSpec((FILTER_ORDER, D_HYENA), lambda d, i: (0, d)),
                  pl.BlockSpec(deltas.shape, c2)],
        out_specs=pl.BlockSpec((1, tm, D_HYENA), lambda d, i: (d, i, 0)),
        compiler_params=_cparams(("arbitrary", "arbitrary")),
        name="hyena_filters",
    )(zfeat, w1p, b1, w2, b2, w3, b3, fr, w4, deltas)


def _fft_outer_kernel(g_ref, x_ref, o_ref):
    o_ref[0] = _dot3(g_ref[0], g_ref[1], x_ref[0])


def _fft_outer(gmat, x):
    nb, kk, cols = x.shape
    m = gmat.shape[1]
    tc = min(4096, cols)
    return pl.pallas_call(
        _fft_outer_kernel,
        out_shape=jax.ShapeDtypeStruct((nb, m, cols), F32),
        grid=(nb, cols // tc),
        in_specs=[pl.BlockSpec((2, m, kk), lambda b, j: (0, 0, 0)),
                  pl.BlockSpec((1, kk, tc), lambda b, j: (b, 0, j))],
        out_specs=pl.BlockSpec((1, m, tc), lambda b, j: (b, 0, j)),
        compiler_params=_cparams(("arbitrary", "arbitrary")),
        name="fft_outer",
    )(gmat, x)


def _fft_filter_inner_kernel(mf_ref, a_ref, o_ref):
    n2, ch = a_ref.shape[3], a_ref.shape[4]
    a = a_ref[0, :, 0].reshape(2 * n2, ch)
    o_ref[0] = _dot3(mf_ref[0, 0], mf_ref[0, 1], a).reshape(2, n2, ch)


def _fft_filter_inner(mf, a5):
    _, _, h1, n2, ch = a5.shape
    return pl.pallas_call(
        _fft_filter_inner_kernel,
        out_shape=jax.ShapeDtypeStruct((h1, 2, n2, ch), F32),
        grid=(h1,),
        in_specs=[pl.BlockSpec((1, 2, 2 * n2, 2 * n2), lambda k: (k, 0, 0, 0)),
                  pl.BlockSpec((1, 2, 1, n2, ch), lambda k: (0, 0, k, 0, 0))],
        out_specs=pl.BlockSpec((1, 2, n2, ch), lambda k: (k, 0, 0, 0)),
        compiler_params=_cparams(("arbitrary",)),
        name="fft_filter_inner",
    )(mf, a5)


def _fft_inner_kernel(mf_ref, mi_ref, kf_ref, a_ref, o_ref):
    n2, ch = a_ref.shape[3], a_ref.shape[4]
    a = a_ref[0, :, 0].reshape(2 * n2, ch)
    xs = _dot3(mf_ref[0, 0], mf_ref[0, 1], a)
    xr, xi = xs[:n2], xs[n2:]
    kr, ki = kf_ref[0, 0], kf_ref[0, 1]
    p = jnp.concatenate([xr * kr - xi * ki, xr * ki + xi * kr], axis=0)
    o_ref[0, :, 0] = _dot3(mi_ref[0, 0], mi_ref[0, 1], p).reshape(2, n2, ch)


def _fft_inner(mf, mi, kf, a5):
    nb, _, h1, n2, ch = a5.shape
    return pl.pallas_call(
        _fft_inner_kernel,
        out_shape=jax.ShapeDtypeStruct(a5.shape, F32),
        grid=(h1, nb),
        in_specs=[pl.BlockSpec((1, 2, 2 * n2, 2 * n2), lambda k, b: (k, 0, 0, 0)),
                  pl.BlockSpec((1, 2, 2 * n2, 2 * n2), lambda k, b: (k, 0, 0, 0)),
                  pl.BlockSpec((1, 2, n2, ch), lambda k, b: (k, 0, 0, 0)),
                  pl.BlockSpec((1, 2, 1, n2, ch), lambda k, b: (b, 0, k, 0, 0))],
        out_specs=pl.BlockSpec((1, 2, 1, n2, ch), lambda k, b: (b, 0, k, 0, 0)),
        compiler_params=_cparams(("arbitrary", "arbitrary")),
        name="fft_inner",
    )(mf, mi, kf, a5)


def _fft_final_kernel(g_ref, b_ref, z_ref, x2_ref, skip_ref, o_ref):
    y = _dot3(g_ref[0], g_ref[1], b_ref[0])
    o_ref[0] = (x2_ref[0] * (y + z_ref[0] * skip_ref[...])).astype(BF16)


def _fft_final(gc, bm, z, x2, skip_row):
    nb, h1, cols = z.shape
    tc = min(4096, cols)
    return pl.pallas_call(
        _fft_final_kernel,
        out_shape=jax.ShapeDtypeStruct((nb, h1, cols), BF16),
        grid=(nb, cols // tc),
        in_specs=[pl.BlockSpec((2, h1, 2 * h1), lambda b, j: (0, 0, 0)),
                  pl.BlockSpec((1, 2 * h1, tc), lambda b, j: (b, 0, j)),
                  pl.BlockSpec((1, h1, tc), lambda b, j: (b, 0, j)),
                  pl.BlockSpec((1, h1, tc), lambda b, j: (b, 0, j)),
                  pl.BlockSpec((1, tc), lambda b, j: (0, 0))],
        out_specs=pl.BlockSpec((1, h1, tc), lambda b, j: (b, 0, j)),
        compiler_params=_cparams(("arbitrary", "arbitrary")),
        name="fft_final",
    )(gc, bm, z, x2, skip_row)


def _dft_tables(seq):
    n = 2 * seq
    n2 = FFT_N2
    n1 = n // n2
    h1 = n1 // 2
    i32 = jnp.int32
    k1 = jnp.arange(h1, dtype=i32)[:, None]
    a1 = jnp.arange(n1, dtype=i32)[None, :]
    th = (math.pi / n1) * ((a1 * (2 * k1 + 1)) % (2 * n1)).astype(F32)
    g_re, g_im = jnp.cos(th), -jnp.sin(th)
    g_full = jnp.concatenate([g_re, g_im], axis=0)
    g_half = g_full[:, :h1]
    g_out = (2.0 / n) * jnp.concatenate([g_re[:, :h1].T, g_im[:, :h1].T], axis=1)
    k2 = jnp.arange(n2, dtype=i32)[None, :, None]
    b2 = jnp.arange(n2, dtype=i32)[None, None, :]
    kk = jnp.arange(h1, dtype=i32)[:, None, None]
    ph = (math.pi / n) * ((b2 * (2 * kk + 1 + 2 * n1 * k2)) % (2 * n)).astype(F32)
    m_re, m_im = jnp.cos(ph), -jnp.sin(ph)
    mf = jnp.concatenate([jnp.concatenate([m_re, -m_im], axis=2),
                          jnp.concatenate([m_im, m_re], axis=2)], axis=1)
    mt_re, mt_im = jnp.transpose(m_re, (0, 2, 1)), jnp.transpose(m_im, (0, 2, 1))
    mi = jnp.concatenate([jnp.concatenate([mt_re, mt_im], axis=2),
                          jnp.concatenate([-mt_im, mt_re], axis=2)], axis=1)
    pair = lambda a, axis: jnp.stack(_split_bf16(a), axis=axis)
    return dict(n1=n1, h1=h1, g_full=pair(g_full, 0), g_half=pair(g_half, 0), g_out=pair(g_out, 0),
                mf=pair(mf, 1), mi=pair(mi, 1))


def _merge_kernel(attn_ref, hy_ref, g_ref, x_ref, gate_ref, shift_ref, scale_ref, gain_ref,
                  wa_ref, wh_ref, wo_ref, wrh_ref, wrl_ref, br_ref,
                  x1_ref, h2_ref, route_ref, cnt_ref):
    g = g_ref[...].astype(F32)
    merged = (g[:, :D_MODEL] * _dotbf(attn_ref[...], wa_ref[...])
              + g[:, D_MODEL:] * _dotbf(hy_ref[...], wh_ref[...]))
    mix = _dotbf(merged.astype(BF16), wo_ref[...])
    x1 = x_ref[...] + gate_ref[0] * mix
    x1_ref[...] = x1

    ms = jnp.mean(x1 * x1, axis=-1, keepdims=True)
    h2 = x1 * lax.rsqrt(ms + RMS_EPS) * gain_ref[...]
    h2 = h2 * (1.0 + scale_ref[0]) + shift_ref[0]
    h2_ref[...] = h2

    hi = h2.astype(BF16)
    lo = (h2 - hi.astype(F32)).astype(BF16)
    wrh = wrh_ref[...]
    lg = _dotbf(hi, wrh) + _dotbf(lo, wrh) + _dotbf(hi, wrl_ref[...]) + br_ref[...]

    lane = lax.broadcasted_iota(jnp.int32, lg.shape, 1).astype(F32)
    is_grp = jnp.logical_and(lane >= N_EXPERTS, lane < N_EXPERTS + N_GROUPS)
    gm = jnp.where(is_grp, lg, NEG_BIG)
    gmax = jnp.max(gm, axis=-1, keepdims=True)
    gidx = jnp.min(jnp.where(gm == gmax, lane, 1e9), axis=-1, keepdims=True) - N_EXPERTS
    p_group = 1.0 / jnp.sum(jnp.where(is_grp, jnp.exp(gm - gmax), 0.0), axis=-1, keepdims=True)
    lo_lane = gidx * EXPERTS_PER_GROUP
    in_grp = jnp.logical_and(lane >= lo_lane, lane < lo_lane + EXPERTS_PER_GROUP)
    e1v = jnp.where(in_grp, lg, NEG_BIG)
    t1 = jnp.max(e1v, axis=-1, keepdims=True)
    i1 = jnp.min(jnp.where(e1v == t1, lane, 1e9), axis=-1, keepdims=True)
    e2v = jnp.where(lane == i1, NEG_BIG, e1v)
    t2 = jnp.max(e2v, axis=-1, keepdims=True)
    i2 = jnp.min(jnp.where(e2v == t2, lane, 1e9), axis=-1, keepdims=True)
    d = jnp.exp(t2 - t1)
    w1 = p_group / (1.0 + d)
    w2 = p_group * d / (1.0 + d)
    route_ref[...] = jnp.where(lane == 0, i1, jnp.where(lane == 1, i2,
                               jnp.where(lane == 2, w1, jnp.where(lane == 3, w2, 0.0))))

    onehot = (lane == i1).astype(F32) + (lane == i2).astype(F32)

    @pl.when(pl.program_id(0) == 0)
    def _():
        cnt_ref[...] = jnp.zeros(cnt_ref.shape, F32)

    cnt_ref[...] += jnp.sum(onehot, axis=0, keepdims=True)


def _merge(attn, hy, gates, x, gate1, shift2, scale2, gain, wa, wh, wo, wrh, wrl, br, nb, seq):
    t = nb * seq
    tm = min(512, seq)
    tps = seq // tm
    row = lambda i: (i, 0)
    per_b = lambda i: (i // tps, 0, 0)
    c2 = lambda i: (0, 0)
    return pl.pallas_call(
        _merge_kernel,
        out_shape=(jax.ShapeDtypeStruct((t, D_MODEL), F32),
                   jax.ShapeDtypeStruct((t, D_MODEL), F32),
                   jax.ShapeDtypeStruct((t, LANES), F32),
                   jax.ShapeDtypeStruct((1, LANES), F32)),
        grid=(t // tm,),
        in_specs=[pl.BlockSpec((tm, ATTN_WIDTH), row),
                  pl.BlockSpec((tm, D_HYENA), row),
                  pl.BlockSpec((tm, 2 * D_MODEL), row),
                  pl.BlockSpec((tm, D_MODEL), row),
                  pl.BlockSpec((1, 1, D_MODEL), per_b),
                  pl.BlockSpec((1, 1, D_MODEL), per_b),
                  pl.BlockSpec((1, 1, D_MODEL), per_b),
                  pl.BlockSpec((1, D_MODEL), c2),
                  pl.BlockSpec(wa.shape, c2), pl.BlockSpec(wh.shape, c2), pl.BlockSpec(wo.shape, c2),
                  pl.BlockSpec(wrh.shape, c2), pl.BlockSpec(wrl.shape, c2), pl.BlockSpec(br.shape, c2)],
        out_specs=(pl.BlockSpec((tm, D_MODEL), row),
                   pl.BlockSpec((tm, D_MODEL), row),
                   pl.BlockSpec((tm, LANES), row),
                   pl.BlockSpec((1, LANES), c2)),
        compiler_params=_cparams(("arbitrary",)),
        name="merge_router",
    )(attn, hy, gates, x, gate1, shift2, scale2, gain, wa, wh, wo, wrh, wrl, br)


def _rank_kernel(route_ref, pstart_ref, tri_ref, dest_ref, carry_ref):
    @pl.when(pl.program_id(0) == 0)
    def _():
        carry_ref[...] = jnp.zeros(carry_ref.shape, F32)

    r = route_ref[...]
    lane = lax.broadcasted_iota(jnp.int32, r.shape, 1).astype(F32)
    oh1 = (lane == r[:, 0:1]).astype(F32)
    oh2 = (lane == r[:, 1:2]).astype(F32)
    tri = tri_ref[...]
    before1 = _dotbf(tri, oh1.astype(BF16))
    before2 = _dotbf(tri, oh2.astype(BF16))
    base1 = pstart_ref[...] + carry_ref[...]
    d1 = jnp.sum(oh1 * (base1 + before1), axis=-1, keepdims=True)
    base2 = base1 + jnp.sum(oh1, axis=0, keepdims=True)
    d2 = jnp.sum(oh2 * (base2 + before2), axis=-1, keepdims=True)
    carry_ref[...] = base2 + jnp.sum(oh2, axis=0, keepdims=True) - pstart_ref[...]
    dest_ref[...] = jnp.where(lane == 0, d1, jnp.where(lane == 1, d2, 0.0))


def _rank(route, pstart, tri):
    t = route.shape[0]
    tm = tri.shape[0]
    return pl.pallas_call(
        _rank_kernel,
        out_shape=jax.ShapeDtypeStruct((t, LANES), F32),
        grid=(t // tm,),
        in_specs=[pl.BlockSpec((tm, LANES), lambda i: (i, 0)),
                  pl.BlockSpec((1, LANES), lambda i: (0, 0)),
                  pl.BlockSpec((tm, tm), lambda i: (0, 0))],
        out_specs=pl.BlockSpec((tm, LANES), lambda i: (i, 0)),
        scratch_shapes=[pltpu.VMEM((1, LANES), F32)],
        compiler_params=_cparams(("arbitrary",)),
        name="moe_rank",
    )(route, pstart, tri)


def _dispatch_kernel(d1_ref, d2_ref, h_ref, init_ref, xs_ref, sem):
    del init_ref
    tm = h_ref.shape[0]
    base = pl.program_id(0) * tm

    def copies(r):
        src = h_ref.at[pl.ds(r, 1)]
        return (pltpu.make_async_copy(src, xs_ref.at[pl.ds(d1_ref[base + r], 1)], sem),
                pltpu.make_async_copy(src, xs_ref.at[pl.ds(d2_ref[base + r], 1)], sem))

    def issue(r, c):
        a, b = copies(r)
        a.start()
        b.start()
        return c

    def drain(r, c):
        a, b = copies(r)
        a.wait()
        b.wait()
        return c

    lax.fori_loop(0, tm, issue, 0, unroll=8)
    lax.fori_loop(0, tm, drain, 0, unroll=8)


def _dispatch(d1, d2, h2, cap):
    t, d = h2.shape
    tm = min(256, t)
    init = jnp.zeros((cap, d), h2.dtype)
    return pl.pallas_call(
        _dispatch_kernel,
        out_shape=jax.ShapeDtypeStruct((cap, d), h2.dtype),
        grid_spec=pltpu.PrefetchScalarGridSpec(
            num_scalar_prefetch=2,
            grid=(t // tm,),
            in_specs=[pl.BlockSpec((tm, d), lambda i, a, b: (i, 0)),
                      pl.BlockSpec(memory_space=pl.ANY)],
            out_specs=pl.BlockSpec(memory_space=pl.ANY),
            scratch_shapes=[pltpu.SemaphoreType.DMA(())]),
        input_output_aliases={3: 0},
        compiler_params=_cparams(("arbitrary",)),
        name="moe_dispatch",
    )(d1, d2, h2, init)


def _expert_kernel(be_ref, xs_ref, wgu_ref, wd_ref, ys_ref):
    del be_ref
    x = xs_ref[...].astype(BF16)
    gu = _dotbf(x, wgu_ref[0])
    g = gu[:, :D_EXPERT]
    u = gu[:, D_EXPERT:]
    a = (g * jax.nn.sigmoid(g) * u).astype(BF16)
    ys_ref[...] = _dotbf(a, wd_ref[0])


def _experts(blk_expert, xs, wgu, wd):
    cap, d = xs.shape
    return pl.pallas_call(
        _expert_kernel,
        out_shape=jax.ShapeDtypeStruct((cap, d), F32),
        grid_spec=pltpu.PrefetchScalarGridSpec(
            num_scalar_prefetch=1,
            grid=(cap // EXPERT_BLOCK,),
            in_specs=[pl.BlockSpec((EXPERT_BLOCK, d), lambda i, be: (i, 0)),
                      pl.BlockSpec((1, d, 2 * D_EXPERT), lambda i, be: (be[i], 0, 0)),
                      pl.BlockSpec((1, D_EXPERT, d), lambda i, be: (be[i], 0, 0))],
            out_specs=pl.BlockSpec((EXPERT_BLOCK, d), lambda i, be: (i, 0))),
        compiler_params=_cparams(("arbitrary",)),
        name="moe_experts",
    )(blk_expert, xs, wgu, wd)


def _combine_kernel(d1_ref, d2_ref, ys_ref, x_ref, route_ref, gate_ref, o_ref, y1_ref, y2_ref, sem):
    tm = x_ref.shape[0]
    base = pl.program_id(0) * tm

    def copies(r):
        return (pltpu.make_async_copy(ys_ref.at[pl.ds(d1_ref[base + r], 1)], y1_ref.at[pl.ds(r, 1)], sem),
                pltpu.make_async_copy(ys_ref.at[pl.ds(d2_ref[base + r], 1)], y2_ref.at[pl.ds(r, 1)], sem))

    def issue(r, c):
        a, b = copies(r)
        a.start()
        b.start()
        return c

    def drain(r, c):
        a, b = copies(r)
        a.wait()
        b.wait()
        return c

    lax.fori_loop(0, tm, issue, 0, unroll=8)
    lax.fori_loop(0, tm, drain, 0, unroll=8)
    r = route_ref[...]
    ffn = y1_ref[...] * r[:, 2:3] + y2_ref[...] * r[:, 3:4]
    o_ref[...] = x_ref[...] + gate_ref[0] * ffn


def _combine(d1, d2, ys, x1, route, gate2, nb, seq):
    t, d = x1.shape
    tm = min(256, seq)
    tps = seq // tm
    return pl.pallas_call(
        _combine_kernel,
        out_shape=jax.ShapeDtypeStruct((t, d), F32),
        grid_spec=pltpu.PrefetchScalarGridSpec(
            num_scalar_prefetch=2,
            grid=(t // tm,),
            in_specs=[pl.BlockSpec(memory_space=pl.ANY),
                      pl.BlockSpec((tm, d), lambda i, a, b: (i, 0)),
                      pl.BlockSpec((tm, LANES), lambda i, a, b: (i, 0)),
                      pl.BlockSpec((1, 1, d), lambda i, a, b: (i // tps, 0, 0))],
            out_specs=pl.BlockSpec((tm, d), lambda i, a, b: (i, 0)),
            scratch_shapes=[pltpu.VMEM((tm, d), F32), pltpu.VMEM((tm, d), F32),
                            pltpu.SemaphoreType.DMA(())]),
        compiler_params=_cparams(("arbitrary",)),
        name="moe_combine",
    )(d1, d2, ys, x1, route, gate2)


def _rope_tables(seq):
    rows = seq // GRID_W
    row = jnp.broadcast_to(jnp.arange(rows, dtype=F32)[:, None], (rows, GRID_W)).reshape(-1)
    col = jnp.broadcast_to(jnp.arange(GRID_W, dtype=F32)[None, :], (rows, GRID_W)).reshape(-1)
    inv_freq = ROPE_THETA ** (-jnp.arange(ROPE_PAIRS_PER_AXIS, dtype=F32) / ROPE_PAIRS_PER_AXIS)
    ang = jnp.concatenate([row[:, None] * inv_freq, col[:, None] * inv_freq], axis=-1)
    cos = jnp.repeat(jnp.cos(ang), 2, axis=-1)
    sin = jnp.repeat(jnp.sin(ang), 2, axis=-1)
    sign = jnp.tile(jnp.array([-1.0, 1.0], F32), HEAD_DIM // 2)
    return jnp.tile(cos, (1, N_HEADS)), jnp.tile(sin * sign, (1, N_HEADS))


def _filter_features(seq):
    t = jnp.linspace(0.0, 1.0, seq, dtype=F32)[:, None]
    w = (2.0 * math.pi / seq) * jnp.arange(seq, dtype=F32)[:, None]
    bands = jnp.linspace(1e-4, FILTER_BANDS - 1, FILTER_BANDS, dtype=F32)[None, :]
    z = jnp.concatenate([t, jnp.cos(bands * w), -jnp.sin(bands * w)], axis=-1)
    pad = jnp.zeros((seq, FILTER_ORDER - FILTER_EMB), F32)
    fwd = jnp.concatenate([z, pad], axis=-1).at[:, FILTER_EMB].set(1.0)
    rev = jnp.concatenate([z[:1], z[:0:-1]], axis=0)
    sign = jnp.concatenate([jnp.zeros((1,), F32), -jnp.ones((seq - 1,), F32)])
    bwd = jnp.concatenate([rev, pad], axis=-1).at[:, FILTER_EMB].set(sign)
    deltas = jnp.abs(jnp.linspace(MIN_DECAY, MAX_DECAY, D_HYENA, dtype=F32))[None, :]
    return jnp.stack([fwd, bwd]), deltas


def _run_trunk(x, mod, p):
    nb, seq, d = x.shape
    t = nb * seq
    depth = mod.shape[0]
    cos, sin = _rope_tables(seq)
    zfeat, deltas = _filter_features(seq)
    tabs = _dft_tables(seq)
    h1, n1 = tabs["h1"], tabs["n1"]
    cols = FFT_N2 * D_HYENA
    tm_rank = min(512, t)
    tri = jnp.tril(jnp.ones((tm_rank, tm_rank), BF16), -1)
    cap = t * TOP_K + N_EXPERTS * EXPERT_BLOCK
    n_blk = cap // EXPERT_BLOCK
    lane_e = jnp.arange(LANES)

    xf = x.reshape(t, d)
    for l in range(depth):
        m6 = mod[l].reshape(nb, 6, 1, d)
        shift1, scale1, gate1, shift2, scale2, gate2 = (m6[:, j] for j in range(6))

        q, k, vt, u, gates = _inproj(xf, shift1, scale1, p["norm_mix"][l], p["w_in"][l], p["w_vt"][l],
                                     p["qg"][l], p["kg"][l], p["bd"], cos, sin, nb, seq)
        bound = HEAD_DIM * jnp.max(jnp.abs(p["qg"][l])) * jnp.max(jnp.abs(p["kg"][l]))
        attn = lax.cond(bound <= SCORE_BOUND,
                        functools.partial(_attention, online=False),
                        functools.partial(_attention, online=True), q, k, vt).reshape(t, ATTN_WIDTH)

        z, x2 = _short_conv(u.reshape(nb, seq, 3 * D_HYENA), p["conv_w"][l], p["conv_b"][l])
        kern = _filters(zfeat, p["filt_w1"][l], p["filt_b1"][l], p["filt_w2"][l], p["filt_b2"][l],
                        p["filt_w3"][l], p["filt_b3"][l], p["filt_freq"][l], p["filt_w4"][l], deltas)
        ka = _fft_outer(tabs["g_full"], kern.reshape(1, n1, cols))
        kf = _fft_filter_inner(tabs["mf"], ka.reshape(1, 2, h1, FFT_N2, D_HYENA))
        za = _fft_outer(tabs["g_half"], z.reshape(nb, h1, cols))
        zb = _fft_inner(tabs["mf"], tabs["mi"], kf, za.reshape(nb, 2, h1, FFT_N2, D_HYENA))
        hy = _fft_final(tabs["g_out"], zb.reshape(nb, 2 * h1, cols), z.reshape(nb, h1, cols),
                        x2.reshape(nb, h1, cols), p["skip_row"][l]).reshape(t, D_HYENA)

        x1, h2, route, counts = _merge(attn, hy, gates, xf, gate1, shift2, scale2, p["norm_ffn"][l],
                                       p["w_br_attn"][l], p["w_br_hyena"][l], p["w_out"][l],
                                       p["wr_hi"][l], p["wr_lo"][l], p["b_route"][l], nb, seq)

        cnt = counts[0].astype(jnp.int32)
        padded = jnp.where(lane_e < N_EXPERTS, (cnt + EXPERT_BLOCK - 1) // EXPERT_BLOCK * EXPERT_BLOCK, 0)
        pad_end = jnp.cumsum(padded)
        pstart = (pad_end - padded).astype(F32)[None, :]
        blk_start = jnp.arange(n_blk, dtype=jnp.int32) * EXPERT_BLOCK
        blk_expert = jnp.minimum(
            jnp.sum(blk_start[:, None] >= pad_end[None, :N_EXPERTS], axis=1), N_EXPERTS - 1).astype(jnp.int32)

        dest = _rank(route, pstart, tri)
        d1 = dest[:, 0].astype(jnp.int32)
        d2 = dest[:, 1].astype(jnp.int32)
        xs = _dispatch(d1, d2, h2, cap)
        ys = _experts(blk_expert, xs, p["w_gu"][l], p["w_down"][l])
        xf = _combine(d1, d2, ys, x1, route, gate2, nb, seq)
    return xf.reshape(nb, seq, d)


def kernel(x_prompt, x_sample, c_prompt, c_sample, w_ada, b_ada, norm_mix, norm_ffn, w_in, q_gain, k_gain, conv_w, conv_b, filt_w1, filt_b1, filt_w2, filt_b2, filt_w3, filt_b3, filt_freq, filt_w4, hyena_skip, w_br_attn, w_br_hyena, w_out, w_group, b_group, w_router, b_router, w_e_gate, w_e_up, w_e_down):
    depth = w_ada.shape[0]
    bp, bs = c_prompt.shape[0], c_sample.shape[0]
    rows = -(-(bp + bs) // SUBLANES) * SUBLANES
    c_pad = jnp.zeros((rows, D_MODEL), F32).at[:bp].set(c_prompt).at[bp:bp + bs].set(c_sample)
    mod = _ada(c_pad, w_ada, b_ada)

    scale = HEAD_DIM ** -0.5 * math.log2(math.e)
    head_id = np.arange(ATTN_WIDTH) // HEAD_DIM
    route_w = jnp.concatenate([w_router, w_group], axis=-1)
    route_w = jnp.pad(route_w, ((0, 0), (0, 0), (0, LANES - route_w.shape[-1])))
    wr_hi = route_w.astype(BF16)
    p = dict(
        norm_mix=norm_mix.reshape(depth, 1, D_MODEL),
        norm_ffn=norm_ffn.reshape(depth, 1, D_MODEL),
        w_in=w_in.astype(BF16),
        w_vt=jnp.swapaxes(w_in[:, :, COL_V:COL_U], 1, 2).astype(BF16),
        qg=(jnp.tile(q_gain, (1, N_HEADS)) * scale).reshape(depth, 1, ATTN_WIDTH),
        kg=jnp.tile(k_gain, (1, N_KV_HEADS)).reshape(depth, 1, KV_WIDTH),
        bd=jnp.asarray(head_id[:, None] == head_id[None, :], dtype=BF16),
        conv_w=conv_w, conv_b=conv_b,
        filt_w1=jnp.pad(filt_w1, ((0, 0), (0, FILTER_ORDER - FILTER_EMB), (0, 0))),
        filt_b1=filt_b1.reshape(depth, 1, FILTER_ORDER),
        filt_w2=filt_w2, filt_b2=filt_b2.reshape(depth, 1, FILTER_ORDER),
        filt_w3=filt_w3, filt_b3=filt_b3.reshape(depth, 1, FILTER_ORDER),
        filt_freq=filt_freq.reshape(depth, 1, FILTER_ORDER),
        filt_w4=filt_w4,
        skip_row=jnp.tile(hyena_skip, (1, 4096 // D_HYENA)).reshape(depth, 1, 4096),
        w_br_attn=w_br_attn.astype(BF16), w_br_hyena=w_br_hyena.astype(BF16), w_out=w_out.astype(BF16),
        wr_hi=wr_hi, wr_lo=(route_w - wr_hi.astype(F32)).astype(BF16),
        b_route=jnp.pad(jnp.concatenate([b_router, b_group], axis=-1),
                        ((0, 0), (0, LANES - N_EXPERTS - N_GROUPS))).reshape(depth, 1, LANES),
        w_gu=jnp.concatenate([w_e_gate, w_e_up], axis=-1).astype(BF16),
        w_down=w_e_down.astype(BF16),
    )
    y_prompt = _run_trunk(x_prompt, mod[:, :bp], p)
    y_sample = _run_trunk(x_sample, mod[:, bp:bp + bs], p)
    return (y_prompt, y_sample)
```

```python
import functools
import math

import numpy as np
import jax
import jax.numpy as jnp
from jax import lax
from jax.experimental import pallas as pl
from jax.experimental.pallas import tpu as pltpu

F32 = jnp.float32
BF16 = jnp.bfloat16

D_MODEL = 1024
GRID_W = 64
N_HEADS = 8
N_KV_HEADS = 4
HEAD_DIM = 64
Q_PER_KV = N_HEADS // N_KV_HEADS
ATTN_WIDTH = N_HEADS * HEAD_DIM
KV_WIDTH = N_KV_HEADS * HEAD_DIM
ROPE_THETA = 10000.0
ROPE_PAIRS_PER_AXIS = HEAD_DIM // 4
D_HYENA = D_MODEL // 2
FILTER_EMB = 33
FILTER_BANDS = (FILTER_EMB - 1) // 2
FILTER_ORDER = 64
DECAY_TARGET = 1e-2
MAX_DECAY = math.log(DECAY_TARGET) / 0.3
MIN_DECAY = math.log(DECAY_TARGET) / 1.5
MOD_SHIFT = 0.05
N_GROUPS = 4
EXPERTS_PER_GROUP = 8
N_EXPERTS = N_GROUPS * EXPERTS_PER_GROUP
TOP_K = 2
D_EXPERT = D_MODEL // 4
RMS_EPS = 1e-6
IN_COLS = ATTN_WIDTH + 2 * KV_WIDTH + 3 * D_HYENA + 2 * D_MODEL
COL_K = ATTN_WIDTH
COL_V = ATTN_WIDTH + KV_WIDTH
COL_U = ATTN_WIDTH + 2 * KV_WIDTH
COL_G = COL_U + 3 * D_HYENA

LANES = 128
SUBLANES = 8
VMEM_LIMIT = 56 * 1024 * 1024

V_ROWS = HEAD_DIM + 16
SCORE_BOUND = 60.0
FFT_N2 = 128
EXPERT_BLOCK = 512
NEG_BIG = -1e30

_HI = lax.Precision.HIGHEST


def _dot32(a, b):
    return jnp.dot(a, b, precision=_HI, preferred_element_type=F32)


def _dotbf(a, b):
    return jnp.dot(a, b, preferred_element_type=F32)


def _split_bf16(a):
    hi = a.astype(BF16)
    return hi, (a - hi.astype(F32)).astype(BF16)


def _dot3(w_hi, w_lo, x):
    x_hi, x_lo = _split_bf16(x)
    return _dotbf(w_hi, x_hi) + _dotbf(w_lo, x_hi) + _dotbf(w_hi, x_lo)


def _cparams(sem):
    return pltpu.CompilerParams(dimension_semantics=sem, vmem_limit_bytes=VMEM_LIMIT)


def _ada_kernel(c_ref, w_ref, b_ref, o_ref):
    c = c_ref[...]
    act = c * jax.nn.sigmoid(c)
    o_ref[0] = _dot32(act, w_ref[0]) + b_ref[0]


def _ada(c_pad, w_ada, b_ada):
    depth, d, n = w_ada.shape
    tn = 1536
    return pl.pallas_call(
        _ada_kernel,
        out_shape=jax.ShapeDtypeStruct((depth, c_pad.shape[0], n), F32),
        grid=(depth, n // tn),
        in_specs=[pl.BlockSpec(c_pad.shape, lambda l, j: (0, 0)),
                  pl.BlockSpec((1, d, tn), lambda l, j: (l, 0, j)),
                  pl.BlockSpec((1, 1, tn), lambda l, j: (l, 0, j))],
        out_specs=pl.BlockSpec((1, c_pad.shape[0], tn), lambda l, j: (l, 0, j)),
        compiler_params=_cparams(("arbitrary", "arbitrary")),
        name="ada_mod",
    )(c_pad, w_ada, b_ada.reshape(depth, 1, n))


def _swap_pairs(x):
    n = x.shape[-1]
    lane = lax.broadcasted_iota(jnp.int32, x.shape, 1)
    nxt = pltpu.roll(x, n - 1, 1)
    prv = pltpu.roll(x, 1, 1)
    return jnp.where(lane % 2 == 0, nxt, prv)


def _head_norm_rope(p, gain, bd, cos, sin_signed):
    sq = (p * p).astype(BF16)
    ms = _dotbf(sq, bd) * (1.0 / HEAD_DIM)
    pn = p * lax.rsqrt(ms + RMS_EPS) * gain
    return pn * cos + _swap_pairs(pn) * sin_signed


def _inproj_kernel(x_ref, shift_ref, scale_ref, gain_ref, w_ref, wvt_ref, qg_ref, kg_ref, bd_ref,
                   cos_ref, sin_ref, q_ref, k_ref, vt_ref, u_ref, g_ref):
    x = x_ref[...]
    ms = jnp.mean(x * x, axis=-1, keepdims=True)
    h = x * lax.rsqrt(ms + RMS_EPS) * gain_ref[...]
    h = h * (1.0 + scale_ref[0]) + shift_ref[0]
    hb = h.astype(BF16)
    cos = cos_ref[...]
    sin = sin_ref[...]
    bd = bd_ref[...]

    q = _dotbf(hb, w_ref[:, 0:COL_K])
    q = _head_norm_rope(q, qg_ref[...], bd, cos, sin)
    for hd in range(N_HEADS):
        q_ref[0, hd] = q[:, hd * HEAD_DIM:(hd + 1) * HEAD_DIM].astype(BF16)

    k = _dotbf(hb, w_ref[:, COL_K:COL_V])
    k = _head_norm_rope(k, kg_ref[...], bd[:KV_WIDTH, :KV_WIDTH], cos[:, :KV_WIDTH], sin[:, :KV_WIDTH])
    vt = lax.dot_general(wvt_ref[...], hb, (((1,), (1,)), ((), ())), preferred_element_type=F32)
    ones = jnp.ones((V_ROWS - HEAD_DIM, vt.shape[1]), BF16)
    for hd in range(N_KV_HEADS):
        k_ref[0, hd] = k[:, hd * HEAD_DIM:(hd + 1) * HEAD_DIM].astype(BF16)
        vt_ref[0, hd, 0:HEAD_DIM, :] = vt[hd * HEAD_DIM:(hd + 1) * HEAD_DIM].astype(BF16)
        vt_ref[0, hd, HEAD_DIM:V_ROWS, :] = ones

    u_ref[...] = _dotbf(hb, w_ref[:, COL_U:COL_G])
    g_ref[...] = jax.nn.sigmoid(_dotbf(hb, w_ref[:, COL_G:IN_COLS])).astype(BF16)


def _inproj(x, shift, scale, gain, w_in, w_vt, qg, kg, bd, cos, sin, nb, seq):
    t = nb * seq
    tm = min(512, seq)
    tps = seq // tm
    row = lambda i: (i, 0)
    per_b = lambda i: (i // tps, 0, 0)
    const2 = lambda i: (0, 0)
    pos = lambda i: (i % tps, 0)
    head_out = lambda i: (i // tps, 0, i % tps, 0)
    return pl.pallas_call(
        _inproj_kernel,
        out_shape=(jax.ShapeDtypeStruct((nb, N_HEADS, seq, HEAD_DIM), BF16),
                   jax.ShapeDtypeStruct((nb, N_KV_HEADS, seq, HEAD_DIM), BF16),
                   jax.ShapeDtypeStruct((nb, N_KV_HEADS, V_ROWS, seq), BF16),
                   jax.ShapeDtypeStruct((t, 3 * D_HYENA), F32),
                   jax.ShapeDtypeStruct((t, 2 * D_MODEL), BF16)),
        grid=(t // tm,),
        in_specs=[pl.BlockSpec((tm, D_MODEL), row),
                  pl.BlockSpec((1, 1, D_MODEL), per_b),
                  pl.BlockSpec((1, 1, D_MODEL), per_b),
                  pl.BlockSpec((1, D_MODEL), const2),
                  pl.BlockSpec((D_MODEL, IN_COLS), const2),
                  pl.BlockSpec((KV_WIDTH, D_MODEL), const2),
                  pl.BlockSpec((1, ATTN_WIDTH), const2),
                  pl.BlockSpec((1, KV_WIDTH), const2),
                  pl.BlockSpec((ATTN_WIDTH, ATTN_WIDTH), const2),
                  pl.BlockSpec((tm, ATTN_WIDTH), pos),
                  pl.BlockSpec((tm, ATTN_WIDTH), pos)],
        out_specs=(pl.BlockSpec((1, N_HEADS, tm, HEAD_DIM), head_out),
                   pl.BlockSpec((1, N_KV_HEADS, tm, HEAD_DIM), head_out),
                   pl.BlockSpec((1, N_KV_HEADS, V_ROWS, tm), lambda i: (i // tps, 0, 0, i % tps)),
                   pl.BlockSpec((tm, 3 * D_HYENA), row),
                   pl.BlockSpec((tm, 2 * D_MODEL), row)),
        compiler_params=_cparams(("arbitrary",)),
        name="in_proj",
    )(x, shift, scale, gain, w_in, w_vt, qg, kg, bd, cos, sin)


def _attn_kernel(q_ref, k_ref, vt_ref, o_ref, m_ref, acc_ref, *, tkc, online):
    seq = k_ref.shape[2]
    nchunk = seq // tkc
    acc_ref[...] = jnp.zeros(acc_ref.shape, F32)
    if online:
        m_ref[...] = jnp.full(m_ref.shape, NEG_BIG, F32)

    def body(c, carry):
        off = pl.multiple_of(c * tkc, tkc)
        kc = k_ref[0, 0, pl.ds(off, tkc), :]
        vc = vt_ref[0, 0, :, pl.ds(off, tkc)]
        for h in range(Q_PER_KV):
            st = lax.dot_general(kc, q_ref[0, h], (((1,), (1,)), ((), ())), preferred_element_type=F32)
            if online:
                m_prev = m_ref[h]
                m_new = jnp.maximum(m_prev, jnp.max(st, axis=0, keepdims=True))
                p = jnp.exp2(st - m_new).astype(BF16)
                acc_ref[h] = jnp.exp2(m_prev - m_new) * acc_ref[h] + _dotbf(vc, p)
                m_ref[h] = m_new
            else:
                acc_ref[h] += _dotbf(vc, jnp.exp2(st).astype(BF16))
        return carry

    lax.fori_loop(0, nchunk, body, 0, unroll=2 if nchunk % 2 == 0 else 1)
    outs = []
    for h in range(Q_PER_KV):
        a = acc_ref[h]
        outs.append((a[:HEAD_DIM] / a[HEAD_DIM:HEAD_DIM + 1]).T)
    o_ref[0] = jnp.concatenate(outs, axis=1).astype(BF16)


def _attention(q, k, vt, online):
    nb, _, seq, _ = q.shape
    tq = min(512 if online else 1024, seq)
    tkc = min(512 if online else 1024, seq)
    return pl.pallas_call(
        functools.partial(_attn_kernel, tkc=tkc, online=online),
        out_shape=jax.ShapeDtypeStruct((nb, seq, ATTN_WIDTH), BF16),
        grid=(nb, N_KV_HEADS, seq // tq),
        in_specs=[pl.BlockSpec((1, Q_PER_KV, tq, HEAD_DIM), lambda b, g, i: (b, g, i, 0)),
                  pl.BlockSpec((1, 1, seq, HEAD_DIM), lambda b, g, i: (b, g, 0, 0)),
                  pl.BlockSpec((1, 1, V_ROWS, seq), lambda b, g, i: (b, g, 0, 0))],
        out_specs=pl.BlockSpec((1, tq, Q_PER_KV * HEAD_DIM), lambda b, g, i: (b, i, g)),
        scratch_shapes=[pltpu.VMEM((Q_PER_KV, 1, tq), F32),
                        pltpu.VMEM((Q_PER_KV, V_ROWS, tq), F32)],
        compiler_params=_cparams(("arbitrary", "arbitrary", "arbitrary")),
        name="attn_online" if online else "attn_bounded",
    )(q, k, vt)


def _conv_kernel(u_ref, up_ref, un_ref, w_ref, b_ref, z_ref, x2_ref):
    i = pl.program_id(1)
    last = pl.num_programs(1) - 1
    u = u_ref[0]
    tr = u.shape[0]
    prev_row = up_ref[0][SUBLANES - 1:SUBLANES] * (i > 0).astype(F32)
    next_row = un_ref[0][0:1] * (i < last).astype(F32)
    ridx = lax.broadcasted_iota(jnp.int32, u.shape, 0)
    u_prev = jnp.where(ridx == 0, prev_row, pltpu.roll(u, 1, 0))
    u_next = jnp.where(ridx == tr - 1, next_row, pltpu.roll(u, tr - 1, 0))
    w = w_ref[...]
    c = b_ref[...] + u_prev * w[0:1] + u * w[1:2] + u_next * w[2:3]
    x1 = c[:, 0:D_HYENA]
    x2 = c[:, D_HYENA:2 * D_HYENA]
    vh = c[:, 2 * D_HYENA:3 * D_HYENA]
    z_ref[0] = vh * x1
    x2_ref[0] = x2


def _short_conv(u, conv_w, conv_b):
    nb, seq, ch = u.shape
    tr = min(512, seq)
    nblk8 = seq // SUBLANES
    rpb = tr // SUBLANES
    return pl.pallas_call(
        _conv_kernel,
        out_shape=(jax.ShapeDtypeStruct((nb, seq, D_HYENA), F32),
                   jax.ShapeDtypeStruct((nb, seq, D_HYENA), F32)),
        grid=(nb, seq // tr),
        in_specs=[pl.BlockSpec((1, tr, ch), lambda b, i: (b, i, 0)),
                  pl.BlockSpec((1, SUBLANES, ch), lambda b, i: (b, jnp.maximum(i * rpb - 1, 0), 0)),
                  pl.BlockSpec((1, SUBLANES, ch),
                               lambda b, i: (b, jnp.minimum((i + 1) * rpb, nblk8 - 1), 0)),
                  pl.BlockSpec((3, ch), lambda b, i: (0, 0)),
                  pl.BlockSpec((1, ch), lambda b, i: (0, 0))],
        out_specs=(pl.BlockSpec((1, tr, D_HYENA), lambda b, i: (b, i, 0)),
                   pl.BlockSpec((1, tr, D_HYENA), lambda b, i: (b, i, 0))),
        compiler_params=_cparams(("arbitrary", "arbitrary")),
        name="short_conv",
    )(u, u, u, conv_w, conv_b.reshape(1, ch))


def _filter_kernel(z_ref, w1_ref, b1_ref, w2_ref, b2_ref, w3_ref, b3_ref, fr_ref, w4_ref,
                   dl_ref, o_ref):
    z = z_ref[0]
    fr = fr_ref[...]
    h = jnp.sin(fr * (_dot32(z, w1_ref[...]) + b1_ref[...]))
    h = jnp.sin(fr * (_dot32(h, w2_ref[...]) + b2_ref[...]))
    h = jnp.sin(fr * (_dot32(h, w3_ref[...]) + b3_ref[...]))
    h = _dot32(h, w4_ref[...])
    t = z[:, 0:1]
    sign = z[:, FILTER_EMB:FILTER_EMB + 1]
    o_ref[0] = h * (jnp.exp(-t * dl_ref[...]) + MOD_SHIFT) * sign


def _filters(zfeat, w1p, b1, w2, b2, w3, b3, fr, w4, deltas):
    _, seq, fe = zfeat.shape
    tm = min(512, seq)
    c2 = lambda d, i: (0, 0)
    return pl.pallas_call(
        _filter_kernel,
        out_shape=jax.ShapeDtypeStruct((2, seq, D_HYENA), F32),
        grid=(2, seq // tm),
        in_specs=[pl.BlockSpec((1, tm, fe), lambda d, i: (d, i, 0)),
                  pl.BlockSpec(w1p.shape, c2), pl.BlockSpec(b1.shape, c2),
                  pl.BlockSpec(w2.shape, c2), pl.BlockSpec(b2.shape, c2),
                  pl.BlockSpec(w3.shape, c2), pl.BlockSpec(b3.shape, c2),
                  pl.BlockSpec(fr.shape, c2),
                  pl.BlockSpec((FILTER_ORDER, D_HYENA), lambda d, i: (0, d)),
                  pl.BlockSpec(deltas.shape, c2)],
        out_specs=pl.BlockSpec((1, tm, D_HYENA), lambda d, i: (d, i, 0)),
        compiler_params=_cparams(("arbitrary", "arbitrary")),
        name="hyena_filters",
    )(zfeat, w1p, b1, w2, b2, w3, b3, fr, w4, deltas)


def _fft_outer_kernel(g_ref, x_ref, o_ref, *, n2):
    g_hi, g_lo = g_ref[0], g_ref[1]
    m, kk = g_hi.shape

    def body(j, c):
        xj = x_ref[pl.ds(j, kk, stride=n2), :]
        o_ref[pl.ds(j, m, stride=n2), :] = _dot3(g_hi, g_lo, xj)
        return c

    lax.fori_loop(0, n2, body, 0, unroll=8)


def _fft_outer(gmat, x, n2):
    nb, rows, ch = x.shape
    _, m, kk = gmat.shape
    return pl.pallas_call(
        functools.partial(_fft_outer_kernel, n2=n2),
        out_shape=jax.ShapeDtypeStruct((nb, m * n2, ch), F32),
        grid=(nb, ch // LANES),
        in_specs=[pl.BlockSpec((2, m, kk), lambda b, c: (0, 0, 0)),
                  pl.BlockSpec((None, rows, LANES), lambda b, c: (b, 0, c))],
        out_specs=pl.BlockSpec((None, m * n2, LANES), lambda b, c: (b, 0, c)),
        compiler_params=_cparams(("arbitrary", "arbitrary")),
        name="fft_outer",
    )(gmat, x)


def _fft_filter_inner_kernel(mf_ref, a_ref, o_ref):
    n2, ch = a_ref.shape[3], a_ref.shape[4]
    a = a_ref[0, :, 0].reshape(2 * n2, ch)
    o_ref[0] = _dot3(mf_ref[0, 0], mf_ref[0, 1], a).reshape(2, n2, ch)


def _fft_filter_inner(mf, a5):
    _, _, h1, n2, ch = a5.shape
    return pl.pallas_call(
        _fft_filter_inner_kernel,
        out_shape=jax.ShapeDtypeStruct((h1, 2, n2, ch), F32),
        grid=(h1,),
        in_specs=[pl.BlockSpec((1, 2, 2 * n2, 2 * n2), lambda k: (k, 0, 0, 0)),
                  pl.BlockSpec((1, 2, 1, n2, ch), lambda k: (0, 0, k, 0, 0))],
        out_specs=pl.BlockSpec((1, 2, n2, ch), lambda k: (k, 0, 0, 0)),
        compiler_params=_cparams(("arbitrary",)),
        name="fft_filter_inner",
    )(mf, a5)


def _fft_inner_kernel(mf_ref, mi_ref, kf_ref, a_ref, o_ref):
    n2, ch = a_ref.shape[3], a_ref.shape[4]
    a = a_ref[0, :, 0].reshape(2 * n2, ch)
    xs = _dot3(mf_ref[0, 0], mf_ref[0, 1], a)
    xr, xi = xs[:n2], xs[n2:]
    kr, ki = kf_ref[0, 0], kf_ref[0, 1]
    p = jnp.concatenate([xr * kr - xi * ki, xr * ki + xi * kr], axis=0)
    o_ref[0, :, 0] = _dot3(mi_ref[0, 0], mi_ref[0, 1], p).reshape(2, n2, ch)


def _fft_inner(mf, mi, kf, a5):
    nb, _, h1, n2, ch = a5.shape
    return pl.pallas_call(
        _fft_inner_kernel,
        out_shape=jax.ShapeDtypeStruct(a5.shape, F32),
        grid=(h1, nb),
        in_specs=[pl.BlockSpec((1, 2, 2 * n2, 2 * n2), lambda k, b: (k, 0, 0, 0)),
                  pl.BlockSpec((1, 2, 2 * n2, 2 * n2), lambda k, b: (k, 0, 0, 0)),
                  pl.BlockSpec((1, 2, n2, ch), lambda k, b: (k, 0, 0, 0)),
                  pl.BlockSpec((1, 2, 1, n2, ch), lambda k, b: (b, 0, k, 0, 0))],
        out_specs=pl.BlockSpec((1, 2, 1, n2, ch), lambda k, b: (b, 0, k, 0, 0)),
        compiler_params=_cparams(("arbitrary", "arbitrary")),
        name="fft_inner",
    )(mf, mi, kf, a5)


def _fft_final_kernel(g_ref, b_ref, z_ref, x2_ref, skip_ref, o_ref, *, n2):
    g_hi, g_lo = g_ref[0], g_ref[1]
    h1, m = g_hi.shape
    skip = skip_ref[...]

    def body(j, c):
        y = _dot3(g_hi, g_lo, b_ref[pl.ds(j, m, stride=n2), :])
        zj = z_ref[pl.ds(j, h1, stride=n2), :]
        xj = x2_ref[pl.ds(j, h1, stride=n2), :]
        o_ref[pl.ds(j, h1, stride=n2), :] = xj * (y + zj * skip)
        return c

    lax.fori_loop(0, n2, body, 0, unroll=8)


def _fft_final(gc, bm, z, x2, skip, n2):
    nb, seq, ch = z.shape
    _, h1, m = gc.shape
    blk = lambda rows: pl.BlockSpec((None, rows, LANES), lambda b, c: (b, 0, c))
    return pl.pallas_call(
        functools.partial(_fft_final_kernel, n2=n2),
        out_shape=jax.ShapeDtypeStruct((nb, seq, ch), F32),
        grid=(nb, ch // LANES),
        in_specs=[pl.BlockSpec((2, h1, m), lambda b, c: (0, 0, 0)), blk(m * n2), blk(seq), blk(seq),
                  pl.BlockSpec((1, LANES), lambda b, c: (0, c))],
        out_specs=blk(seq),
        compiler_params=_cparams(("arbitrary", "arbitrary")),
        name="fft_final",
    )(gc, bm, z, x2, skip)


def _dft_tables(seq):
    n = 2 * seq
    n2 = FFT_N2
    n1 = n // n2
    h1 = n1 // 2
    i32 = jnp.int32
    k1 = jnp.arange(h1, dtype=i32)[:, None]
    a1 = jnp.arange(n1, dtype=i32)[None, :]
    th = (math.pi / n1) * ((a1 * (2 * k1 + 1)) % (2 * n1)).astype(F32)
    g_re, g_im = jnp.cos(th), -jnp.sin(th)
    g_full = jnp.concatenate([g_re, g_im], axis=0)
    g_half = g_full[:, :h1]
    g_out = (2.0 / n) * jnp.concatenate([g_re[:, :h1].T, g_im[:, :h1].T], axis=1)
    k2 = jnp.arange(n2, dtype=i32)[None, :, None]
    b2 = jnp.arange(n2, dtype=i32)[None, None, :]
    kk = jnp.arange(h1, dtype=i32)[:, None, None]
    ph = (math.pi / n) * ((b2 * (2 * kk + 1 + 2 * n1 * k2)) % (2 * n)).astype(F32)
    m_re, m_im = jnp.cos(ph), -jnp.sin(ph)
    mf = jnp.concatenate([jnp.concatenate([m_re, -m_im], axis=2),
                          jnp.concatenate([m_im, m_re], axis=2)], axis=1)
    mt_re, mt_im = jnp.transpose(m_re, (0, 2, 1)), jnp.transpose(m_im, (0, 2, 1))
    mi = jnp.concatenate([jnp.concatenate([mt_re, mt_im], axis=2),
                          jnp.concatenate([-mt_im, mt_re], axis=2)], axis=1)
    pair = lambda a, axis: jnp.stack(_split_bf16(a), axis=axis)
    return dict(n1=n1, h1=h1, g_full=pair(g_full, 0), g_half=pair(g_half, 0), g_out=pair(g_out, 0),
                mf=pair(mf, 1), mi=pair(mi, 1))


def _merge_kernel(attn_ref, hy_ref, g_ref, x_ref, gate_ref, shift_ref, scale_ref, gain_ref,
                  wa_ref, wh_ref, wo_ref, wrh_ref, wrl_ref, br_ref,
                  x1_ref, h2_ref, route_ref, cnt_ref):
    g = g_ref[...].astype(F32)
    merged = (g[:, :D_MODEL] * _dotbf(attn_ref[...], wa_ref[...])
              + g[:, D_MODEL:] * _dotbf(hy_ref[...].astype(BF16), wh_ref[...]))
    mix = _dotbf(merged.astype(BF16), wo_ref[...])
    x1 = x_ref[...] + gate_ref[0] * mix
    x1_ref[...] = x1

    ms = jnp.mean(x1 * x1, axis=-1, keepdims=True)
    h2 = x1 * lax.rsqrt(ms + RMS_EPS) * gain_ref[...]
    h2 = h2 * (1.0 + scale_ref[0]) + shift_ref[0]
    h2_ref[...] = h2

    hi = h2.astype(BF16)
    lo = (h2 - hi.astype(F32)).astype(BF16)
    wrh = wrh_ref[...]
    lg = _dotbf(hi, wrh) + _dotbf(lo, wrh) + _dotbf(hi, wrl_ref[...]) + br_ref[...]

    lane = lax.broadcasted_iota(jnp.int32, lg.shape, 1).astype(F32)
    is_grp = jnp.logical_and(lane >= N_EXPERTS, lane < N_EXPERTS + N_GROUPS)
    gm = jnp.where(is_grp, lg, NEG_BIG)
    gmax = jnp.max(gm, axis=-1, keepdims=True)
    gidx = jnp.min(jnp.where(gm == gmax, lane, 1e9), axis=-1, keepdims=True) - N_EXPERTS
    p_group = 1.0 / jnp.sum(jnp.where(is_grp, jnp.exp(gm - gmax), 0.0), axis=-1, keepdims=True)
    lo_lane = gidx * EXPERTS_PER_GROUP
    in_grp = jnp.logical_and(lane >= lo_lane, lane < lo_lane + EXPERTS_PER_GROUP)
    e1v = jnp.where(in_grp, lg, NEG_BIG)
    t1 = jnp.max(e1v, axis=-1, keepdims=True)
    i1 = jnp.min(jnp.where(e1v == t1, lane, 1e9), axis=-1, keepdims=True)
    e2v = jnp.where(lane == i1, NEG_BIG, e1v)
    t2 = jnp.max(e2v, axis=-1, keepdims=True)
    i2 = jnp.min(jnp.where(e2v == t2, lane, 1e9), axis=-1, keepdims=True)
    d = jnp.exp(t2 - t1)
    w1 = p_group / (1.0 + d)
    w2 = p_group * d / (1.0 + d)
    route_ref[...] = jnp.where(lane == 0, i1, jnp.where(lane == 1, i2,
                               jnp.where(lane == 2, w1, jnp.where(lane == 3, w2, 0.0))))

    onehot = (lane == i1).astype(F32) + (lane == i2).astype(F32)

    @pl.when(pl.program_id(0) == 0)
    def _():
        cnt_ref[...] = jnp.zeros(cnt_ref.shape, F32)

    cnt_ref[...] += jnp.sum(onehot, axis=0, keepdims=True)


def _merge(attn, hy, gates, x, gate1, shift2, scale2, gain, wa, wh, wo, wrh, wrl, br, nb, seq):
    t = nb * seq
    tm = min(512, seq)
    tps = seq // tm
    row = lambda i: (i, 0)
    per_b = lambda i: (i // tps, 0, 0)
    c2 = lambda i: (0, 0)
    return pl.pallas_call(
        _merge_kernel,
        out_shape=(jax.ShapeDtypeStruct((t, D_MODEL), F32),
                   jax.ShapeDtypeStruct((t, D_MODEL), F32),
                   jax.ShapeDtypeStruct((t, LANES), F32),
                   jax.ShapeDtypeStruct((1, LANES), F32)),
        grid=(t // tm,),
        in_specs=[pl.BlockSpec((tm, ATTN_WIDTH), row),
                  pl.BlockSpec((tm, D_HYENA), row),
                  pl.BlockSpec((tm, 2 * D_MODEL), row),
                  pl.BlockSpec((tm, D_MODEL), row),
                  pl.BlockSpec((1, 1, D_MODEL), per_b),
                  pl.BlockSpec((1, 1, D_MODEL), per_b),
                  pl.BlockSpec((1, 1, D_MODEL), per_b),
                  pl.BlockSpec((1, D_MODEL), c2),
                  pl.BlockSpec(wa.shape, c2), pl.BlockSpec(wh.shape, c2), pl.BlockSpec(wo.shape, c2),
                  pl.BlockSpec(wrh.shape, c2), pl.BlockSpec(wrl.shape, c2), pl.BlockSpec(br.shape, c2)],
        out_specs=(pl.BlockSpec((tm, D_MODEL), row),
                   pl.BlockSpec((tm, D_MODEL), row),
                   pl.BlockSpec((tm, LANES), row),
                   pl.BlockSpec((1, LANES), c2)),
        compiler_params=_cparams(("arbitrary",)),
        name="merge_router",
    )(attn, hy, gates, x, gate1, shift2, scale2, gain, wa, wh, wo, wrh, wrl, br)


def _rank_kernel(route_ref, pstart_ref, tri_ref, dest_ref, carry_ref):
    @pl.when(pl.program_id(0) == 0)
    def _():
        carry_ref[...] = jnp.zeros(carry_ref.shape, F32)

    r = route_ref[...]
    lane = lax.broadcasted_iota(jnp.int32, r.shape, 1).astype(F32)
    oh1 = (lane == r[:, 0:1]).astype(F32)
    oh2 = (lane == r[:, 1:2]).astype(F32)
    tri = tri_ref[...]
    before1 = _dotbf(tri, oh1.astype(BF16))
    before2 = _dotbf(tri, oh2.astype(BF16))
    base1 = pstart_ref[...] + carry_ref[...]
    d1 = jnp.sum(oh1 * (base1 + before1), axis=-1, keepdims=True)
    base2 = base1 + jnp.sum(oh1, axis=0, keepdims=True)
    d2 = jnp.sum(oh2 * (base2 + before2), axis=-1, keepdims=True)
    carry_ref[...] = base2 + jnp.sum(oh2, axis=0, keepdims=True) - pstart_ref[...]
    dest_ref[...] = jnp.where(lane == 0, d1, jnp.where(lane == 1, d2, 0.0))


def _rank(route, pstart, tri):
    t = route.shape[0]
    tm = tri.shape[0]
    return pl.pallas_call(
        _rank_kernel,
        out_shape=jax.ShapeDtypeStruct((t, LANES), F32),
        grid=(t // tm,),
        in_specs=[pl.BlockSpec((tm, LANES), lambda i: (i, 0)),
                  pl.BlockSpec((1, LANES), lambda i: (0, 0)),
                  pl.BlockSpec((tm, tm), lambda i: (0, 0))],
        out_specs=pl.BlockSpec((tm, LANES), lambda i: (i, 0)),
        scratch_shapes=[pltpu.VMEM((1, LANES), F32)],
        compiler_params=_cparams(("arbitrary",)),
        name="moe_rank",
    )(route, pstart, tri)


def _dispatch_kernel(d1_ref, d2_ref, h_ref, init_ref, xs_ref, sem):
    del init_ref
    tm = h_ref.shape[0]
    base = pl.program_id(0) * tm

    def copies(r):
        src = h_ref.at[pl.ds(r, 1)]
        return (pltpu.make_async_copy(src, xs_ref.at[pl.ds(d1_ref[base + r], 1)], sem),
                pltpu.make_async_copy(src, xs_ref.at[pl.ds(d2_ref[base + r], 1)], sem))

    def issue(r, c):
        a, b = copies(r)
        a.start()
        b.start()
        return c

    def drain(r, c):
        a, b = copies(r)
        a.wait()
        b.wait()
        return c

    lax.fori_loop(0, tm, issue, 0, unroll=8)
    lax.fori_loop(0, tm, drain, 0, unroll=8)


def _dispatch(d1, d2, h2, cap):
    t, d = h2.shape
    tm = min(256, t)
    init = jnp.zeros((cap, d), h2.dtype)
    return pl.pallas_call(
        _dispatch_kernel,
        out_shape=jax.ShapeDtypeStruct((cap, d), h2.dtype),
        grid_spec=pltpu.PrefetchScalarGridSpec(
            num_scalar_prefetch=2,
            grid=(t // tm,),
            in_specs=[pl.BlockSpec((tm, d), lambda i, a, b: (i, 0)),
                      pl.BlockSpec(memory_space=pl.ANY)],
            out_specs=pl.BlockSpec(memory_space=pl.ANY),
            scratch_shapes=[pltpu.SemaphoreType.DMA(())]),
        input_output_aliases={3: 0},
        compiler_params=_cparams(("arbitrary",)),
        name="moe_dispatch",
    )(d1, d2, h2, init)


def _expert_kernel(be_ref, xs_ref, wgu_ref, wd_ref, ys_ref):
    del be_ref
    x = xs_ref[...].astype(BF16)
    gu = _dotbf(x, wgu_ref[0])
    g = gu[:, :D_EXPERT]
    u = gu[:, D_EXPERT:]
    a = (g * jax.nn.sigmoid(g) * u).astype(BF16)
    ys_ref[...] = _dotbf(a, wd_ref[0])


def _experts(blk_expert, xs, wgu, wd):
    cap, d = xs.shape
    return pl.pallas_call(
        _expert_kernel,
        out_shape=jax.ShapeDtypeStruct((cap, d), F32),
        grid_spec=pltpu.PrefetchScalarGridSpec(
            num_scalar_prefetch=1,
            grid=(cap // EXPERT_BLOCK,),
            in_specs=[pl.BlockSpec((EXPERT_BLOCK, d), lambda i, be: (i, 0)),
                      pl.BlockSpec((1, d, 2 * D_EXPERT), lambda i, be: (be[i], 0, 0)),
                      pl.BlockSpec((1, D_EXPERT, d), lambda i, be: (be[i], 0, 0))],
            out_specs=pl.BlockSpec((EXPERT_BLOCK, d), lambda i, be: (i, 0))),
        compiler_params=_cparams(("arbitrary",)),
        name="moe_experts",
    )(blk_expert, xs, wgu, wd)


def _combine_kernel(d1_ref, d2_ref, ys_ref, x_ref, route_ref, gate_ref, o_ref, y1_ref, y2_ref, sem):
    tm = x_ref.shape[0]
    base = pl.program_id(0) * tm

    def copies(r):
        return (pltpu.make_async_copy(ys_ref.at[pl.ds(d1_ref[base + r], 1)], y1_ref.at[pl.ds(r, 1)], sem),
                pltpu.make_async_copy(ys_ref.at[pl.ds(d2_ref[base + r], 1)], y2_ref.at[pl.ds(r, 1)], sem))

    def issue(r, c):
        a, b = copies(r)
        a.start()
        b.start()
        return c

    def drain(r, c):
        a, b = copies(r)
        a.wait()
        b.wait()
        return c

    lax.fori_loop(0, tm, issue, 0, unroll=8)
    lax.fori_loop(0, tm, drain, 0, unroll=8)
    r = route_ref[...]
    ffn = y1_ref[...] * r[:, 2:3] + y2_ref[...] * r[:, 3:4]
    o_ref[...] = x_ref[...] + gate_ref[0] * ffn


def _combine(d1, d2, ys, x1, route, gate2, nb, seq):
    t, d = x1.shape
    tm = min(256, seq)
    tps = seq // tm
    return pl.pallas_call(
        _combine_kernel,
        out_shape=jax.ShapeDtypeStruct((t, d), F32),
        grid_spec=pltpu.PrefetchScalarGridSpec(
            num_scalar_prefetch=2,
            grid=(t // tm,),
            in_specs=[pl.BlockSpec(memory_space=pl.ANY),
                      pl.BlockSpec((tm, d), lambda i, a, b: (i, 0)),
                      pl.BlockSpec((tm, LANES), lambda i, a, b: (i, 0)),
                      pl.BlockSpec((1, 1, d), lambda i, a, b: (i // tps, 0, 0))],
            out_specs=pl.BlockSpec((tm, d), lambda i, a, b: (i, 0)),
            scratch_shapes=[pltpu.VMEM((tm, d), F32), pltpu.VMEM((tm, d), F32),
                            pltpu.SemaphoreType.DMA(())]),
        compiler_params=_cparams(("arbitrary",)),
        name="moe_combine",
    )(d1, d2, ys, x1, route, gate2)


def _rope_tables(seq):
    rows = seq // GRID_W
    row = jnp.broadcast_to(jnp.arange(rows, dtype=F32)[:, None], (rows, GRID_W)).reshape(-1)
    col = jnp.broadcast_to(jnp.arange(GRID_W, dtype=F32)[None, :], (rows, GRID_W)).reshape(-1)
    inv_freq = ROPE_THETA ** (-jnp.arange(ROPE_PAIRS_PER_AXIS, dtype=F32) / ROPE_PAIRS_PER_AXIS)
    ang = jnp.concatenate([row[:, None] * inv_freq, col[:, None] * inv_freq], axis=-1)
    cos = jnp.repeat(jnp.cos(ang), 2, axis=-1)
    sin = jnp.repeat(jnp.sin(ang), 2, axis=-1)
    sign = jnp.tile(jnp.array([-1.0, 1.0], F32), HEAD_DIM // 2)
    return jnp.tile(cos, (1, N_HEADS)), jnp.tile(sin * sign, (1, N_HEADS))


def _filter_features(seq):
    t = jnp.linspace(0.0, 1.0, seq, dtype=F32)[:, None]
    w = (2.0 * math.pi / seq) * jnp.arange(seq, dtype=F32)[:, None]
    bands = jnp.linspace(1e-4, FILTER_BANDS - 1, FILTER_BANDS, dtype=F32)[None, :]
    z = jnp.concatenate([t, jnp.cos(bands * w), -jnp.sin(bands * w)], axis=-1)
    pad = jnp.zeros((seq, FILTER_ORDER - FILTER_EMB), F32)
    fwd = jnp.concatenate([z, pad], axis=-1).at[:, FILTER_EMB].set(1.0)
    rev = jnp.concatenate([z[:1], z[:0:-1]], axis=0)
    sign = jnp.concatenate([jnp.zeros((1,), F32), -jnp.ones((seq - 1,), F32)])
    bwd = jnp.concatenate([rev, pad], axis=-1).at[:, FILTER_EMB].set(sign)
    deltas = jnp.abs(jnp.linspace(MIN_DECAY, MAX_DECAY, D_HYENA, dtype=F32))[None, :]
    return jnp.stack([fwd, bwd]), deltas


def _run_trunk(x, mod, p):
    nb, seq, d = x.shape
    t = nb * seq
    depth = mod.shape[0]
    cos, sin = _rope_tables(seq)
    zfeat, deltas = _filter_features(seq)
    tabs = _dft_tables(seq)
    h1 = tabs["h1"]
    tm_rank = min(512, t)
    tri = jnp.tril(jnp.ones((tm_rank, tm_rank), BF16), -1)
    cap = t * TOP_K + N_EXPERTS * EXPERT_BLOCK
    n_blk = cap // EXPERT_BLOCK
    lane_e = jnp.arange(LANES)

    xf = x.reshape(t, d)
    for l in range(depth):
        m6 = mod[l].reshape(nb, 6, 1, d)
        shift1, scale1, gate1, shift2, scale2, gate2 = (m6[:, j] for j in range(6))

        q, k, vt, u, gates = _inproj(xf, shift1, scale1, p["norm_mix"][l], p["w_in"][l], p["w_vt"][l],
                                     p["qg"][l], p["kg"][l], p["bd"], cos, sin, nb, seq)
        bound = HEAD_DIM * jnp.max(jnp.abs(p["qg"][l])) * jnp.max(jnp.abs(p["kg"][l]))
        attn = lax.cond(bound <= SCORE_BOUND,
                        functools.partial(_attention, online=False),
                        functools.partial(_attention, online=True), q, k, vt).reshape(t, ATTN_WIDTH)

        z, x2 = _short_conv(u.reshape(nb, seq, 3 * D_HYENA), p["conv_w"][l], p["conv_b"][l])
        kern = _filters(zfeat, p["filt_w1"][l], p["filt_b1"][l], p["filt_w2"][l], p["filt_b2"][l],
                        p["filt_w3"][l], p["filt_b3"][l], p["filt_freq"][l], p["filt_w4"][l], deltas)
        ka = _fft_outer(tabs["g_full"], kern.reshape(1, 2 * seq, D_HYENA), FFT_N2)
        kf = _fft_filter_inner(tabs["mf"], ka.reshape(1, 2, h1, FFT_N2, D_HYENA))
        za = _fft_outer(tabs["g_half"], z, FFT_N2)
        zb = _fft_inner(tabs["mf"], tabs["mi"], kf, za.reshape(nb, 2, h1, FFT_N2, D_HYENA))
        hy = _fft_final(tabs["g_out"], zb.reshape(nb, 2 * h1 * FFT_N2, D_HYENA), z, x2,
                        p["skip"][l], FFT_N2).reshape(t, D_HYENA)

        x1, h2, route, counts = _merge(attn, hy, gates, xf, gate1, shift2, scale2, p["norm_ffn"][l],
                                       p["w_br_attn"][l], p["w_br_hyena"][l], p["w_out"][l],
                                       p["wr_hi"][l], p["wr_lo"][l], p["b_route"][l], nb, seq)

        cnt = counts[0].astype(jnp.int32)
        padded = jnp.where(lane_e < N_EXPERTS, (cnt + EXPERT_BLOCK - 1) // EXPERT_BLOCK * EXPERT_BLOCK, 0)
        pad_end = jnp.cumsum(padded)
        pstart = (pad_end - padded).astype(F32)[None, :]
        blk_start = jnp.arange(n_blk, dtype=jnp.int32) * EXPERT_BLOCK
        blk_expert = jnp.minimum(
            jnp.sum(blk_start[:, None] >= pad_end[None, :N_EXPERTS], axis=1), N_EXPERTS - 1).astype(jnp.int32)

        dest = _rank(route, pstart, tri)
        d1 = dest[:, 0].astype(jnp.int32)
        d2 = dest[:, 1].astype(jnp.int32)
        xs = _dispatch(d1, d2, h2, cap)
        ys = _experts(blk_expert, xs, p["w_gu"][l], p["w_down"][l])
        xf = _combine(d1, d2, ys, x1, route, gate2, nb, seq)
    return xf.reshape(nb, seq, d)


def kernel(x_prompt, x_sample, c_prompt, c_sample, w_ada, b_ada, norm_mix, norm_ffn, w_in, q_gain, k_gain, conv_w, conv_b, filt_w1, filt_b1, filt_w2, filt_b2, filt_w3, filt_b3, filt_freq, filt_w4, hyena_skip, w_br_attn, w_br_hyena, w_out, w_group, b_group, w_router, b_router, w_e_gate, w_e_up, w_e_down):
    depth = w_ada.shape[0]
    bp, bs = c_prompt.shape[0], c_sample.shape[0]
    rows = -(-(bp + bs) // SUBLANES) * SUBLANES
    c_pad = jnp.zeros((rows, D_MODEL), F32).at[:bp].set(c_prompt).at[bp:bp + bs].set(c_sample)
    mod = _ada(c_pad, w_ada, b_ada)

    scale = HEAD_DIM ** -0.5 * math.log2(math.e)
    head_id = np.arange(ATTN_WIDTH) // HEAD_DIM
    route_w = jnp.concatenate([w_router, w_group], axis=-1)
    route_w = jnp.pad(route_w, ((0, 0), (0, 0), (0, LANES - route_w.shape[-1])))
    wr_hi = route_w.astype(BF16)
    p = dict(
        norm_mix=norm_mix.reshape(depth, 1, D_MODEL),
        norm_ffn=norm_ffn.reshape(depth, 1, D_MODEL),
        w_in=w_in.astype(BF16),
        w_vt=jnp.swapaxes(w_in[:, :, COL_V:COL_U], 1, 2).astype(BF16),
        qg=(jnp.tile(q_gain, (1, N_HEADS)) * scale).reshape(depth, 1, ATTN_WIDTH),
        kg=jnp.tile(k_gain, (1, N_KV_HEADS)).reshape(depth, 1, KV_WIDTH),
        bd=jnp.asarray(head_id[:, None] == head_id[None, :], dtype=BF16),
        conv_w=conv_w, conv_b=conv_b,
        filt_w1=jnp.pad(filt_w1, ((0, 0), (0, FILTER_ORDER - FILTER_EMB), (0, 0))),
        filt_b1=filt_b1.reshape(depth, 1, FILTER_ORDER),
        filt_w2=filt_w2, filt_b2=filt_b2.reshape(depth, 1, FILTER_ORDER),
        filt_w3=filt_w3, filt_b3=filt_b3.reshape(depth, 1, FILTER_ORDER),
        filt_freq=filt_freq.reshape(depth, 1, FILTER_ORDER),
        filt_w4=filt_w4,
        skip=hyena_skip.reshape(depth, 1, D_HYENA),
        w_br_attn=w_br_attn.astype(BF16), w_br_hyena=w_br_hyena.astype(BF16), w_out=w_out.astype(BF16),
        wr_hi=wr_hi, wr_lo=(route_w - wr_hi.astype(F32)).astype(BF16),
        b_route=jnp.pad(jnp.concatenate([b_router, b_group], axis=-1),
                        ((0, 0), (0, LANES - N_EXPERTS - N_GROUPS))).reshape(depth, 1, LANES),
        w_gu=jnp.concatenate([w_e_gate, w_e_up], axis=-1).astype(BF16),
        w_down=w_e_down.astype(BF16),
    )
    y_prompt = _run_trunk(x_prompt, mod[:, :bp], p)
    y_sample = _run_trunk(x_sample, mod[:, bp:bp + bs], p)
    return (y_prompt, y_sample)
```

```python
import functools
import math

import numpy as np
import jax
import jax.numpy as jnp
from jax import lax
from jax.experimental import pallas as pl
from jax.experimental.pallas import tpu as pltpu

F32 = jnp.float32
BF16 = jnp.bfloat16

D_MODEL = 1024
GRID_W = 64
N_HEADS = 8
N_KV_HEADS = 4
HEAD_DIM = 64
Q_PER_KV = N_HEADS // N_KV_HEADS
ATTN_WIDTH = N_HEADS * HEAD_DIM
KV_WIDTH = N_KV_HEADS * HEAD_DIM
ROPE_THETA = 10000.0
ROPE_PAIRS_PER_AXIS = HEAD_DIM // 4
D_HYENA = D_MODEL // 2
FILTER_EMB = 33
FILTER_BANDS = (FILTER_EMB - 1) // 2
FILTER_ORDER = 64
DECAY_TARGET = 1e-2
MAX_DECAY = math.log(DECAY_TARGET) / 0.3
MIN_DECAY = math.log(DECAY_TARGET) / 1.5
MOD_SHIFT = 0.05
N_GROUPS = 4
EXPERTS_PER_GROUP = 8
N_EXPERTS = N_GROUPS * EXPERTS_PER_GROUP
TOP_K = 2
D_EXPERT = D_MODEL // 4
RMS_EPS = 1e-6
IN_COLS = ATTN_WIDTH + 2 * KV_WIDTH + 3 * D_HYENA + 2 * D_MODEL
COL_K = ATTN_WIDTH
COL_V = ATTN_WIDTH + KV_WIDTH
COL_U = ATTN_WIDTH + 2 * KV_WIDTH
COL_G = COL_U + 3 * D_HYENA

LANES = 128
SUBLANES = 8
ROW_TILES = D_MODEL // LANES
VMEM_LIMIT = 56 * 1024 * 1024

V_ROWS = HEAD_DIM + 16
SCORE_BOUND = 60.0
FFT_N2 = 128
EXPERT_BLOCK = 512
NEG_BIG = -1e30

_HI = lax.Precision.HIGHEST


def _dot32(a, b):
    return jnp.dot(a, b, precision=_HI, preferred_element_type=F32)


def _dotbf(a, b):
    return jnp.dot(a, b, preferred_element_type=F32)


def _cparams(sem):
    return pltpu.CompilerParams(dimension_semantics=sem, vmem_limit_bytes=VMEM_LIMIT)


def _ada_kernel(c_ref, w_ref, b_ref, o_ref):
    c = c_ref[...]
    act = c * jax.nn.sigmoid(c)
    o_ref[0] = _dot32(act, w_ref[0]) + b_ref[0]


def _ada(c_pad, w_ada, b_ada):
    depth, d, n = w_ada.shape
    tn = 1536
    return pl.pallas_call(
        _ada_kernel,
        out_shape=jax.ShapeDtypeStruct((depth, c_pad.shape[0], n), F32),
        grid=(depth, n // tn),
        in_specs=[pl.BlockSpec(c_pad.shape, lambda l, j: (0, 0)),
                  pl.BlockSpec((1, d, tn), lambda l, j: (l, 0, j)),
                  pl.BlockSpec((1, 1, tn), lambda l, j: (l, 0, j))],
        out_specs=pl.BlockSpec((1, c_pad.shape[0], tn), lambda l, j: (l, 0, j)),
        compiler_params=_cparams(("arbitrary", "arbitrary")),
        name="ada_mod",
    )(c_pad, w_ada, b_ada.reshape(depth, 1, n))


def _swap_pairs(x):
    n = x.shape[-1]
    lane = lax.broadcasted_iota(jnp.int32, x.shape, 1)
    nxt = pltpu.roll(x, n - 1, 1)
    prv = pltpu.roll(x, 1, 1)
    return jnp.where(lane % 2 == 0, nxt, prv)


def _head_norm_rope(p, gain, bd, cos, sin_signed):
    sq = (p * p).astype(BF16)
    ms = _dotbf(sq, bd) * (1.0 / HEAD_DIM)
    pn = p * lax.rsqrt(ms + RMS_EPS) * gain
    return pn * cos + _swap_pairs(pn) * sin_signed


def _inproj_kernel(x_ref, shift_ref, scale_ref, gain_ref, w_ref, wvt_ref, qg_ref, kg_ref, bd_ref,
                   cos_ref, sin_ref, q_ref, k_ref, vt_ref, u_ref, g_ref):
    x = x_ref[...]
    ms = jnp.mean(x * x, axis=-1, keepdims=True)
    h = x * lax.rsqrt(ms + RMS_EPS) * gain_ref[...]
    h = h * (1.0 + scale_ref[0]) + shift_ref[0]
    hb = h.astype(BF16)
    cos = cos_ref[...]
    sin = sin_ref[...]
    bd = bd_ref[...]

    q = _dotbf(hb, w_ref[:, 0:COL_K])
    q = _head_norm_rope(q, qg_ref[...], bd, cos, sin)
    for hd in range(N_HEADS):
        q_ref[0, hd] = q[:, hd * HEAD_DIM:(hd + 1) * HEAD_DIM].astype(BF16)

    k = _dotbf(hb, w_ref[:, COL_K:COL_V])
    k = _head_norm_rope(k, kg_ref[...], bd[:KV_WIDTH, :KV_WIDTH], cos[:, :KV_WIDTH], sin[:, :KV_WIDTH])
    vt = lax.dot_general(wvt_ref[...], hb, (((1,), (1,)), ((), ())), preferred_element_type=F32)
    ones = jnp.ones((V_ROWS - HEAD_DIM, vt.shape[1]), BF16)
    for hd in range(N_KV_HEADS):
        k_ref[0, hd] = k[:, hd * HEAD_DIM:(hd + 1) * HEAD_DIM].astype(BF16)
        vt_ref[0, hd, 0:HEAD_DIM, :] = vt[hd * HEAD_DIM:(hd + 1) * HEAD_DIM].astype(BF16)
        vt_ref[0, hd, HEAD_DIM:V_ROWS, :] = ones

    u_ref[...] = _dotbf(hb, w_ref[:, COL_U:COL_G])
    g_ref[...] = jax.nn.sigmoid(_dotbf(hb, w_ref[:, COL_G:IN_COLS])).astype(BF16)


def _inproj(x, shift, scale, gain, w_in, w_vt, qg, kg, bd, cos, sin, nb, seq):
    t = nb * seq
    tm = min(512, seq)
    tps = seq // tm
    row = lambda i: (i, 0)
    per_b = lambda i: (i // tps, 0, 0)
    const2 = lambda i: (0, 0)
    pos = lambda i: (i % tps, 0)
    head_out = lambda i: (i // tps, 0, i % tps, 0)
    return pl.pallas_call(
        _inproj_kernel,
        out_shape=(jax.ShapeDtypeStruct((nb, N_HEADS, seq, HEAD_DIM), BF16),
                   jax.ShapeDtypeStruct((nb, N_KV_HEADS, seq, HEAD_DIM), BF16),
                   jax.ShapeDtypeStruct((nb, N_KV_HEADS, V_ROWS, seq), BF16),
                   jax.ShapeDtypeStruct((t, 3 * D_HYENA), F32),
                   jax.ShapeDtypeStruct((t, 2 * D_MODEL), BF16)),
        grid=(t // tm,),
        in_specs=[pl.BlockSpec((tm, D_MODEL), row),
                  pl.BlockSpec((1, 1, D_MODEL), per_b),
                  pl.BlockSpec((1, 1, D_MODEL), per_b),
                  pl.BlockSpec((1, D_MODEL), const2),
                  pl.BlockSpec((D_MODEL, IN_COLS), const2),
                  pl.BlockSpec((KV_WIDTH, D_MODEL), const2),
                  pl.BlockSpec((1, ATTN_WIDTH), const2),
                  pl.BlockSpec((1, KV_WIDTH), const2),
                  pl.BlockSpec((ATTN_WIDTH, ATTN_WIDTH), const2),
                  pl.BlockSpec((tm, ATTN_WIDTH), pos),
                  pl.BlockSpec((tm, ATTN_WIDTH), pos)],
        out_specs=(pl.BlockSpec((1, N_HEADS, tm, HEAD_DIM), head_out),
                   pl.BlockSpec((1, N_KV_HEADS, tm, HEAD_DIM), head_out),
                   pl.BlockSpec((1, N_KV_HEADS, V_ROWS, tm), lambda i: (i // tps, 0, 0, i % tps)),
                   pl.BlockSpec((tm, 3 * D_HYENA), row),
                   pl.BlockSpec((tm, 2 * D_MODEL), row)),
        compiler_params=_cparams(("arbitrary",)),
        name="in_proj",
    )(x, shift, scale, gain, w_in, w_vt, qg, kg, bd, cos, sin)


def _attn_kernel(q_ref, k_ref, vt_ref, o_ref, m_ref, acc_ref, *, tkc, online):
    seq = k_ref.shape[2]
    nchunk = seq // tkc
    acc_ref[...] = jnp.zeros(acc_ref.shape, F32)
    if online:
        m_ref[...] = jnp.full(m_ref.shape, NEG_BIG, F32)

    def body(c, carry):
        off = pl.multiple_of(c * tkc, tkc)
        kc = k_ref[0, 0, pl.ds(off, tkc), :]
        vc = vt_ref[0, 0, :, pl.ds(off, tkc)]
        for h in range(Q_PER_KV):
            st = lax.dot_general(kc, q_ref[0, h], (((1,), (1,)), ((), ())), preferred_element_type=F32)
            if online:
                m_prev = m_ref[h]
                m_new = jnp.maximum(m_prev, jnp.max(st, axis=0, keepdims=True))
                p = jnp.exp2(st - m_new).astype(BF16)
                acc_ref[h] = jnp.exp2(m_prev - m_new) * acc_ref[h] + _dotbf(vc, p)
                m_ref[h] = m_new
            else:
                acc_ref[h] += _dotbf(vc, jnp.exp2(st).astype(BF16))
        return carry

    lax.fori_loop(0, nchunk, body, 0, unroll=2 if nchunk % 2 == 0 else 1)
    outs = []
    for h in range(Q_PER_KV):
        a = acc_ref[h]
        outs.append((a[:HEAD_DIM] / a[HEAD_DIM:HEAD_DIM + 1]).T)
    o_ref[0] = jnp.concatenate(outs, axis=1).astype(BF16)


def _attention(q, k, vt, online):
    nb, _, seq, _ = q.shape
    tq = min(512 if online else 1024, seq)
    tkc = min(512 if online else 1024, seq)
    return pl.pallas_call(
        functools.partial(_attn_kernel, tkc=tkc, online=online),
        out_shape=jax.ShapeDtypeStruct((nb, seq, ATTN_WIDTH), BF16),
        grid=(nb, N_KV_HEADS, seq // tq),
        in_specs=[pl.BlockSpec((1, Q_PER_KV, tq, HEAD_DIM), lambda b, g, i: (b, g, i, 0)),
                  pl.BlockSpec((1, 1, seq, HEAD_DIM), lambda b, g, i: (b, g, 0, 0)),
                  pl.BlockSpec((1, 1, V_ROWS, seq), lambda b, g, i: (b, g, 0, 0))],
        out_specs=pl.BlockSpec((1, tq, Q_PER_KV * HEAD_DIM), lambda b, g, i: (b, i, g)),
        scratch_shapes=[pltpu.VMEM((Q_PER_KV, 1, tq), F32),
                        pltpu.VMEM((Q_PER_KV, V_ROWS, tq), F32)],
        compiler_params=_cparams(("arbitrary", "arbitrary", "arbitrary")),
        name="attn_online" if online else "attn_bounded",
    )(q, k, vt)


def _conv_kernel(u_ref, up_ref, un_ref, w_ref, b_ref, z_ref, x2_ref):
    i = pl.program_id(1)
    last = pl.num_programs(1) - 1
    u = u_ref[0]
    tr = u.shape[0]
    prev_row = up_ref[0][SUBLANES - 1:SUBLANES] * (i > 0).astype(F32)
    next_row = un_ref[0][0:1] * (i < last).astype(F32)
    ridx = lax.broadcasted_iota(jnp.int32, u.shape, 0)
    u_prev = jnp.where(ridx == 0, prev_row, pltpu.roll(u, 1, 0))
    u_next = jnp.where(ridx == tr - 1, next_row, pltpu.roll(u, tr - 1, 0))
    w = w_ref[...]
    c = b_ref[...] + u_prev * w[0:1] + u * w[1:2] + u_next * w[2:3]
    x1 = c[:, 0:D_HYENA]
    x2 = c[:, D_HYENA:2 * D_HYENA]
    vh = c[:, 2 * D_HYENA:3 * D_HYENA]
    z_ref[0] = vh * x1
    x2_ref[0] = x2


def _short_conv(u, conv_w, conv_b):
    nb, seq, ch = u.shape
    tr = min(512, seq)
    nblk8 = seq // SUBLANES
    rpb = tr // SUBLANES
    return pl.pallas_call(
        _conv_kernel,
        out_shape=(jax.ShapeDtypeStruct((nb, seq, D_HYENA), F32),
                   jax.ShapeDtypeStruct((nb, seq, D_HYENA), F32)),
        grid=(nb, seq // tr),
        in_specs=[pl.BlockSpec((1, tr, ch), lambda b, i: (b, i, 0)),
                  pl.BlockSpec((1, SUBLANES, ch), lambda b, i: (b, jnp.maximum(i * rpb - 1, 0), 0)),
                  pl.BlockSpec((1, SUBLANES, ch),
                               lambda b, i: (b, jnp.minimum((i + 1) * rpb, nblk8 - 1), 0)),
                  pl.BlockSpec((3, ch), lambda b, i: (0, 0)),
                  pl.BlockSpec((1, ch), lambda b, i: (0, 0))],
        out_specs=(pl.BlockSpec((1, tr, D_HYENA), lambda b, i: (b, i, 0)),
                   pl.BlockSpec((1, tr, D_HYENA), lambda b, i: (b, i, 0))),
        compiler_params=_cparams(("arbitrary", "arbitrary")),
        name="short_conv",
    )(u, u, u, conv_w, conv_b.reshape(1, ch))


def _filter_kernel(z_ref, w1_ref, b1_ref, w2_ref, b2_ref, w3_ref, b3_ref, fr_ref, w4_ref,
                   dl_ref, o_ref):
    z = z_ref[...]
    fr = fr_ref[...]
    h = jnp.sin(fr * (_dot32(z, w1_ref[...]) + b1_ref[...]))
    h = jnp.sin(fr * (_dot32(h, w2_ref[...]) + b2_ref[...]))
    h = jnp.sin(fr * (_dot32(h, w3_ref[...]) + b3_ref[...]))
    h = _dot32(h, w4_ref[...])
    dl = dl_ref[...]
    for d in range(2):
        t = z[:, d * FILTER_ORDER:d * FILTER_ORDER + 1]
        sign = z[:, d * FILTER_ORDER + FILTER_EMB:d * FILTER_ORDER + FILTER_EMB + 1]
        o_ref[d] = h[:, d * D_HYENA:(d + 1) * D_HYENA] * (jnp.exp(-t * dl) + MOD_SHIFT) * sign


def _filters(zfeat, w1, b1, w2, b2, w3, b3, fr, w4, deltas):
    seq, fe = zfeat.shape
    tm = min(512, seq)
    c2 = lambda i: (0, 0)
    return pl.pallas_call(
        _filter_kernel,
        out_shape=jax.ShapeDtypeStruct((2, seq, D_HYENA), F32),
        grid=(seq // tm,),
        in_specs=[pl.BlockSpec((tm, fe), lambda i: (i, 0)),
                  pl.BlockSpec(w1.shape, c2), pl.BlockSpec(b1.shape, c2),
                  pl.BlockSpec(w2.shape, c2), pl.BlockSpec(b2.shape, c2),
                  pl.BlockSpec(w3.shape, c2), pl.BlockSpec(b3.shape, c2),
                  pl.BlockSpec(fr.shape, c2), pl.BlockSpec(w4.shape, c2),
                  pl.BlockSpec(deltas.shape, c2)],
        out_specs=pl.BlockSpec((2, tm, D_HYENA), lambda i: (0, i, 0)),
        compiler_params=_cparams(("arbitrary",)),
        name="hyena_filters",
    )(zfeat, w1, b1, w2, b2, w3, b3, fr, w4, deltas)


def _fft_outer_kernel(g_ref, x_ref, o_ref, *, n2):
    g = g_ref[...]
    m, kk = g.shape

    def body(j, c):
        xj = x_ref[pl.ds(j, kk, stride=n2), :]
        o_ref[pl.ds(j, m, stride=n2), :] = _dotbf(g, xj.astype(BF16))
        return c

    lax.fori_loop(0, n2, body, 0, unroll=8)


def _fft_outer(gmat, x, n2):
    nb, rows, ch = x.shape
    m, kk = gmat.shape
    return pl.pallas_call(
        functools.partial(_fft_outer_kernel, n2=n2),
        out_shape=jax.ShapeDtypeStruct((nb, m * n2, ch), F32),
        grid=(nb, ch // LANES),
        in_specs=[pl.BlockSpec((m, kk), lambda b, c: (0, 0)),
                  pl.BlockSpec((None, rows, LANES), lambda b, c: (b, 0, c))],
        out_specs=pl.BlockSpec((None, m * n2, LANES), lambda b, c: (b, 0, c)),
        compiler_params=_cparams(("arbitrary", "arbitrary")),
        name="fft_outer",
    )(gmat, x)


def _fft_filter_inner_kernel(mf_ref, a_ref, o_ref):
    n2, ch = a_ref.shape[3], a_ref.shape[4]
    a = a_ref[0, :, 0].reshape(2 * n2, ch)
    o_ref[0] = _dotbf(mf_ref[0], a.astype(BF16)).reshape(2, n2, ch)


def _fft_filter_inner(mf, a5):
    _, _, h1, n2, ch = a5.shape
    return pl.pallas_call(
        _fft_filter_inner_kernel,
        out_shape=jax.ShapeDtypeStruct((h1, 2, n2, ch), F32),
        grid=(h1,),
        in_specs=[pl.BlockSpec((1, 2 * n2, 2 * n2), lambda k: (k, 0, 0)),
                  pl.BlockSpec((1, 2, 1, n2, ch), lambda k: (0, 0, k, 0, 0))],
        out_specs=pl.BlockSpec((1, 2, n2, ch), lambda k: (k, 0, 0, 0)),
        compiler_params=_cparams(("arbitrary",)),
        name="fft_filter_inner",
    )(mf, a5)


def _fft_inner_kernel(mf_ref, mi_ref, kf_ref, a_ref, o_ref):
    n2, ch = a_ref.shape[3], a_ref.shape[4]
    a = a_ref[0, :, 0].reshape(2 * n2, ch)
    xs = _dotbf(mf_ref[0], a.astype(BF16))
    xr, xi = xs[:n2], xs[n2:]
    kr, ki = kf_ref[0, 0], kf_ref[0, 1]
    p = jnp.concatenate([xr * kr - xi * ki, xr * ki + xi * kr], axis=0)
    o_ref[0, :, 0] = _dotbf(mi_ref[0], p.astype(BF16)).reshape(2, n2, ch)


def _fft_inner(mf, mi, kf, a5):
    nb, _, h1, n2, ch = a5.shape
    return pl.pallas_call(
        _fft_inner_kernel,
        out_shape=jax.ShapeDtypeStruct(a5.shape, F32),
        grid=(h1, nb),
        in_specs=[pl.BlockSpec((1, 2 * n2, 2 * n2), lambda k, b: (k, 0, 0)),
                  pl.BlockSpec((1, 2 * n2, 2 * n2), lambda k, b: (k, 0, 0)),
                  pl.BlockSpec((1, 2, n2, ch), lambda k, b: (k, 0, 0, 0)),
                  pl.BlockSpec((1, 2, 1, n2, ch), lambda k, b: (b, 0, k, 0, 0))],
        out_specs=pl.BlockSpec((1, 2, 1, n2, ch), lambda k, b: (b, 0, k, 0, 0)),
        compiler_params=_cparams(("arbitrary", "arbitrary")),
        name="fft_inner",
    )(mf, mi, kf, a5)


def _fft_final_kernel(g_ref, b_ref, z_ref, x2_ref, skip_ref, o_ref, *, n2):
    g = g_ref[...]
    h1, m = g.shape
    skip = skip_ref[...]

    def body(j, c):
        y = _dotbf(g, b_ref[pl.ds(j, m, stride=n2), :].astype(BF16))
        zj = z_ref[pl.ds(j, h1, stride=n2), :]
        xj = x2_ref[pl.ds(j, h1, stride=n2), :]
        o_ref[pl.ds(j, h1, stride=n2), :] = xj * (y + zj * skip)
        return c

    lax.fori_loop(0, n2, body, 0, unroll=8)


def _fft_final(gc, bm, z, x2, skip, n2):
    nb, seq, ch = z.shape
    h1, m = gc.shape
    blk = lambda rows: pl.BlockSpec((None, rows, LANES), lambda b, c: (b, 0, c))
    return pl.pallas_call(
        functools.partial(_fft_final_kernel, n2=n2),
        out_shape=jax.ShapeDtypeStruct((nb, seq, ch), F32),
        grid=(nb, ch // LANES),
        in_specs=[pl.BlockSpec((h1, m), lambda b, c: (0, 0)), blk(m * n2), blk(seq), blk(seq),
                  pl.BlockSpec((1, LANES), lambda b, c: (0, c))],
        out_specs=blk(seq),
        compiler_params=_cparams(("arbitrary", "arbitrary")),
        name="fft_final",
    )(gc, bm, z, x2, skip)


def _dft_tables(seq):
    n = 2 * seq
    n2 = FFT_N2
    n1 = n // n2
    h1 = n1 // 2
    i32 = jnp.int32
    k1 = jnp.arange(h1, dtype=i32)[:, None]
    a1 = jnp.arange(n1, dtype=i32)[None, :]
    th = (math.pi / n1) * ((a1 * (2 * k1 + 1)) % (2 * n1)).astype(F32)
    g_re, g_im = jnp.cos(th), -jnp.sin(th)
    g_full = jnp.concatenate([g_re, g_im], axis=0)
    g_half = g_full[:, :h1]
    g_out = (2.0 / n) * jnp.concatenate([g_re[:, :h1].T, g_im[:, :h1].T], axis=1)
    k2 = jnp.arange(n2, dtype=i32)[None, :, None]
    b2 = jnp.arange(n2, dtype=i32)[None, None, :]
    kk = jnp.arange(h1, dtype=i32)[:, None, None]
    ph = (math.pi / n) * ((b2 * (2 * kk + 1 + 2 * n1 * k2)) % (2 * n)).astype(F32)
    m_re, m_im = jnp.cos(ph), -jnp.sin(ph)
    mf = jnp.concatenate([jnp.concatenate([m_re, -m_im], axis=2),
                          jnp.concatenate([m_im, m_re], axis=2)], axis=1)
    mt_re, mt_im = jnp.transpose(m_re, (0, 2, 1)), jnp.transpose(m_im, (0, 2, 1))
    mi = jnp.concatenate([jnp.concatenate([mt_re, mt_im], axis=2),
                          jnp.concatenate([-mt_im, mt_re], axis=2)], axis=1)
    b16 = lambda a: a.astype(BF16)
    return dict(n1=n1, h1=h1, g_full=b16(g_full), g_half=b16(g_half), g_out=b16(g_out), mf=b16(mf), mi=b16(mi))


def _merge_kernel(attn_ref, hy_ref, g_ref, x_ref, gate_ref, shift_ref, scale_ref, gain_ref,
                  wa_ref, wh_ref, wo_ref, wrh_ref, wrl_ref, br_ref,
                  x1_ref, h2_ref, route_ref, cnt_ref):
    g = g_ref[...].astype(F32)
    merged = (g[:, :D_MODEL] * _dotbf(attn_ref[...], wa_ref[...])
              + g[:, D_MODEL:] * _dotbf(hy_ref[...].astype(BF16), wh_ref[...]))
    mix = _dotbf(merged.astype(BF16), wo_ref[...])
    x1 = x_ref[...] + gate_ref[0] * mix
    x1_ref[...] = x1

    ms = jnp.mean(x1 * x1, axis=-1, keepdims=True)
    h2 = x1 * lax.rsqrt(ms + RMS_EPS) * gain_ref[...]
    h2 = h2 * (1.0 + scale_ref[0]) + shift_ref[0]
    _rows_to_tiles(h2_ref, h2)

    hi = h2.astype(BF16)
    lo = (h2 - hi.astype(F32)).astype(BF16)
    wrh = wrh_ref[...]
    lg = _dotbf(hi, wrh) + _dotbf(lo, wrh) + _dotbf(hi, wrl_ref[...]) + br_ref[...]

    lane = lax.broadcasted_iota(jnp.int32, lg.shape, 1).astype(F32)
    is_grp = jnp.logical_and(lane >= N_EXPERTS, lane < N_EXPERTS + N_GROUPS)
    gm = jnp.where(is_grp, lg, NEG_BIG)
    gmax = jnp.max(gm, axis=-1, keepdims=True)
    gidx = jnp.min(jnp.where(gm == gmax, lane, 1e9), axis=-1, keepdims=True) - N_EXPERTS
    p_group = 1.0 / jnp.sum(jnp.where(is_grp, jnp.exp(gm - gmax), 0.0), axis=-1, keepdims=True)
    lo_lane = gidx * EXPERTS_PER_GROUP
    in_grp = jnp.logical_and(lane >= lo_lane, lane < lo_lane + EXPERTS_PER_GROUP)
    e1v = jnp.where(in_grp, lg, NEG_BIG)
    t1 = jnp.max(e1v, axis=-1, keepdims=True)
    i1 = jnp.min(jnp.where(e1v == t1, lane, 1e9), axis=-1, keepdims=True)
    e2v = jnp.where(lane == i1, NEG_BIG, e1v)
    t2 = jnp.max(e2v, axis=-1, keepdims=True)
    i2 = jnp.min(jnp.where(e2v == t2, lane, 1e9), axis=-1, keepdims=True)
    d = jnp.exp(t2 - t1)
    w1 = p_group / (1.0 + d)
    w2 = p_group * d / (1.0 + d)
    route_ref[...] = jnp.where(lane == 0, i1, jnp.where(lane == 1, i2,
                               jnp.where(lane == 2, w1, jnp.where(lane == 3, w2, 0.0))))

    onehot = (lane == i1).astype(F32) + (lane == i2).astype(F32)

    @pl.when(pl.program_id(0) == 0)
    def _():
        cnt_ref[...] = jnp.zeros(cnt_ref.shape, F32)

    cnt_ref[...] += jnp.sum(onehot, axis=0, keepdims=True)


def _merge(attn, hy, gates, x, gate1, shift2, scale2, gain, wa, wh, wo, wrh, wrl, br, nb, seq):
    t = nb * seq
    tm = min(512, seq)
    tps = seq // tm
    row = lambda i: (i, 0)
    per_b = lambda i: (i // tps, 0, 0)
    c2 = lambda i: (0, 0)
    return pl.pallas_call(
        _merge_kernel,
        out_shape=(jax.ShapeDtypeStruct((t, D_MODEL), F32),
                   jax.ShapeDtypeStruct((t * ROW_TILES, LANES), F32),
                   jax.ShapeDtypeStruct((t, LANES), F32),
                   jax.ShapeDtypeStruct((1, LANES), F32)),
        grid=(t // tm,),
        in_specs=[pl.BlockSpec((tm, ATTN_WIDTH), row),
                  pl.BlockSpec((tm, D_HYENA), row),
                  pl.BlockSpec((tm, 2 * D_MODEL), row),
                  pl.BlockSpec((tm, D_MODEL), row),
                  pl.BlockSpec((1, 1, D_MODEL), per_b),
                  pl.BlockSpec((1, 1, D_MODEL), per_b),
                  pl.BlockSpec((1, 1, D_MODEL), per_b),
                  pl.BlockSpec((1, D_MODEL), c2),
                  pl.BlockSpec(wa.shape, c2), pl.BlockSpec(wh.shape, c2), pl.BlockSpec(wo.shape, c2),
                  pl.BlockSpec(wrh.shape, c2), pl.BlockSpec(wrl.shape, c2), pl.BlockSpec(br.shape, c2)],
        out_specs=(pl.BlockSpec((tm, D_MODEL), row),
                   pl.BlockSpec((tm * ROW_TILES, LANES), row),
                   pl.BlockSpec((tm, LANES), row),
                   pl.BlockSpec((1, LANES), c2)),
        compiler_params=_cparams(("arbitrary",)),
        name="merge_router",
    )(attn, hy, gates, x, gate1, shift2, scale2, gain, wa, wh, wo, wrh, wrl, br)


def _rank_kernel(route_ref, pstart_ref, tri_ref, dest_ref, carry_ref):
    @pl.when(pl.program_id(0) == 0)
    def _():
        carry_ref[...] = jnp.zeros(carry_ref.shape, F32)

    r = route_ref[...]
    lane = lax.broadcasted_iota(jnp.int32, r.shape, 1).astype(F32)
    oh1 = (lane == r[:, 0:1]).astype(F32)
    oh2 = (lane == r[:, 1:2]).astype(F32)
    tri = tri_ref[...]
    before1 = _dotbf(tri, oh1.astype(BF16))
    before2 = _dotbf(tri, oh2.astype(BF16))
    base1 = pstart_ref[...] + carry_ref[...]
    d1 = jnp.sum(oh1 * (base1 + before1), axis=-1, keepdims=True)
    base2 = base1 + jnp.sum(oh1, axis=0, keepdims=True)
    d2 = jnp.sum(oh2 * (base2 + before2), axis=-1, keepdims=True)
    carry_ref[...] = base2 + jnp.sum(oh2, axis=0, keepdims=True) - pstart_ref[...]
    dest_ref[...] = jnp.where(lane == 0, d1, jnp.where(lane == 1, d2, 0.0))


def _rank(route, pstart, tri):
    t = route.shape[0]
    tm = tri.shape[0]
    return pl.pallas_call(
        _rank_kernel,
        out_shape=jax.ShapeDtypeStruct((t, LANES), F32),
        grid=(t // tm,),
        in_specs=[pl.BlockSpec((tm, LANES), lambda i: (i, 0)),
                  pl.BlockSpec((1, LANES), lambda i: (0, 0)),
                  pl.BlockSpec((tm, tm), lambda i: (0, 0))],
        out_specs=pl.BlockSpec((tm, LANES), lambda i: (i, 0)),
        scratch_shapes=[pltpu.VMEM((1, LANES), F32)],
        compiler_params=_cparams(("arbitrary",)),
        name="moe_rank",
    )(route, pstart, tri)


def _row_tile(ref, r):
    return ref.at[pl.ds(pl.multiple_of(r * ROW_TILES, ROW_TILES), ROW_TILES)]


def _rows_from_tiles(ref, n):
    return jnp.concatenate([ref[pl.ds(s, n, stride=ROW_TILES), :] for s in range(ROW_TILES)], axis=1)


def _rows_to_tiles(ref, x):
    n = x.shape[0]
    for s in range(ROW_TILES):
        ref[pl.ds(s, n, stride=ROW_TILES), :] = x[:, s * LANES:(s + 1) * LANES]


def _dispatch_kernel(d1_ref, d2_ref, h_ref, init_ref, xs_ref, sem):
    del init_ref
    tm = h_ref.shape[0] // ROW_TILES
    base = pl.program_id(0) * tm

    def copies(r):
        src = _row_tile(h_ref, r)
        return (pltpu.make_async_copy(src, _row_tile(xs_ref, d1_ref[base + r]), sem),
                pltpu.make_async_copy(src, _row_tile(xs_ref, d2_ref[base + r]), sem))

    def issue(r, c):
        a, b = copies(r)
        a.start()
        b.start()
        return c

    def drain(r, c):
        a, b = copies(r)
        a.wait()
        b.wait()
        return c

    lax.fori_loop(0, tm, issue, 0, unroll=8)
    lax.fori_loop(0, tm, drain, 0, unroll=8)


def _dispatch(d1, d2, h2, cap):
    t = h2.shape[0] // ROW_TILES
    tm = min(256, t)
    init = jnp.zeros((cap * ROW_TILES, LANES), h2.dtype)
    return pl.pallas_call(
        _dispatch_kernel,
        out_shape=jax.ShapeDtypeStruct((cap * ROW_TILES, LANES), h2.dtype),
        grid_spec=pltpu.PrefetchScalarGridSpec(
            num_scalar_prefetch=2,
            grid=(t // tm,),
            in_specs=[pl.BlockSpec((tm * ROW_TILES, LANES), lambda i, a, b: (i, 0)),
                      pl.BlockSpec(memory_space=pl.ANY)],
            out_specs=pl.BlockSpec(memory_space=pl.ANY),
            scratch_shapes=[pltpu.SemaphoreType.DMA(())]),
        input_output_aliases={3: 0},
        compiler_params=_cparams(("arbitrary",)),
        name="moe_dispatch",
    )(d1, d2, h2, init)


def _expert_kernel(be_ref, xs_ref, wgu_ref, wd_ref, ys_ref):
    del be_ref
    x = _rows_from_tiles(xs_ref, EXPERT_BLOCK).astype(BF16)
    gu = _dotbf(x, wgu_ref[0])
    g = gu[:, :D_EXPERT]
    u = gu[:, D_EXPERT:]
    a = (g * jax.nn.sigmoid(g) * u).astype(BF16)
    _rows_to_tiles(ys_ref, _dotbf(a, wd_ref[0]))


def _experts(blk_expert, xs, wgu, wd):
    cap = xs.shape[0] // ROW_TILES
    d = wgu.shape[1]
    blk = pl.BlockSpec((EXPERT_BLOCK * ROW_TILES, LANES), lambda i, be: (i, 0))
    return pl.pallas_call(
        _expert_kernel,
        out_shape=jax.ShapeDtypeStruct(xs.shape, F32),
        grid_spec=pltpu.PrefetchScalarGridSpec(
            num_scalar_prefetch=1,
            grid=(cap // EXPERT_BLOCK,),
            in_specs=[blk,
                      pl.BlockSpec((1, d, 2 * D_EXPERT), lambda i, be: (be[i], 0, 0)),
                      pl.BlockSpec((1, D_EXPERT, d), lambda i, be: (be[i], 0, 0))],
            out_specs=blk),
        compiler_params=_cparams(("arbitrary",)),
        name="moe_experts",
    )(blk_expert, xs, wgu, wd)


def _combine_kernel(d1_ref, d2_ref, ys_ref, x_ref, route_ref, gate_ref, o_ref, y1_ref, y2_ref, sem):
    tm = x_ref.shape[0]
    base = pl.program_id(0) * tm

    def copies(r):
        return (pltpu.make_async_copy(_row_tile(ys_ref, d1_ref[base + r]), _row_tile(y1_ref, r), sem),
                pltpu.make_async_copy(_row_tile(ys_ref, d2_ref[base + r]), _row_tile(y2_ref, r), sem))

    def issue(r, c):
        a, b = copies(r)
        a.start()
        b.start()
        return c

    def drain(r, c):
        a, b = copies(r)
        a.wait()
        b.wait()
        return c

    lax.fori_loop(0, tm, issue, 0, unroll=8)
    lax.fori_loop(0, tm, drain, 0, unroll=8)
    r = route_ref[...]
    ffn = _rows_from_tiles(y1_ref, tm) * r[:, 2:3] + _rows_from_tiles(y2_ref, tm) * r[:, 3:4]
    o_ref[...] = x_ref[...] + gate_ref[0] * ffn


def _combine(d1, d2, ys, x1, route, gate2, nb, seq):
    t, d = x1.shape
    tm = min(256, seq)
    tps = seq // tm
    return pl.pallas_call(
        _combine_kernel,
        out_shape=jax.ShapeDtypeStruct((t, d), F32),
        grid_spec=pltpu.PrefetchScalarGridSpec(
            num_scalar_prefetch=2,
            grid=(t // tm,),
            in_specs=[pl.BlockSpec(memory_space=pl.ANY),
                      pl.BlockSpec((tm, d), lambda i, a, b: (i, 0)),
                      pl.BlockSpec((tm, LANES), lambda i, a, b: (i, 0)),
                      pl.BlockSpec((1, 1, d), lambda i, a, b: (i // tps, 0, 0))],
            out_specs=pl.BlockSpec((tm, d), lambda i, a, b: (i, 0)),
            scratch_shapes=[pltpu.VMEM((tm * ROW_TILES, LANES), F32), pltpu.VMEM((tm * ROW_TILES, LANES), F32),
                            pltpu.SemaphoreType.DMA(())]),
        compiler_params=_cparams(("arbitrary",)),
        name="moe_combine",
    )(d1, d2, ys, x1, route, gate2)


def _rope_tables(seq):
    rows = seq // GRID_W
    row = jnp.broadcast_to(jnp.arange(rows, dtype=F32)[:, None], (rows, GRID_W)).reshape(-1)
    col = jnp.broadcast_to(jnp.arange(GRID_W, dtype=F32)[None, :], (rows, GRID_W)).reshape(-1)
    inv_freq = ROPE_THETA ** (-jnp.arange(ROPE_PAIRS_PER_AXIS, dtype=F32) / ROPE_PAIRS_PER_AXIS)
    ang = jnp.concatenate([row[:, None] * inv_freq, col[:, None] * inv_freq], axis=-1)
    cos = jnp.repeat(jnp.cos(ang), 2, axis=-1)
    sin = jnp.repeat(jnp.sin(ang), 2, axis=-1)
    sign = jnp.tile(jnp.array([-1.0, 1.0], F32), HEAD_DIM // 2)
    return jnp.tile(cos, (1, N_HEADS)), jnp.tile(sin * sign, (1, N_HEADS))


def _filter_features(seq):
    t = jnp.linspace(0.0, 1.0, seq, dtype=F32)[:, None]
    w = (2.0 * math.pi / seq) * jnp.arange(seq, dtype=F32)[:, None]
    bands = jnp.linspace(1e-4, FILTER_BANDS - 1, FILTER_BANDS, dtype=F32)[None, :]
    z = jnp.concatenate([t, jnp.cos(bands * w), -jnp.sin(bands * w)], axis=-1)
    pad = jnp.zeros((seq, FILTER_ORDER - FILTER_EMB), F32)
    fwd = jnp.concatenate([z, pad], axis=-1).at[:, FILTER_EMB].set(1.0)
    rev = jnp.concatenate([z[:1], z[:0:-1]], axis=0)
    sign = jnp.concatenate([jnp.zeros((1,), F32), -jnp.ones((seq - 1,), F32)])
    bwd = jnp.concatenate([rev, pad], axis=-1).at[:, FILTER_EMB].set(sign)
    deltas = jnp.abs(jnp.linspace(MIN_DECAY, MAX_DECAY, D_HYENA, dtype=F32))[None, :]
    return jnp.concatenate([fwd, bwd], axis=-1), deltas


def _run_trunk(x, mod, p):
    nb, seq, d = x.shape
    t = nb * seq
    depth = mod.shape[0]
    cos, sin = _rope_tables(seq)
    zfeat, deltas = _filter_features(seq)
    tabs = _dft_tables(seq)
    h1 = tabs["h1"]
    tm_rank = min(512, t)
    tri = jnp.tril(jnp.ones((tm_rank, tm_rank), BF16), -1)
    cap = t * TOP_K + N_EXPERTS * EXPERT_BLOCK
    n_blk = cap // EXPERT_BLOCK
    lane_e = jnp.arange(LANES)

    xf = x.reshape(t, d)
    for l in range(depth):
        m6 = mod[l].reshape(nb, 6, 1, d)
        shift1, scale1, gate1, shift2, scale2, gate2 = (m6[:, j] for j in range(6))

        q, k, vt, u, gates = _inproj(xf, shift1, scale1, p["norm_mix"][l], p["w_in"][l], p["w_vt"][l],
                                     p["qg"][l], p["kg"][l], p["bd"], cos, sin, nb, seq)
        bound = HEAD_DIM * jnp.max(jnp.abs(p["qg"][l])) * jnp.max(jnp.abs(p["kg"][l]))
        attn = lax.cond(bound <= SCORE_BOUND,
                        functools.partial(_attention, online=False),
                        functools.partial(_attention, online=True), q, k, vt).reshape(t, ATTN_WIDTH)

        z, x2 = _short_conv(u.reshape(nb, seq, 3 * D_HYENA), p["conv_w"][l], p["conv_b"][l])
        kern = _filters(zfeat, p["filt_w1"][l], p["filt_b1"][l], p["filt_w2"][l], p["filt_b2"][l],
                        p["filt_w3"][l], p["filt_b3"][l], p["filt_freq"][l], p["filt_w4"][l], deltas)
        ka = _fft_outer(tabs["g_full"], kern.reshape(1, 2 * seq, D_HYENA), FFT_N2)
        kf = _fft_filter_inner(tabs["mf"], ka.reshape(1, 2, h1, FFT_N2, D_HYENA))
        za = _fft_outer(tabs["g_half"], z, FFT_N2)
        zb = _fft_inner(tabs["mf"], tabs["mi"], kf, za.reshape(nb, 2, h1, FFT_N2, D_HYENA))
        hy = _fft_final(tabs["g_out"], zb.reshape(nb, 2 * h1 * FFT_N2, D_HYENA), z, x2,
                        p["skip"][l], FFT_N2).reshape(t, D_HYENA)

        x1, h2, route, counts = _merge(attn, hy, gates, xf, gate1, shift2, scale2, p["norm_ffn"][l],
                                       p["w_br_attn"][l], p["w_br_hyena"][l], p["w_out"][l],
                                       p["wr_hi"][l], p["wr_lo"][l], p["b_route"][l], nb, seq)

        cnt = counts[0].astype(jnp.int32)
        padded = jnp.where(lane_e < N_EXPERTS, (cnt + EXPERT_BLOCK - 1) // EXPERT_BLOCK * EXPERT_BLOCK, 0)
        pad_end = jnp.cumsum(padded)
        pstart = (pad_end - padded).astype(F32)[None, :]
        blk_start = jnp.arange(n_blk, dtype=jnp.int32) * EXPERT_BLOCK
        blk_expert = jnp.minimum(
            jnp.sum(blk_start[:, None] >= pad_end[None, :N_EXPERTS], axis=1), N_EXPERTS - 1).astype(jnp.int32)

        dest = _rank(route, pstart, tri)
        d1 = dest[:, 0].astype(jnp.int32)
        d2 = dest[:, 1].astype(jnp.int32)
        xs = _dispatch(d1, d2, h2, cap)
        ys = _experts(blk_expert, xs, p["w_gu"][l], p["w_down"][l])
        xf = _combine(d1, d2, ys, x1, route, gate2, nb, seq)
    return xf.reshape(nb, seq, d)


def kernel(x_prompt, x_sample, c_prompt, c_sample, w_ada, b_ada, norm_mix, norm_ffn, w_in, q_gain, k_gain, conv_w, conv_b, filt_w1, filt_b1, filt_w2, filt_b2, filt_w3, filt_b3, filt_freq, filt_w4, hyena_skip, w_br_attn, w_br_hyena, w_out, w_group, b_group, w_router, b_router, w_e_gate, w_e_up, w_e_down):
    depth = w_ada.shape[0]
    bp, bs = c_prompt.shape[0], c_sample.shape[0]
    rows = -(-(bp + bs) // SUBLANES) * SUBLANES
    c_pad = jnp.zeros((rows, D_MODEL), F32).at[:bp].set(c_prompt).at[bp:bp + bs].set(c_sample)
    mod = _ada(c_pad, w_ada, b_ada)

    scale = HEAD_DIM ** -0.5 * math.log2(math.e)
    head_id = np.arange(ATTN_WIDTH) // HEAD_DIM
    route_w = jnp.concatenate([w_router, w_group], axis=-1)
    route_w = jnp.pad(route_w, ((0, 0), (0, 0), (0, LANES - route_w.shape[-1])))
    wr_hi = route_w.astype(BF16)

    def block_diag(a, b):
        return jnp.concatenate([jnp.pad(a, ((0, 0), (0, 0), (0, b.shape[2]))),
                                jnp.pad(b, ((0, 0), (0, 0), (a.shape[2], 0)))], axis=1)

    twice = lambda w: block_diag(w, w)
    p = dict(
        norm_mix=norm_mix.reshape(depth, 1, D_MODEL),
        norm_ffn=norm_ffn.reshape(depth, 1, D_MODEL),
        w_in=w_in.astype(BF16),
        w_vt=jnp.swapaxes(w_in[:, :, COL_V:COL_U], 1, 2).astype(BF16),
        qg=(jnp.tile(q_gain, (1, N_HEADS)) * scale).reshape(depth, 1, ATTN_WIDTH),
        kg=jnp.tile(k_gain, (1, N_KV_HEADS)).reshape(depth, 1, KV_WIDTH),
        bd=jnp.asarray(head_id[:, None] == head_id[None, :], dtype=BF16),
        conv_w=conv_w, conv_b=conv_b,
        filt_w1=twice(jnp.pad(filt_w1, ((0, 0), (0, FILTER_ORDER - FILTER_EMB), (0, 0)))),
        filt_b1=jnp.tile(filt_b1, (1, 2)).reshape(depth, 1, 2 * FILTER_ORDER),
        filt_w2=twice(filt_w2), filt_b2=jnp.tile(filt_b2, (1, 2)).reshape(depth, 1, 2 * FILTER_ORDER),
        filt_w3=twice(filt_w3), filt_b3=jnp.tile(filt_b3, (1, 2)).reshape(depth, 1, 2 * FILTER_ORDER),
        filt_freq=jnp.tile(filt_freq, (1, 2)).reshape(depth, 1, 2 * FILTER_ORDER),
        filt_w4=block_diag(filt_w4[:, :, :D_HYENA], filt_w4[:, :, D_HYENA:]),
        skip=hyena_skip.reshape(depth, 1, D_HYENA),
        w_br_attn=w_br_attn.astype(BF16), w_br_hyena=w_br_hyena.astype(BF16), w_out=w_out.astype(BF16),
        wr_hi=wr_hi, wr_lo=(route_w - wr_hi.astype(F32)).astype(BF16),
        b_route=jnp.pad(jnp.concatenate([b_router, b_group], axis=-1),
                        ((0, 0), (0, LANES - N_EXPERTS - N_GROUPS))).reshape(depth, 1, LANES),
        w_gu=jnp.concatenate([w_e_gate, w_e_up], axis=-1).astype(BF16),
        w_down=w_e_down.astype(BF16),
    )
    y_prompt = _run_trunk(x_prompt, mod[:, :bp], p)
    y_sample = _run_trunk(x_sample, mod[:, bp:bp + bs], p)
    return (y_prompt, y_sample)
```

```python
import functools
import math

import numpy as np
import jax
import jax.numpy as jnp
from jax import lax
from jax.experimental import pallas as pl
from jax.experimental.pallas import tpu as pltpu

F32 = jnp.float32
BF16 = jnp.bfloat16

D_MODEL = 1024
GRID_W = 64
N_HEADS = 8
N_KV_HEADS = 4
HEAD_DIM = 64
Q_PER_KV = N_HEADS // N_KV_HEADS
ATTN_WIDTH = N_HEADS * HEAD_DIM
KV_WIDTH = N_KV_HEADS * HEAD_DIM
ROPE_THETA = 10000.0
ROPE_PAIRS_PER_AXIS = HEAD_DIM // 4
D_HYENA = D_MODEL // 2
FILTER_EMB = 33
FILTER_BANDS = (FILTER_EMB - 1) // 2
FILTER_ORDER = 64
DECAY_TARGET = 1e-2
MAX_DECAY = math.log(DECAY_TARGET) / 0.3
MIN_DECAY = math.log(DECAY_TARGET) / 1.5
MOD_SHIFT = 0.05
N_GROUPS = 4
EXPERTS_PER_GROUP = 8
N_EXPERTS = N_GROUPS * EXPERTS_PER_GROUP
TOP_K = 2
D_EXPERT = D_MODEL // 4
RMS_EPS = 1e-6
IN_COLS = ATTN_WIDTH + 2 * KV_WIDTH + 3 * D_HYENA + 2 * D_MODEL
COL_K = ATTN_WIDTH
COL_V = ATTN_WIDTH + KV_WIDTH
COL_U = ATTN_WIDTH + 2 * KV_WIDTH
COL_G = COL_U + 3 * D_HYENA

LANES = 128
SUBLANES = 8
ROW_TILES = D_MODEL // LANES
VMEM_LIMIT = 56 * 1024 * 1024

V_ROWS = HEAD_DIM + 16
SCORE_BOUND = 60.0
FFT_N2 = 128
EXPERT_BLOCK = 512
NEG_BIG = -1e30

_HI = lax.Precision.HIGHEST


def _dot32(a, b):
    return jnp.dot(a, b, precision=_HI, preferred_element_type=F32)


def _dotbf(a, b):
    return jnp.dot(a, b, preferred_element_type=F32)


def _cparams(sem):
    return pltpu.CompilerParams(dimension_semantics=sem, vmem_limit_bytes=VMEM_LIMIT)


def _ada_kernel(c_ref, w_ref, b_ref, o_ref):
    c = c_ref[...]
    act = c * jax.nn.sigmoid(c)
    o_ref[0] = _dot32(act, w_ref[0]) + b_ref[0]


def _ada(c_pad, w_ada, b_ada):
    depth, d, n = w_ada.shape
    tn = 1536
    return pl.pallas_call(
        _ada_kernel,
        out_shape=jax.ShapeDtypeStruct((depth, c_pad.shape[0], n), F32),
        grid=(depth, n // tn),
        in_specs=[pl.BlockSpec(c_pad.shape, lambda l, j: (0, 0)),
                  pl.BlockSpec((1, d, tn), lambda l, j: (l, 0, j)),
                  pl.BlockSpec((1, 1, tn), lambda l, j: (l, 0, j))],
        out_specs=pl.BlockSpec((1, c_pad.shape[0], tn), lambda l, j: (l, 0, j)),
        compiler_params=_cparams(("arbitrary", "arbitrary")),
        name="ada_mod",
    )(c_pad, w_ada, b_ada.reshape(depth, 1, n))


def _swap_pairs(x):
    n = x.shape[-1]
    lane = lax.broadcasted_iota(jnp.int32, x.shape, 1)
    nxt = pltpu.roll(x, n - 1, 1)
    prv = pltpu.roll(x, 1, 1)
    return jnp.where(lane % 2 == 0, nxt, prv)


def _head_norm_rope(p, gain, bd, cos, sin_signed):
    sq = (p * p).astype(BF16)
    ms = _dotbf(sq, bd) * (1.0 / HEAD_DIM)
    pn = p * lax.rsqrt(ms + RMS_EPS) * gain
    return pn * cos + _swap_pairs(pn) * sin_signed


def _inproj_kernel(x_ref, shift_ref, scale_ref, gain_ref, w_ref, wvt_ref, qg_ref, kg_ref, bd_ref,
                   cos_ref, sin_ref, q_ref, k_ref, vt_ref, u_ref, g_ref):
    x = x_ref[...]
    ms = jnp.mean(x * x, axis=-1, keepdims=True)
    h = x * lax.rsqrt(ms + RMS_EPS) * gain_ref[...]
    h = h * (1.0 + scale_ref[0]) + shift_ref[0]
    hb = h.astype(BF16)
    cos = cos_ref[...]
    sin = sin_ref[...]
    bd = bd_ref[...]

    q = _dotbf(hb, w_ref[:, 0:COL_K])
    q = _head_norm_rope(q, qg_ref[...], bd, cos, sin)
    for hd in range(N_HEADS):
        q_ref[0, hd] = q[:, hd * HEAD_DIM:(hd + 1) * HEAD_DIM].astype(BF16)

    k = _dotbf(hb, w_ref[:, COL_K:COL_V])
    k = _head_norm_rope(k, kg_ref[...], bd[:KV_WIDTH, :KV_WIDTH], cos[:, :KV_WIDTH], sin[:, :KV_WIDTH])
    vt = lax.dot_general(wvt_ref[...], hb, (((1,), (1,)), ((), ())), preferred_element_type=F32)
    ones = jnp.ones((V_ROWS - HEAD_DIM, vt.shape[1]), BF16)
    for hd in range(N_KV_HEADS):
        k_ref[0, hd] = k[:, hd * HEAD_DIM:(hd + 1) * HEAD_DIM].astype(BF16)
        vt_ref[0, hd, 0:HEAD_DIM, :] = vt[hd * HEAD_DIM:(hd + 1) * HEAD_DIM].astype(BF16)
        vt_ref[0, hd, HEAD_DIM:V_ROWS, :] = ones

    u_ref[...] = _dotbf(hb, w_ref[:, COL_U:COL_G])
    g_ref[...] = jax.nn.sigmoid(_dotbf(hb, w_ref[:, COL_G:IN_COLS])).astype(BF16)


def _inproj(x, shift, scale, gain, w_in, w_vt, qg, kg, bd, cos, sin, nb, seq):
    t = nb * seq
    tm = min(512, seq)
    tps = seq // tm
    row = lambda i: (i, 0)
    per_b = lambda i: (i // tps, 0, 0)
    const2 = lambda i: (0, 0)
    pos = lambda i: (i % tps, 0)
    head_out = lambda i: (i // tps, 0, i % tps, 0)
    return pl.pallas_call(
        _inproj_kernel,
        out_shape=(jax.ShapeDtypeStruct((nb, N_HEADS, seq, HEAD_DIM), BF16),
                   jax.ShapeDtypeStruct((nb, N_KV_HEADS, seq, HEAD_DIM), BF16),
                   jax.ShapeDtypeStruct((nb, N_KV_HEADS, V_ROWS, seq), BF16),
                   jax.ShapeDtypeStruct((t, 3 * D_HYENA), F32),
                   jax.ShapeDtypeStruct((t, 2 * D_MODEL), BF16)),
        grid=(t // tm,),
        in_specs=[pl.BlockSpec((tm, D_MODEL), row),
                  pl.BlockSpec((1, 1, D_MODEL), per_b),
                  pl.BlockSpec((1, 1, D_MODEL), per_b),
                  pl.BlockSpec((1, D_MODEL), const2),
                  pl.BlockSpec((D_MODEL, IN_COLS), const2),
                  pl.BlockSpec((KV_WIDTH, D_MODEL), const2),
                  pl.BlockSpec((1, ATTN_WIDTH), const2),
                  pl.BlockSpec((1, KV_WIDTH), const2),
                  pl.BlockSpec((ATTN_WIDTH, ATTN_WIDTH), const2),
                  pl.BlockSpec((tm, ATTN_WIDTH), pos),
                  pl.BlockSpec((tm, ATTN_WIDTH), pos)],
        out_specs=(pl.BlockSpec((1, N_HEADS, tm, HEAD_DIM), head_out),
                   pl.BlockSpec((1, N_KV_HEADS, tm, HEAD_DIM), head_out),
                   pl.BlockSpec((1, N_KV_HEADS, V_ROWS, tm), lambda i: (i // tps, 0, 0, i % tps)),
                   pl.BlockSpec((tm, 3 * D_HYENA), row),
                   pl.BlockSpec((tm, 2 * D_MODEL), row)),
        compiler_params=_cparams(("arbitrary",)),
        name="in_proj",
    )(x, shift, scale, gain, w_in, w_vt, qg, kg, bd, cos, sin)


def _attn_kernel(q_ref, k_ref, vt_ref, o_ref, m_ref, acc_ref, *, tkc, online):
    seq = k_ref.shape[2]
    nchunk = seq // tkc
    acc_ref[...] = jnp.zeros(acc_ref.shape, F32)
    if online:
        m_ref[...] = jnp.full(m_ref.shape, NEG_BIG, F32)

    def body(c, carry):
        off = pl.multiple_of(c * tkc, tkc)
        kc = k_ref[0, 0, pl.ds(off, tkc), :]
        vc = vt_ref[0, 0, :, pl.ds(off, tkc)]
        for h in range(Q_PER_KV):
            st = lax.dot_general(kc, q_ref[0, h], (((1,), (1,)), ((), ())), preferred_element_type=F32)
            if online:
                m_prev = m_ref[h]
                m_new = jnp.maximum(m_prev, jnp.max(st, axis=0, keepdims=True))
                p = jnp.exp2(st - m_new).astype(BF16)
                acc_ref[h] = jnp.exp2(m_prev - m_new) * acc_ref[h] + _dotbf(vc, p)
                m_ref[h] = m_new
            else:
                acc_ref[h] += _dotbf(vc, jnp.exp2(st).astype(BF16))
        return carry

    lax.fori_loop(0, nchunk, body, 0, unroll=2 if nchunk % 2 == 0 else 1)
    outs = []
    for h in range(Q_PER_KV):
        a = acc_ref[h]
        outs.append((a[:HEAD_DIM] / a[HEAD_DIM:HEAD_DIM + 1]).T)
    o_ref[0] = jnp.concatenate(outs, axis=1).astype(BF16)


def _attention(q, k, vt, online):
    nb, _, seq, _ = q.shape
    tq = min(512 if online else 1024, seq)
    tkc = min(512 if online else 1024, seq)
    return pl.pallas_call(
        functools.partial(_attn_kernel, tkc=tkc, online=online),
        out_shape=jax.ShapeDtypeStruct((nb, seq, ATTN_WIDTH), BF16),
        grid=(nb, N_KV_HEADS, seq // tq),
        in_specs=[pl.BlockSpec((1, Q_PER_KV, tq, HEAD_DIM), lambda b, g, i: (b, g, i, 0)),
                  pl.BlockSpec((1, 1, seq, HEAD_DIM), lambda b, g, i: (b, g, 0, 0)),
                  pl.BlockSpec((1, 1, V_ROWS, seq), lambda b, g, i: (b, g, 0, 0))],
        out_specs=pl.BlockSpec((1, tq, Q_PER_KV * HEAD_DIM), lambda b, g, i: (b, i, g)),
        scratch_shapes=[pltpu.VMEM((Q_PER_KV, 1, tq), F32),
                        pltpu.VMEM((Q_PER_KV, V_ROWS, tq), F32)],
        compiler_params=_cparams(("arbitrary", "arbitrary", "arbitrary")),
        name="attn_online" if online else "attn_bounded",
    )(q, k, vt)


def _conv_kernel(u_ref, up_ref, un_ref, w_ref, b_ref, z_ref, x2_ref):
    i = pl.program_id(1)
    last = pl.num_programs(1) - 1
    u = u_ref[0]
    tr = u.shape[0]
    prev_row = up_ref[0][SUBLANES - 1:SUBLANES] * (i > 0).astype(F32)
    next_row = un_ref[0][0:1] * (i < last).astype(F32)
    ridx = lax.broadcasted_iota(jnp.int32, u.shape, 0)
    u_prev = jnp.where(ridx == 0, prev_row, pltpu.roll(u, 1, 0))
    u_next = jnp.where(ridx == tr - 1, next_row, pltpu.roll(u, tr - 1, 0))
    w = w_ref[...]
    c = b_ref[...] + u_prev * w[0:1] + u * w[1:2] + u_next * w[2:3]
    x1 = c[:, 0:D_HYENA]
    x2 = c[:, D_HYENA:2 * D_HYENA]
    vh = c[:, 2 * D_HYENA:3 * D_HYENA]
    z_ref[0] = vh * x1
    x2_ref[0] = x2


def _short_conv(u, conv_w, conv_b):
    nb, seq, ch = u.shape
    tr = min(512, seq)
    nblk8 = seq // SUBLANES
    rpb = tr // SUBLANES
    return pl.pallas_call(
        _conv_kernel,
        out_shape=(jax.ShapeDtypeStruct((nb, seq, D_HYENA), F32),
                   jax.ShapeDtypeStruct((nb, seq, D_HYENA), F32)),
        grid=(nb, seq // tr),
        in_specs=[pl.BlockSpec((1, tr, ch), lambda b, i: (b, i, 0)),
                  pl.BlockSpec((1, SUBLANES, ch), lambda b, i: (b, jnp.maximum(i * rpb - 1, 0), 0)),
                  pl.BlockSpec((1, SUBLANES, ch),
                               lambda b, i: (b, jnp.minimum((i + 1) * rpb, nblk8 - 1), 0)),
                  pl.BlockSpec((3, ch), lambda b, i: (0, 0)),
                  pl.BlockSpec((1, ch), lambda b, i: (0, 0))],
        out_specs=(pl.BlockSpec((1, tr, D_HYENA), lambda b, i: (b, i, 0)),
                   pl.BlockSpec((1, tr, D_HYENA), lambda b, i: (b, i, 0))),
        compiler_params=_cparams(("arbitrary", "arbitrary")),
        name="short_conv",
    )(u, u, u, conv_w, conv_b.reshape(1, ch))


def _filter_kernel(z_ref, w1_ref, b1_ref, w2_ref, b2_ref, w3_ref, b3_ref, fr_ref, w4_ref,
                   dl_ref, o_ref):
    z = z_ref[...]
    fr = fr_ref[...]
    h = jnp.sin(fr * (_dot32(z, w1_ref[...]) + b1_ref[...]))
    h = jnp.sin(fr * (_dot32(h, w2_ref[...]) + b2_ref[...]))
    h = jnp.sin(fr * (_dot32(h, w3_ref[...]) + b3_ref[...]))
    h = _dot32(h, w4_ref[...])
    dl = dl_ref[...]
    for d in range(2):
        t = z[:, d * FILTER_ORDER:d * FILTER_ORDER + 1]
        sign = z[:, d * FILTER_ORDER + FILTER_EMB:d * FILTER_ORDER + FILTER_EMB + 1]
        o_ref[d] = h[:, d * D_HYENA:(d + 1) * D_HYENA] * (jnp.exp(-t * dl) + MOD_SHIFT) * sign


def _filters(zfeat, w1, b1, w2, b2, w3, b3, fr, w4, deltas):
    seq, fe = zfeat.shape
    tm = min(512, seq)
    c2 = lambda i: (0, 0)
    return pl.pallas_call(
        _filter_kernel,
        out_shape=jax.ShapeDtypeStruct((2, seq, D_HYENA), F32),
        grid=(seq // tm,),
        in_specs=[pl.BlockSpec((tm, fe), lambda i: (i, 0)),
                  pl.BlockSpec(w1.shape, c2), pl.BlockSpec(b1.shape, c2),
                  pl.BlockSpec(w2.shape, c2), pl.BlockSpec(b2.shape, c2),
                  pl.BlockSpec(w3.shape, c2), pl.BlockSpec(b3.shape, c2),
                  pl.BlockSpec(fr.shape, c2), pl.BlockSpec(w4.shape, c2),
                  pl.BlockSpec(deltas.shape, c2)],
        out_specs=pl.BlockSpec((2, tm, D_HYENA), lambda i: (0, i, 0)),
        compiler_params=_cparams(("arbitrary",)),
        name="hyena_filters",
    )(zfeat, w1, b1, w2, b2, w3, b3, fr, w4, deltas)


def _fft_outer_kernel(g_ref, x_ref, o_ref, *, n2):
    g = g_ref[...]
    m, kk = g.shape

    def body(j, c):
        xj = x_ref[pl.ds(j, kk, stride=n2), :]
        o_ref[pl.ds(j, m, stride=n2), :] = _dotbf(g, xj.astype(BF16))
        return c

    lax.fori_loop(0, n2, body, 0, unroll=8)


def _fft_outer(gmat, x, n2):
    nb, rows, ch = x.shape
    m, kk = gmat.shape
    return pl.pallas_call(
        functools.partial(_fft_outer_kernel, n2=n2),
        out_shape=jax.ShapeDtypeStruct((nb, m * n2, ch), F32),
        grid=(nb, ch // LANES),
        in_specs=[pl.BlockSpec((m, kk), lambda b, c: (0, 0)),
                  pl.BlockSpec((None, rows, LANES), lambda b, c: (b, 0, c))],
        out_specs=pl.BlockSpec((None, m * n2, LANES), lambda b, c: (b, 0, c)),
        compiler_params=_cparams(("arbitrary", "arbitrary")),
        name="fft_outer",
    )(gmat, x)


def _fft_filter_inner_kernel(mf_ref, a_ref, o_ref):
    n2, ch = a_ref.shape[3], a_ref.shape[4]
    a = a_ref[0, :, 0].reshape(2 * n2, ch)
    o_ref[0] = _dotbf(mf_ref[0], a.astype(BF16)).reshape(2, n2, ch)


def _fft_filter_inner(mf, a5):
    _, _, h1, n2, ch = a5.shape
    return pl.pallas_call(
        _fft_filter_inner_kernel,
        out_shape=jax.ShapeDtypeStruct((h1, 2, n2, ch), F32),
        grid=(h1,),
        in_specs=[pl.BlockSpec((1, 2 * n2, 2 * n2), lambda k: (k, 0, 0)),
                  pl.BlockSpec((1, 2, 1, n2, ch), lambda k: (0, 0, k, 0, 0))],
        out_specs=pl.BlockSpec((1, 2, n2, ch), lambda k: (k, 0, 0, 0)),
        compiler_params=_cparams(("arbitrary",)),
        name="fft_filter_inner",
    )(mf, a5)


def _fft_inner_kernel(mf_ref, mi_ref, kf_ref, a_ref, o_ref):
    n2, ch = a_ref.shape[3], a_ref.shape[4]
    for r in range(a_ref.shape[2]):
        a = a_ref[0, :, r].reshape(2 * n2, ch)
        xs = _dotbf(mf_ref[r], a.astype(BF16))
        xr, xi = xs[:n2], xs[n2:]
        kr, ki = kf_ref[r, 0], kf_ref[r, 1]
        p = jnp.concatenate([xr * kr - xi * ki, xr * ki + xi * kr], axis=0)
        o_ref[0, :, r] = _dotbf(mi_ref[r], p.astype(BF16)).reshape(2, n2, ch)


def _fft_inner(mf, mi, kf, a5):
    nb, _, h1, n2, ch = a5.shape
    kr = 2
    return pl.pallas_call(
        _fft_inner_kernel,
        out_shape=jax.ShapeDtypeStruct(a5.shape, F32),
        grid=(h1 // kr, nb),
        in_specs=[pl.BlockSpec((kr, 2 * n2, 2 * n2), lambda k, b: (k, 0, 0)),
                  pl.BlockSpec((kr, 2 * n2, 2 * n2), lambda k, b: (k, 0, 0)),
                  pl.BlockSpec((kr, 2, n2, ch), lambda k, b: (k, 0, 0, 0)),
                  pl.BlockSpec((1, 2, kr, n2, ch), lambda k, b: (b, 0, k, 0, 0))],
        out_specs=pl.BlockSpec((1, 2, kr, n2, ch), lambda k, b: (b, 0, k, 0, 0)),
        compiler_params=_cparams(("arbitrary", "arbitrary")),
        name="fft_inner",
    )(mf, mi, kf, a5)


def _fft_final_kernel(g_ref, b_ref, z_ref, x2_ref, skip_ref, o_ref, *, n2):
    g = g_ref[...]
    h1, m = g.shape
    skip = skip_ref[...]

    def body(j, c):
        y = _dotbf(g, b_ref[pl.ds(j, m, stride=n2), :].astype(BF16))
        zj = z_ref[pl.ds(j, h1, stride=n2), :]
        xj = x2_ref[pl.ds(j, h1, stride=n2), :]
        o_ref[pl.ds(j, h1, stride=n2), :] = xj * (y + zj * skip)
        return c

    lax.fori_loop(0, n2, body, 0, unroll=8)


def _fft_final(gc, bm, z, x2, skip, n2):
    nb, seq, ch = z.shape
    h1, m = gc.shape
    blk = lambda rows: pl.BlockSpec((None, rows, LANES), lambda b, c: (b, 0, c))
    return pl.pallas_call(
        functools.partial(_fft_final_kernel, n2=n2),
        out_shape=jax.ShapeDtypeStruct((nb, seq, ch), F32),
        grid=(nb, ch // LANES),
        in_specs=[pl.BlockSpec((h1, m), lambda b, c: (0, 0)), blk(m * n2), blk(seq), blk(seq),
                  pl.BlockSpec((1, LANES), lambda b, c: (0, c))],
        out_specs=blk(seq),
        compiler_params=_cparams(("arbitrary", "arbitrary")),
        name="fft_final",
    )(gc, bm, z, x2, skip)


def _dft_tables(seq):
    n = 2 * seq
    n2 = FFT_N2
    n1 = n // n2
    h1 = n1 // 2
    i32 = jnp.int32
    k1 = jnp.arange(h1, dtype=i32)[:, None]
    a1 = jnp.arange(n1, dtype=i32)[None, :]
    th = (math.pi / n1) * ((a1 * (2 * k1 + 1)) % (2 * n1)).astype(F32)
    g_re, g_im = jnp.cos(th), -jnp.sin(th)
    g_full = jnp.concatenate([g_re, g_im], axis=0)
    g_half = g_full[:, :h1]
    g_out = (2.0 / n) * jnp.concatenate([g_re[:, :h1].T, g_im[:, :h1].T], axis=1)
    k2 = jnp.arange(n2, dtype=i32)[None, :, None]
    b2 = jnp.arange(n2, dtype=i32)[None, None, :]
    kk = jnp.arange(h1, dtype=i32)[:, None, None]
    ph = (math.pi / n) * ((b2 * (2 * kk + 1 + 2 * n1 * k2)) % (2 * n)).astype(F32)
    m_re, m_im = jnp.cos(ph), -jnp.sin(ph)
    mf = jnp.concatenate([jnp.concatenate([m_re, -m_im], axis=2),
                          jnp.concatenate([m_im, m_re], axis=2)], axis=1)
    mt_re, mt_im = jnp.transpose(m_re, (0, 2, 1)), jnp.transpose(m_im, (0, 2, 1))
    mi = jnp.concatenate([jnp.concatenate([mt_re, mt_im], axis=2),
                          jnp.concatenate([-mt_im, mt_re], axis=2)], axis=1)
    b16 = lambda a: a.astype(BF16)
    return dict(n1=n1, h1=h1, g_full=b16(g_full), g_half=b16(g_half), g_out=b16(g_out), mf=b16(mf), mi=b16(mi))


def _merge_kernel(attn_ref, hy_ref, g_ref, x_ref, gate_ref, shift_ref, scale_ref, gain_ref,
                  wa_ref, wh_ref, wo_ref, wrh_ref, wrl_ref, br_ref,
                  x1_ref, h2_ref, route_ref, cnt_ref):
    g = g_ref[...].astype(F32)
    merged = (g[:, :D_MODEL] * _dotbf(attn_ref[...], wa_ref[...])
              + g[:, D_MODEL:] * _dotbf(hy_ref[...].astype(BF16), wh_ref[...]))
    mix = _dotbf(merged.astype(BF16), wo_ref[...])
    x1 = x_ref[...] + gate_ref[0] * mix
    x1_ref[...] = x1

    ms = jnp.mean(x1 * x1, axis=-1, keepdims=True)
    h2 = x1 * lax.rsqrt(ms + RMS_EPS) * gain_ref[...]
    h2 = h2 * (1.0 + scale_ref[0]) + shift_ref[0]
    _rows_to_tiles(h2_ref, h2)

    hi = h2.astype(BF16)
    lo = (h2 - hi.astype(F32)).astype(BF16)
    wrh = wrh_ref[...]
    lg = _dotbf(hi, wrh) + _dotbf(lo, wrh) + _dotbf(hi, wrl_ref[...]) + br_ref[...]

    lane = lax.broadcasted_iota(jnp.int32, lg.shape, 1).astype(F32)
    is_grp = jnp.logical_and(lane >= N_EXPERTS, lane < N_EXPERTS + N_GROUPS)
    gm = jnp.where(is_grp, lg, NEG_BIG)
    gmax = jnp.max(gm, axis=-1, keepdims=True)
    gidx = jnp.min(jnp.where(gm == gmax, lane, 1e9), axis=-1, keepdims=True) - N_EXPERTS
    p_group = 1.0 / jnp.sum(jnp.where(is_grp, jnp.exp(gm - gmax), 0.0), axis=-1, keepdims=True)
    lo_lane = gidx * EXPERTS_PER_GROUP
    in_grp = jnp.logical_and(lane >= lo_lane, lane < lo_lane + EXPERTS_PER_GROUP)
    e1v = jnp.where(in_grp, lg, NEG_BIG)
    t1 = jnp.max(e1v, axis=-1, keepdims=True)
    i1 = jnp.min(jnp.where(e1v == t1, lane, 1e9), axis=-1, keepdims=True)
    e2v = jnp.where(lane == i1, NEG_BIG, e1v)
    t2 = jnp.max(e2v, axis=-1, keepdims=True)
    i2 = jnp.min(jnp.where(e2v == t2, lane, 1e9), axis=-1, keepdims=True)
    d = jnp.exp(t2 - t1)
    w1 = p_group / (1.0 + d)
    w2 = p_group * d / (1.0 + d)
    route_ref[...] = jnp.where(lane == 0, i1, jnp.where(lane == 1, i2,
                               jnp.where(lane == 2, w1, jnp.where(lane == 3, w2, 0.0))))

    onehot = (lane == i1).astype(F32) + (lane == i2).astype(F32)

    @pl.when(pl.program_id(0) == 0)
    def _():
        cnt_ref[...] = jnp.zeros(cnt_ref.shape, F32)

    cnt_ref[...] += jnp.sum(onehot, axis=0, keepdims=True)


def _merge(attn, hy, gates, x, gate1, shift2, scale2, gain, wa, wh, wo, wrh, wrl, br, nb, seq):
    t = nb * seq
    tm = min(512, seq)
    tps = seq // tm
    row = lambda i: (i, 0)
    per_b = lambda i: (i // tps, 0, 0)
    c2 = lambda i: (0, 0)
    return pl.pallas_call(
        _merge_kernel,
        out_shape=(jax.ShapeDtypeStruct((t, D_MODEL), F32),
                   jax.ShapeDtypeStruct((t * ROW_TILES, LANES), F32),
                   jax.ShapeDtypeStruct((t, LANES), F32),
                   jax.ShapeDtypeStruct((1, LANES), F32)),
        grid=(t // tm,),
        in_specs=[pl.BlockSpec((tm, ATTN_WIDTH), row),
                  pl.BlockSpec((tm, D_HYENA), row),
                  pl.BlockSpec((tm, 2 * D_MODEL), row),
                  pl.BlockSpec((tm, D_MODEL), row),
                  pl.BlockSpec((1, 1, D_MODEL), per_b),
                  pl.BlockSpec((1, 1, D_MODEL), per_b),
                  pl.BlockSpec((1, 1, D_MODEL), per_b),
                  pl.BlockSpec((1, D_MODEL), c2),
                  pl.BlockSpec(wa.shape, c2), pl.BlockSpec(wh.shape, c2), pl.BlockSpec(wo.shape, c2),
                  pl.BlockSpec(wrh.shape, c2), pl.BlockSpec(wrl.shape, c2), pl.BlockSpec(br.shape, c2)],
        out_specs=(pl.BlockSpec((tm, D_MODEL), row),
                   pl.BlockSpec((tm * ROW_TILES, LANES), row),
                   pl.BlockSpec((tm, LANES), row),
                   pl.BlockSpec((1, LANES), c2)),
        compiler_params=_cparams(("arbitrary",)),
        name="merge_router",
    )(attn, hy, gates, x, gate1, shift2, scale2, gain, wa, wh, wo, wrh, wrl, br)


def _rank_kernel(route_ref, pstart_ref, tri_ref, dest_ref, carry_ref):
    @pl.when(pl.program_id(0) == 0)
    def _():
        carry_ref[...] = jnp.zeros(carry_ref.shape, F32)

    r = route_ref[...]
    lane = lax.broadcasted_iota(jnp.int32, r.shape, 1).astype(F32)
    oh1 = (lane == r[:, 0:1]).astype(F32)
    oh2 = (lane == r[:, 1:2]).astype(F32)
    tri = tri_ref[...]
    before1 = _dotbf(tri, oh1.astype(BF16))
    before2 = _dotbf(tri, oh2.astype(BF16))
    base1 = pstart_ref[...] + carry_ref[...]
    d1 = jnp.sum(oh1 * (base1 + before1), axis=-1, keepdims=True)
    base2 = base1 + jnp.sum(oh1, axis=0, keepdims=True)
    d2 = jnp.sum(oh2 * (base2 + before2), axis=-1, keepdims=True)
    carry_ref[...] = base2 + jnp.sum(oh2, axis=0, keepdims=True) - pstart_ref[...]
    dest_ref[...] = jnp.where(lane == 0, d1, jnp.where(lane == 1, d2, 0.0))


def _rank(route, pstart, tri):
    t = route.shape[0]
    tm = tri.shape[0]
    return pl.pallas_call(
        _rank_kernel,
        out_shape=jax.ShapeDtypeStruct((t, LANES), F32),
        grid=(t // tm,),
        in_specs=[pl.BlockSpec((tm, LANES), lambda i: (i, 0)),
                  pl.BlockSpec((1, LANES), lambda i: (0, 0)),
                  pl.BlockSpec((tm, tm), lambda i: (0, 0))],
        out_specs=pl.BlockSpec((tm, LANES), lambda i: (i, 0)),
        scratch_shapes=[pltpu.VMEM((1, LANES), F32)],
        compiler_params=_cparams(("arbitrary",)),
        name="moe_rank",
    )(route, pstart, tri)


def _row_tile(ref, r):
    return ref.at[pl.ds(pl.multiple_of(r * ROW_TILES, ROW_TILES), ROW_TILES)]


def _rows_from_tiles(ref, n):
    return jnp.concatenate([ref[pl.ds(s, n, stride=ROW_TILES), :] for s in range(ROW_TILES)], axis=1)


def _rows_to_tiles(ref, x):
    n = x.shape[0]
    for s in range(ROW_TILES):
        ref[pl.ds(s, n, stride=ROW_TILES), :] = x[:, s * LANES:(s + 1) * LANES]


def _dispatch_kernel(d1_ref, d2_ref, h_ref, init_ref, xs_ref, sem):
    del init_ref
    tm = h_ref.shape[0] // ROW_TILES
    base = pl.program_id(0) * tm

    def copies(r):
        src = _row_tile(h_ref, r)
        return (pltpu.make_async_copy(src, _row_tile(xs_ref, d1_ref[base + r]), sem),
                pltpu.make_async_copy(src, _row_tile(xs_ref, d2_ref[base + r]), sem))

    def issue(r, c):
        a, b = copies(r)
        a.start(priority=0)
        b.start(priority=1)
        return c

    def drain(r, c):
        a, b = copies(r)
        a.wait()
        b.wait()
        return c

    lax.fori_loop(0, tm, issue, 0, unroll=8)
    lax.fori_loop(0, tm, drain, 0, unroll=8)


def _dispatch(d1, d2, h2, cap):
    t = h2.shape[0] // ROW_TILES
    tm = min(256, t)
    init = jnp.zeros((cap * ROW_TILES, LANES), h2.dtype)
    return pl.pallas_call(
        _dispatch_kernel,
        out_shape=jax.ShapeDtypeStruct((cap * ROW_TILES, LANES), h2.dtype),
        grid_spec=pltpu.PrefetchScalarGridSpec(
            num_scalar_prefetch=2,
            grid=(t // tm,),
            in_specs=[pl.BlockSpec((tm * ROW_TILES, LANES), lambda i, a, b: (i, 0)),
                      pl.BlockSpec(memory_space=pl.ANY)],
            out_specs=pl.BlockSpec(memory_space=pl.ANY),
            scratch_shapes=[pltpu.SemaphoreType.DMA(())]),
        input_output_aliases={3: 0},
        compiler_params=_cparams(("arbitrary",)),
        name="moe_dispatch",
    )(d1, d2, h2, init)


def _expert_kernel(be_ref, xs_ref, wgu_ref, wd_ref, ys_ref):
    del be_ref
    x = _rows_from_tiles(xs_ref, EXPERT_BLOCK).astype(BF16)
    gu = _dotbf(x, wgu_ref[0])
    g = gu[:, :D_EXPERT]
    u = gu[:, D_EXPERT:]
    a = (g * jax.nn.sigmoid(g) * u).astype(BF16)
    _rows_to_tiles(ys_ref, _dotbf(a, wd_ref[0]))


def _experts(blk_expert, xs, wgu, wd):
    cap = xs.shape[0] // ROW_TILES
    d = wgu.shape[1]
    blk = pl.BlockSpec((EXPERT_BLOCK * ROW_TILES, LANES), lambda i, be: (i, 0))
    return pl.pallas_call(
        _expert_kernel,
        out_shape=jax.ShapeDtypeStruct(xs.shape, F32),
        grid_spec=pltpu.PrefetchScalarGridSpec(
            num_scalar_prefetch=1,
            grid=(cap // EXPERT_BLOCK,),
            in_specs=[blk,
                      pl.BlockSpec((1, d, 2 * D_EXPERT), lambda i, be: (be[i], 0, 0)),
                      pl.BlockSpec((1, D_EXPERT, d), lambda i, be: (be[i], 0, 0))],
            out_specs=blk),
        compiler_params=_cparams(("arbitrary",)),
        name="moe_experts",
    )(blk_expert, xs, wgu, wd)


def _combine_kernel(d1_ref, d2_ref, ys_ref, x_ref, route_ref, gate_ref, o_ref, y1_ref, y2_ref, sem):
    tm = x_ref.shape[0]
    base = pl.program_id(0) * tm

    def copies(r):
        return (pltpu.make_async_copy(_row_tile(ys_ref, d1_ref[base + r]), _row_tile(y1_ref, r), sem),
                pltpu.make_async_copy(_row_tile(ys_ref, d2_ref[base + r]), _row_tile(y2_ref, r), sem))

    def issue(r, c):
        a, b = copies(r)
        a.start(priority=0)
        b.start(priority=1)
        return c

    def drain(r, c):
        a, b = copies(r)
        a.wait()
        b.wait()
        return c

    lax.fori_loop(0, tm, issue, 0, unroll=8)
    lax.fori_loop(0, tm, drain, 0, unroll=8)
    r = route_ref[...]
    ffn = _rows_from_tiles(y1_ref, tm) * r[:, 2:3] + _rows_from_tiles(y2_ref, tm) * r[:, 3:4]
    o_ref[...] = x_ref[...] + gate_ref[0] * ffn


def _combine(d1, d2, ys, x1, route, gate2, nb, seq):
    t, d = x1.shape
    tm = min(256, seq)
    tps = seq // tm
    return pl.pallas_call(
        _combine_kernel,
        out_shape=jax.ShapeDtypeStruct((t, d), F32),
        grid_spec=pltpu.PrefetchScalarGridSpec(
            num_scalar_prefetch=2,
            grid=(t // tm,),
            in_specs=[pl.BlockSpec(memory_space=pl.ANY),
                      pl.BlockSpec((tm, d), lambda i, a, b: (i, 0)),
                      pl.BlockSpec((tm, LANES), lambda i, a, b: (i, 0)),
                      pl.BlockSpec((1, 1, d), lambda i, a, b: (i // tps, 0, 0))],
            out_specs=pl.BlockSpec((tm, d), lambda i, a, b: (i, 0)),
            scratch_shapes=[pltpu.VMEM((tm * ROW_TILES, LANES), F32), pltpu.VMEM((tm * ROW_TILES, LANES), F32),
                            pltpu.SemaphoreType.DMA(())]),
        compiler_params=_cparams(("arbitrary",)),
        name="moe_combine",
    )(d1, d2, ys, x1, route, gate2)


def _rope_tables(seq):
    rows = seq // GRID_W
    row = jnp.broadcast_to(jnp.arange(rows, dtype=F32)[:, None], (rows, GRID_W)).reshape(-1)
    col = jnp.broadcast_to(jnp.arange(GRID_W, dtype=F32)[None, :], (rows, GRID_W)).reshape(-1)
    inv_freq = ROPE_THETA ** (-jnp.arange(ROPE_PAIRS_PER_AXIS, dtype=F32) / ROPE_PAIRS_PER_AXIS)
    ang = jnp.concatenate([row[:, None] * inv_freq, col[:, None] * inv_freq], axis=-1)
    cos = jnp.repeat(jnp.cos(ang), 2, axis=-1)
    sin = jnp.repeat(jnp.sin(ang), 2, axis=-1)
    sign = jnp.tile(jnp.array([-1.0, 1.0], F32), HEAD_DIM // 2)
    return jnp.tile(cos, (1, N_HEADS)), jnp.tile(sin * sign, (1, N_HEADS))


def _filter_features(seq):
    t = jnp.linspace(0.0, 1.0, seq, dtype=F32)[:, None]
    w = (2.0 * math.pi / seq) * jnp.arange(seq, dtype=F32)[:, None]
    bands = jnp.linspace(1e-4, FILTER_BANDS - 1, FILTER_BANDS, dtype=F32)[None, :]
    z = jnp.concatenate([t, jnp.cos(bands * w), -jnp.sin(bands * w)], axis=-1)
    pad = jnp.zeros((seq, FILTER_ORDER - FILTER_EMB), F32)
    fwd = jnp.concatenate([z, pad], axis=-1).at[:, FILTER_EMB].set(1.0)
    rev = jnp.concatenate([z[:1], z[:0:-1]], axis=0)
    sign = jnp.concatenate([jnp.zeros((1,), F32), -jnp.ones((seq - 1,), F32)])
    bwd = jnp.concatenate([rev, pad], axis=-1).at[:, FILTER_EMB].set(sign)
    deltas = jnp.abs(jnp.linspace(MIN_DECAY, MAX_DECAY, D_HYENA, dtype=F32))[None, :]
    return jnp.concatenate([fwd, bwd], axis=-1), deltas


def _run_trunk(x, mod, p):
    nb, seq, d = x.shape
    t = nb * seq
    depth = mod.shape[0]
    cos, sin = _rope_tables(seq)
    zfeat, deltas = _filter_features(seq)
    tabs = _dft_tables(seq)
    h1 = tabs["h1"]
    tm_rank = min(512, t)
    tri = jnp.tril(jnp.ones((tm_rank, tm_rank), BF16), -1)
    cap = t * TOP_K + N_EXPERTS * EXPERT_BLOCK
    n_blk = cap // EXPERT_BLOCK
    lane_e = jnp.arange(LANES)

    xf = x.reshape(t, d)
    for l in range(depth):
        m6 = mod[l].reshape(nb, 6, 1, d)
        shift1, scale1, gate1, shift2, scale2, gate2 = (m6[:, j] for j in range(6))

        q, k, vt, u, gates = _inproj(xf, shift1, scale1, p["norm_mix"][l], p["w_in"][l], p["w_vt"][l],
                                     p["qg"][l], p["kg"][l], p["bd"], cos, sin, nb, seq)
        bound = HEAD_DIM * jnp.max(jnp.abs(p["qg"][l])) * jnp.max(jnp.abs(p["kg"][l]))
        attn = lax.cond(bound <= SCORE_BOUND,
                        functools.partial(_attention, online=False),
                        functools.partial(_attention, online=True), q, k, vt).reshape(t, ATTN_WIDTH)

        z, x2 = _short_conv(u.reshape(nb, seq, 3 * D_HYENA), p["conv_w"][l], p["conv_b"][l])
        kern = _filters(zfeat, p["filt_w1"][l], p["filt_b1"][l], p["filt_w2"][l], p["filt_b2"][l],
                        p["filt_w3"][l], p["filt_b3"][l], p["filt_freq"][l], p["filt_w4"][l], deltas)
        ka = _fft_outer(tabs["g_full"], kern.reshape(1, 2 * seq, D_HYENA), FFT_N2)
        kf = _fft_filter_inner(tabs["mf"], ka.reshape(1, 2, h1, FFT_N2, D_HYENA))
        za = _fft_outer(tabs["g_half"], z, FFT_N2)
        zb = _fft_inner(tabs["mf"], tabs["mi"], kf, za.reshape(nb, 2, h1, FFT_N2, D_HYENA))
        hy = _fft_final(tabs["g_out"], zb.reshape(nb, 2 * h1 * FFT_N2, D_HYENA), z, x2,
                        p["skip"][l], FFT_N2).reshape(t, D_HYENA)

        x1, h2, route, counts = _merge(attn, hy, gates, xf, gate1, shift2, scale2, p["norm_ffn"][l],
                                       p["w_br_attn"][l], p["w_br_hyena"][l], p["w_out"][l],
                                       p["wr_hi"][l], p["wr_lo"][l], p["b_route"][l], nb, seq)

        cnt = counts[0].astype(jnp.int32)
        padded = jnp.where(lane_e < N_EXPERTS, (cnt + EXPERT_BLOCK - 1) // EXPERT_BLOCK * EXPERT_BLOCK, 0)
        pad_end = jnp.cumsum(padded)
        pstart = (pad_end - padded).astype(F32)[None, :]
        blk_start = jnp.arange(n_blk, dtype=jnp.int32) * EXPERT_BLOCK
        blk_expert = jnp.minimum(
            jnp.sum(blk_start[:, None] >= pad_end[None, :N_EXPERTS], axis=1), N_EXPERTS - 1).astype(jnp.int32)

        dest = _rank(route, pstart, tri)
        d1 = dest[:, 0].astype(jnp.int32)
        d2 = dest[:, 1].astype(jnp.int32)
        xs = _dispatch(d1, d2, h2, cap)
        ys = _experts(blk_expert, xs, p["w_gu"][l], p["w_down"][l])
        xf = _combine(d1, d2, ys, x1, route, gate2, nb, seq)
    return xf.reshape(nb, seq, d)


def kernel(x_prompt, x_sample, c_prompt, c_sample, w_ada, b_ada, norm_mix, norm_ffn, w_in, q_gain, k_gain, conv_w, conv_b, filt_w1, filt_b1, filt_w2, filt_b2, filt_w3, filt_b3, filt_freq, filt_w4, hyena_skip, w_br_attn, w_br_hyena, w_out, w_group, b_group, w_router, b_router, w_e_gate, w_e_up, w_e_down):
    depth = w_ada.shape[0]
    bp, bs = c_prompt.shape[0], c_sample.shape[0]
    rows = -(-(bp + bs) // SUBLANES) * SUBLANES
    c_pad = jnp.zeros((rows, D_MODEL), F32).at[:bp].set(c_prompt).at[bp:bp + bs].set(c_sample)
    mod = _ada(c_pad, w_ada, b_ada)

    scale = HEAD_DIM ** -0.5 * math.log2(math.e)
    head_id = np.arange(ATTN_WIDTH) // HEAD_DIM
    route_w = jnp.concatenate([w_router, w_group], axis=-1)
    route_w = jnp.pad(route_w, ((0, 0), (0, 0), (0, LANES - route_w.shape[-1])))
    wr_hi = route_w.astype(BF16)

    def block_diag(a, b):
        return jnp.concatenate([jnp.pad(a, ((0, 0), (0, 0), (0, b.shape[2]))),
                                jnp.pad(b, ((0, 0), (0, 0), (a.shape[2], 0)))], axis=1)

    twice = lambda w: block_diag(w, w)
    p = dict(
        norm_mix=norm_mix.reshape(depth, 1, D_MODEL),
        norm_ffn=norm_ffn.reshape(depth, 1, D_MODEL),
        w_in=w_in.astype(BF16),
        w_vt=jnp.swapaxes(w_in[:, :, COL_V:COL_U], 1, 2).astype(BF16),
        qg=(jnp.tile(q_gain, (1, N_HEADS)) * scale).reshape(depth, 1, ATTN_WIDTH),
        kg=jnp.tile(k_gain, (1, N_KV_HEADS)).reshape(depth, 1, KV_WIDTH),
        bd=jnp.asarray(head_id[:, None] == head_id[None, :], dtype=BF16),
        conv_w=conv_w, conv_b=conv_b,
        filt_w1=twice(jnp.pad(filt_w1, ((0, 0), (0, FILTER_ORDER - FILTER_EMB), (0, 0)))),
        filt_b1=jnp.tile(filt_b1, (1, 2)).reshape(depth, 1, 2 * FILTER_ORDER),
        filt_w2=twice(filt_w2), filt_b2=jnp.tile(filt_b2, (1, 2)).reshape(depth, 1, 2 * FILTER_ORDER),
        filt_w3=twice(filt_w3), filt_b3=jnp.tile(filt_b3, (1, 2)).reshape(depth, 1, 2 * FILTER_ORDER),
        filt_freq=jnp.tile(filt_freq, (1, 2)).reshape(depth, 1, 2 * FILTER_ORDER),
        filt_w4=block_diag(filt_w4[:, :, :D_HYENA], filt_w4[:, :, D_HYENA:]),
        skip=hyena_skip.reshape(depth, 1, D_HYENA),
        w_br_attn=w_br_attn.astype(BF16), w_br_hyena=w_br_hyena.astype(BF16), w_out=w_out.astype(BF16),
        wr_hi=wr_hi, wr_lo=(route_w - wr_hi.astype(F32)).astype(BF16),
        b_route=jnp.pad(jnp.concatenate([b_router, b_group], axis=-1),
                        ((0, 0), (0, LANES - N_EXPERTS - N_GROUPS))).reshape(depth, 1, LANES),
        w_gu=jnp.concatenate([w_e_gate, w_e_up], axis=-1).astype(BF16),
        w_down=w_e_down.astype(BF16),
    )
    y_prompt = _run_trunk(x_prompt, mod[:, :bp], p)
    y_sample = _run_trunk(x_sample, mod[:, bp:bp + bs], p)
    return (y_prompt, y_sample)
```

```python
import functools
import math

import numpy as np
import jax
import jax.numpy as jnp
from jax import lax
from jax.experimental import pallas as pl
from jax.experimental.pallas import tpu as pltpu

F32 = jnp.float32
BF16 = jnp.bfloat16

D_MODEL = 1024
GRID_W = 64
N_HEADS = 8
N_KV_HEADS = 4
HEAD_DIM = 64
Q_PER_KV = N_HEADS // N_KV_HEADS
ATTN_WIDTH = N_HEADS * HEAD_DIM
KV_WIDTH = N_KV_HEADS * HEAD_DIM
ROPE_THETA = 10000.0
ROPE_PAIRS_PER_AXIS = HEAD_DIM // 4
D_HYENA = D_MODEL // 2
FILTER_EMB = 33
FILTER_BANDS = (FILTER_EMB - 1) // 2
FILTER_ORDER = 64
DECAY_TARGET = 1e-2
MAX_DECAY = math.log(DECAY_TARGET) / 0.3
MIN_DECAY = math.log(DECAY_TARGET) / 1.5
MOD_SHIFT = 0.05
N_GROUPS = 4
EXPERTS_PER_GROUP = 8
N_EXPERTS = N_GROUPS * EXPERTS_PER_GROUP
TOP_K = 2
D_EXPERT = D_MODEL // 4
RMS_EPS = 1e-6
IN_COLS = ATTN_WIDTH + 2 * KV_WIDTH + 3 * D_HYENA + 2 * D_MODEL
COL_K = ATTN_WIDTH
COL_V = ATTN_WIDTH + KV_WIDTH
COL_U = ATTN_WIDTH + 2 * KV_WIDTH
COL_G = COL_U + 3 * D_HYENA

LANES = 128
SUBLANES = 8
ROW_TILES = D_MODEL // LANES
VMEM_LIMIT = 56 * 1024 * 1024

V_ROWS = HEAD_DIM + 16
SCORE_BOUND = 60.0
FFT_N2 = 128
EXPERT_BLOCK = 512
NEG_BIG = -1e30

_HI = lax.Precision.HIGHEST


def _dot32(a, b):
    return jnp.dot(a, b, precision=_HI, preferred_element_type=F32)


def _dotbf(a, b):
    return jnp.dot(a, b, preferred_element_type=F32)


def _cparams(sem):
    return pltpu.CompilerParams(dimension_semantics=sem, vmem_limit_bytes=VMEM_LIMIT)


def _ada_kernel(c_ref, w_ref, b_ref, o_ref):
    c = c_ref[...]
    act = c * jax.nn.sigmoid(c)
    o_ref[0] = _dot32(act, w_ref[0]) + b_ref[0]


def _ada(c_pad, w_ada, b_ada):
    depth, d, n = w_ada.shape
    tn = 1536
    return pl.pallas_call(
        _ada_kernel,
        out_shape=jax.ShapeDtypeStruct((depth, c_pad.shape[0], n), F32),
        grid=(depth, n // tn),
        in_specs=[pl.BlockSpec(c_pad.shape, lambda l, j: (0, 0)),
                  pl.BlockSpec((1, d, tn), lambda l, j: (l, 0, j)),
                  pl.BlockSpec((1, 1, tn), lambda l, j: (l, 0, j))],
        out_specs=pl.BlockSpec((1, c_pad.shape[0], tn), lambda l, j: (l, 0, j)),
        compiler_params=_cparams(("arbitrary", "arbitrary")),
        name="ada_mod",
    )(c_pad, w_ada, b_ada.reshape(depth, 1, n))


def _swap_pairs(x):
    n = x.shape[-1]
    lane = lax.broadcasted_iota(jnp.int32, x.shape, 1)
    nxt = pltpu.roll(x, n - 1, 1)
    prv = pltpu.roll(x, 1, 1)
    return jnp.where(lane % 2 == 0, nxt, prv)


def _head_norm_rope(p, gain, bd, cos, sin_signed):
    sq = (p * p).astype(BF16)
    ms = _dotbf(sq, bd) * (1.0 / HEAD_DIM)
    pn = p * lax.rsqrt(ms + RMS_EPS) * gain
    return pn * cos + _swap_pairs(pn) * sin_signed


def _inproj_kernel(x_ref, xp_ref, xn_ref, shift_ref, scale_ref, gain_ref, w_ref, wvt_ref, qg_ref, kg_ref,
                   bd_ref, cos_ref, sin_ref, cw_ref, cb_ref, q_ref, k_ref, vt_ref, z_ref, x2_ref, g_ref, *, tps):
    def normed(x):
        ms = jnp.mean(x * x, axis=-1, keepdims=True)
        h = x * lax.rsqrt(ms + RMS_EPS) * gain_ref[...]
        return (h * (1.0 + scale_ref[0]) + shift_ref[0]).astype(BF16)

    hb = normed(x_ref[...])
    tm = hb.shape[0]
    cos = cos_ref[...]
    sin = sin_ref[...]
    bd = bd_ref[...]

    q = _dotbf(hb, w_ref[:, 0:COL_K])
    q = _head_norm_rope(q, qg_ref[...], bd, cos, sin)
    for hd in range(N_HEADS):
        q_ref[0, hd] = q[:, hd * HEAD_DIM:(hd + 1) * HEAD_DIM].astype(BF16)

    k = _dotbf(hb, w_ref[:, COL_K:COL_V])
    k = _head_norm_rope(k, kg_ref[...], bd[:KV_WIDTH, :KV_WIDTH], cos[:, :KV_WIDTH], sin[:, :KV_WIDTH])
    vt = lax.dot_general(wvt_ref[...], hb, (((1,), (1,)), ((), ())), preferred_element_type=F32)
    ones = jnp.ones((V_ROWS - HEAD_DIM, vt.shape[1]), BF16)
    for hd in range(N_KV_HEADS):
        k_ref[0, hd] = k[:, hd * HEAD_DIM:(hd + 1) * HEAD_DIM].astype(BF16)
        vt_ref[0, hd, 0:HEAD_DIM, :] = vt[hd * HEAD_DIM:(hd + 1) * HEAD_DIM].astype(BF16)
        vt_ref[0, hd, HEAD_DIM:V_ROWS, :] = ones

    g_ref[...] = jax.nn.sigmoid(_dotbf(hb, w_ref[:, COL_G:IN_COLS])).astype(BF16)

    halo = normed(jnp.concatenate([xp_ref[...], xn_ref[...]], axis=0))
    u_all = _dotbf(jnp.concatenate([hb, halo], axis=0), w_ref[:, COL_U:COL_G])
    u = u_all[:tm]
    pos = pl.program_id(0) % tps
    prev_row = u_all[tm + SUBLANES - 1:tm + SUBLANES] * (pos > 0).astype(F32)
    next_row = u_all[tm + SUBLANES:tm + SUBLANES + 1] * (pos < tps - 1).astype(F32)
    ridx = lax.broadcasted_iota(jnp.int32, u.shape, 0)
    u_prev = jnp.where(ridx == 0, prev_row, pltpu.roll(u, 1, 0))
    u_next = jnp.where(ridx == tm - 1, next_row, pltpu.roll(u, tm - 1, 0))
    w = cw_ref[...]
    c = cb_ref[...] + u_prev * w[0:1] + u * w[1:2] + u_next * w[2:3]
    z_ref[...] = c[:, 2 * D_HYENA:3 * D_HYENA] * c[:, 0:D_HYENA]
    x2_ref[...] = c[:, D_HYENA:2 * D_HYENA]


def _inproj(x, shift, scale, gain, w_in, w_vt, qg, kg, bd, cos, sin, conv_w, conv_b, nb, seq):
    t = nb * seq
    tm = min(512, seq)
    tps = seq // tm
    rpb = tm // SUBLANES
    row = lambda i: (i, 0)
    per_b = lambda i: (i // tps, 0, 0)
    const2 = lambda i: (0, 0)
    pos = lambda i: (i % tps, 0)
    head_out = lambda i: (i // tps, 0, i % tps, 0)
    return pl.pallas_call(
        functools.partial(_inproj_kernel, tps=tps),
        out_shape=(jax.ShapeDtypeStruct((nb, N_HEADS, seq, HEAD_DIM), BF16),
                   jax.ShapeDtypeStruct((nb, N_KV_HEADS, seq, HEAD_DIM), BF16),
                   jax.ShapeDtypeStruct((nb, N_KV_HEADS, V_ROWS, seq), BF16),
                   jax.ShapeDtypeStruct((t, D_HYENA), F32),
                   jax.ShapeDtypeStruct((t, D_HYENA), F32),
                   jax.ShapeDtypeStruct((t, 2 * D_MODEL), BF16)),
        grid=(t // tm,),
        in_specs=[pl.BlockSpec((tm, D_MODEL), row),
                  pl.BlockSpec((SUBLANES, D_MODEL), lambda i: (jnp.maximum(i * rpb - 1, 0), 0)),
                  pl.BlockSpec((SUBLANES, D_MODEL), lambda i: (jnp.minimum((i + 1) * rpb, t // SUBLANES - 1), 0)),
                  pl.BlockSpec((1, 1, D_MODEL), per_b),
                  pl.BlockSpec((1, 1, D_MODEL), per_b),
                  pl.BlockSpec((1, D_MODEL), const2),
                  pl.BlockSpec((D_MODEL, IN_COLS), const2),
                  pl.BlockSpec((KV_WIDTH, D_MODEL), const2),
                  pl.BlockSpec((1, ATTN_WIDTH), const2),
                  pl.BlockSpec((1, KV_WIDTH), const2),
                  pl.BlockSpec((ATTN_WIDTH, ATTN_WIDTH), const2),
                  pl.BlockSpec((tm, ATTN_WIDTH), pos),
                  pl.BlockSpec((tm, ATTN_WIDTH), pos),
                  pl.BlockSpec((3, 3 * D_HYENA), const2),
                  pl.BlockSpec((1, 3 * D_HYENA), const2)],
        out_specs=(pl.BlockSpec((1, N_HEADS, tm, HEAD_DIM), head_out),
                   pl.BlockSpec((1, N_KV_HEADS, tm, HEAD_DIM), head_out),
                   pl.BlockSpec((1, N_KV_HEADS, V_ROWS, tm), lambda i: (i // tps, 0, 0, i % tps)),
                   pl.BlockSpec((tm, D_HYENA), row),
                   pl.BlockSpec((tm, D_HYENA), row),
                   pl.BlockSpec((tm, 2 * D_MODEL), row)),
        compiler_params=_cparams(("arbitrary",)),
        name="in_proj",
    )(x, x, x, shift, scale, gain, w_in, w_vt, qg, kg, bd, cos, sin, conv_w, conv_b)


def _attn_kernel(q_ref, k_ref, vt_ref, o_ref, m_ref, acc_ref, *, tkc, online):
    seq = k_ref.shape[2]
    nchunk = seq // tkc
    acc_ref[...] = jnp.zeros(acc_ref.shape, F32)
    if online:
        m_ref[...] = jnp.full(m_ref.shape, NEG_BIG, F32)

    def body(c, carry):
        off = pl.multiple_of(c * tkc, tkc)
        kc = k_ref[0, 0, pl.ds(off, tkc), :]
        vc = vt_ref[0, 0, :, pl.ds(off, tkc)]
        for h in range(Q_PER_KV):
            st = lax.dot_general(kc, q_ref[0, h], (((1,), (1,)), ((), ())), preferred_element_type=F32)
            if online:
                m_prev = m_ref[h]
                m_new = jnp.maximum(m_prev, jnp.max(st, axis=0, keepdims=True))
                p = jnp.exp2(st - m_new).astype(BF16)
                acc_ref[h] = jnp.exp2(m_prev - m_new) * acc_ref[h] + _dotbf(vc, p)
                m_ref[h] = m_new
            else:
                acc_ref[h] += _dotbf(vc, jnp.exp2(st).astype(BF16))
        return carry

    lax.fori_loop(0, nchunk, body, 0, unroll=2 if nchunk % 2 == 0 else 1)
    outs = []
    for h in range(Q_PER_KV):
        a = acc_ref[h]
        outs.append((a[:HEAD_DIM] / a[HEAD_DIM:HEAD_DIM + 1]).T)
    o_ref[0] = jnp.concatenate(outs, axis=1).astype(BF16)


def _attention(q, k, vt, online):
    nb, _, seq, _ = q.shape
    tq = min(512 if online else 1024, seq)
    tkc = min(512 if online else 1024, seq)
    return pl.pallas_call(
        functools.partial(_attn_kernel, tkc=tkc, online=online),
        out_shape=jax.ShapeDtypeStruct((nb, seq, ATTN_WIDTH), BF16),
        grid=(nb, N_KV_HEADS, seq // tq),
        in_specs=[pl.BlockSpec((1, Q_PER_KV, tq, HEAD_DIM), lambda b, g, i: (b, g, i, 0)),
                  pl.BlockSpec((1, 1, seq, HEAD_DIM), lambda b, g, i: (b, g, 0, 0)),
                  pl.BlockSpec((1, 1, V_ROWS, seq), lambda b, g, i: (b, g, 0, 0))],
        out_specs=pl.BlockSpec((1, tq, Q_PER_KV * HEAD_DIM), lambda b, g, i: (b, i, g)),
        scratch_shapes=[pltpu.VMEM((Q_PER_KV, 1, tq), F32),
                        pltpu.VMEM((Q_PER_KV, V_ROWS, tq), F32)],
        compiler_params=_cparams(("arbitrary", "arbitrary", "arbitrary")),
        name="attn_online" if online else "attn_bounded",
    )(q, k, vt)


def _filter_kernel(z_ref, w1_ref, b1_ref, w2_ref, b2_ref, w3_ref, b3_ref, fr_ref, w4_ref,
                   dl_ref, o_ref):
    z = z_ref[...]
    fr = fr_ref[...]
    h = jnp.sin(fr * (_dot32(z, w1_ref[...]) + b1_ref[...]))
    h = jnp.sin(fr * (_dot32(h, w2_ref[...]) + b2_ref[...]))
    h = jnp.sin(fr * (_dot32(h, w3_ref[...]) + b3_ref[...]))
    h = _dot32(h, w4_ref[...])
    dl = dl_ref[...]
    for d in range(2):
        t = z[:, d * FILTER_ORDER:d * FILTER_ORDER + 1]
        sign = z[:, d * FILTER_ORDER + FILTER_EMB:d * FILTER_ORDER + FILTER_EMB + 1]
        o_ref[d] = h[:, d * D_HYENA:(d + 1) * D_HYENA] * (jnp.exp(-t * dl) + MOD_SHIFT) * sign


def _filters(zfeat, w1, b1, w2, b2, w3, b3, fr, w4, deltas):
    seq, fe = zfeat.shape
    tm = min(512, seq)
    c2 = lambda i: (0, 0)
    return pl.pallas_call(
        _filter_kernel,
        out_shape=jax.ShapeDtypeStruct((2, seq, D_HYENA), F32),
        grid=(seq // tm,),
        in_specs=[pl.BlockSpec((tm, fe), lambda i: (i, 0)),
                  pl.BlockSpec(w1.shape, c2), pl.BlockSpec(b1.shape, c2),
                  pl.BlockSpec(w2.shape, c2), pl.BlockSpec(b2.shape, c2),
                  pl.BlockSpec(w3.shape, c2), pl.BlockSpec(b3.shape, c2),
                  pl.BlockSpec(fr.shape, c2), pl.BlockSpec(w4.shape, c2),
                  pl.BlockSpec(deltas.shape, c2)],
        out_specs=pl.BlockSpec((2, tm, D_HYENA), lambda i: (0, i, 0)),
        compiler_params=_cparams(("arbitrary",)),
        name="hyena_filters",
    )(zfeat, w1, b1, w2, b2, w3, b3, fr, w4, deltas)


def _fft_outer_kernel(g_ref, x_ref, o_ref, *, n2):
    g = g_ref[...]
    m, kk = g.shape

    def body(jp, c):
        j = 2 * jp
        xj = jnp.concatenate([x_ref[pl.ds(j, kk, stride=n2), :], x_ref[pl.ds(j + 1, kk, stride=n2), :]], axis=1)
        r = _dotbf(g, xj.astype(BF16))
        o_ref[pl.ds(j, m, stride=n2), :] = r[:, :LANES]
        o_ref[pl.ds(j + 1, m, stride=n2), :] = r[:, LANES:]
        return c

    lax.fori_loop(0, n2 // 2, body, 0, unroll=4)


def _fft_outer(gmat, x, n2):
    nb, rows, ch = x.shape
    m, kk = gmat.shape
    return pl.pallas_call(
        functools.partial(_fft_outer_kernel, n2=n2),
        out_shape=jax.ShapeDtypeStruct((nb, m * n2, ch), F32),
        grid=(nb, ch // LANES),
        in_specs=[pl.BlockSpec((m, kk), lambda b, c: (0, 0)),
                  pl.BlockSpec((None, rows, LANES), lambda b, c: (b, 0, c))],
        out_specs=pl.BlockSpec((None, m * n2, LANES), lambda b, c: (b, 0, c)),
        compiler_params=_cparams(("arbitrary", "arbitrary")),
        name="fft_outer",
    )(gmat, x)


def _fft_filter_inner_kernel(mf_ref, a_ref, o_ref):
    n2, ch = a_ref.shape[3], a_ref.shape[4]
    a = a_ref[0, :, 0].reshape(2 * n2, ch)
    o_ref[0] = _dotbf(mf_ref[0], a.astype(BF16)).reshape(2, n2, ch)


def _fft_filter_inner(mf, a5):
    _, _, h1, n2, ch = a5.shape
    return pl.pallas_call(
        _fft_filter_inner_kernel,
        out_shape=jax.ShapeDtypeStruct((h1, 2, n2, ch), F32),
        grid=(h1,),
        in_specs=[pl.BlockSpec((1, 2 * n2, 2 * n2), lambda k: (k, 0, 0)),
                  pl.BlockSpec((1, 2, 1, n2, ch), lambda k: (0, 0, k, 0, 0))],
        out_specs=pl.BlockSpec((1, 2, n2, ch), lambda k: (k, 0, 0, 0)),
        compiler_params=_cparams(("arbitrary",)),
        name="fft_filter_inner",
    )(mf, a5)


def _fft_inner_kernel(mf_ref, mi_ref, kf_ref, a_ref, o_ref):
    n2, ch = a_ref.shape[3], a_ref.shape[4]
    for r in range(a_ref.shape[2]):
        a = a_ref[0, :, r].reshape(2 * n2, ch)
        xs = _dotbf(mf_ref[r], a.astype(BF16))
        xr, xi = xs[:n2], xs[n2:]
        kr, ki = kf_ref[r, 0], kf_ref[r, 1]
        p = jnp.concatenate([xr * kr - xi * ki, xr * ki + xi * kr], axis=0)
        o_ref[0, :, r] = _dotbf(mi_ref[r], p.astype(BF16)).reshape(2, n2, ch)


def _fft_inner(mf, mi, kf, a5):
    nb, _, h1, n2, ch = a5.shape
    kr = 2
    return pl.pallas_call(
        _fft_inner_kernel,
        out_shape=jax.ShapeDtypeStruct(a5.shape, F32),
        grid=(h1 // kr, nb),
        in_specs=[pl.BlockSpec((kr, 2 * n2, 2 * n2), lambda k, b: (k, 0, 0)),
                  pl.BlockSpec((kr, 2 * n2, 2 * n2), lambda k, b: (k, 0, 0)),
                  pl.BlockSpec((kr, 2, n2, ch), lambda k, b: (k, 0, 0, 0)),
                  pl.BlockSpec((1, 2, kr, n2, ch), lambda k, b: (b, 0, k, 0, 0))],
        out_specs=pl.BlockSpec((1, 2, kr, n2, ch), lambda k, b: (b, 0, k, 0, 0)),
        compiler_params=_cparams(("arbitrary", "arbitrary")),
        name="fft_inner",
    )(mf, mi, kf, a5)


def _fft_final_kernel(g_ref, b_ref, z_ref, x2_ref, skip_ref, o_ref, *, n2):
    g = g_ref[...]
    h1, m = g.shape
    skip = skip_ref[...]

    def body(jp, c):
        j = 2 * jp
        bj = jnp.concatenate([b_ref[pl.ds(j, m, stride=n2), :], b_ref[pl.ds(j + 1, m, stride=n2), :]], axis=1)
        y = _dotbf(g, bj.astype(BF16))
        for d in range(2):
            zj = z_ref[pl.ds(j + d, h1, stride=n2), :]
            xj = x2_ref[pl.ds(j + d, h1, stride=n2), :]
            o_ref[pl.ds(j + d, h1, stride=n2), :] = xj * (y[:, d * LANES:(d + 1) * LANES] + zj * skip)
        return c

    lax.fori_loop(0, n2 // 2, body, 0, unroll=4)


def _fft_final(gc, bm, z, x2, skip, n2):
    nb, seq, ch = z.shape
    h1, m = gc.shape
    blk = lambda rows: pl.BlockSpec((None, rows, LANES), lambda b, c: (b, 0, c))
    return pl.pallas_call(
        functools.partial(_fft_final_kernel, n2=n2),
        out_shape=jax.ShapeDtypeStruct((nb, seq, ch), F32),
        grid=(nb, ch // LANES),
        in_specs=[pl.BlockSpec((h1, m), lambda b, c: (0, 0)), blk(m * n2), blk(seq), blk(seq),
                  pl.BlockSpec((1, LANES), lambda b, c: (0, c))],
        out_specs=blk(seq),
        compiler_params=_cparams(("arbitrary", "arbitrary")),
        name="fft_final",
    )(gc, bm, z, x2, skip)


def _dft_tables(seq):
    n = 2 * seq
    n2 = FFT_N2
    n1 = n // n2
    h1 = n1 // 2
    i32 = jnp.int32
    k1 = jnp.arange(h1, dtype=i32)[:, None]
    a1 = jnp.arange(n1, dtype=i32)[None, :]
    th = (math.pi / n1) * ((a1 * (2 * k1 + 1)) % (2 * n1)).astype(F32)
    g_re, g_im = jnp.cos(th), -jnp.sin(th)
    g_full = jnp.concatenate([g_re, g_im], axis=0)
    g_half = g_full[:, :h1]
    g_out = (2.0 / n) * jnp.concatenate([g_re[:, :h1].T, g_im[:, :h1].T], axis=1)
    k2 = jnp.arange(n2, dtype=i32)[None, :, None]
    b2 = jnp.arange(n2, dtype=i32)[None, None, :]
    kk = jnp.arange(h1, dtype=i32)[:, None, None]
    ph = (math.pi / n) * ((b2 * (2 * kk + 1 + 2 * n1 * k2)) % (2 * n)).astype(F32)
    m_re, m_im = jnp.cos(ph), -jnp.sin(ph)
    mf = jnp.concatenate([jnp.concatenate([m_re, -m_im], axis=2),
                          jnp.concatenate([m_im, m_re], axis=2)], axis=1)
    mt_re, mt_im = jnp.transpose(m_re, (0, 2, 1)), jnp.transpose(m_im, (0, 2, 1))
    mi = jnp.concatenate([jnp.concatenate([mt_re, mt_im], axis=2),
                          jnp.concatenate([-mt_im, mt_re], axis=2)], axis=1)
    b16 = lambda a: a.astype(BF16)
    return dict(n1=n1, h1=h1, g_full=b16(g_full), g_half=b16(g_half), g_out=b16(g_out), mf=b16(mf), mi=b16(mi))


def _merge_kernel(attn_ref, hy_ref, g_ref, x_ref, gate_ref, shift_ref, scale_ref, gain_ref,
                  wa_ref, wh_ref, wo_ref, wrh_ref, wrl_ref, br_ref,
                  x1_ref, h2_ref, route_ref, cnt_ref):
    g = g_ref[...].astype(F32)
    merged = (g[:, :D_MODEL] * _dotbf(attn_ref[...], wa_ref[...])
              + g[:, D_MODEL:] * _dotbf(hy_ref[...].astype(BF16), wh_ref[...]))
    mix = _dotbf(merged.astype(BF16), wo_ref[...])
    x1 = x_ref[...] + gate_ref[0] * mix
    x1_ref[...] = x1

    ms = jnp.mean(x1 * x1, axis=-1, keepdims=True)
    h2 = x1 * lax.rsqrt(ms + RMS_EPS) * gain_ref[...]
    h2 = h2 * (1.0 + scale_ref[0]) + shift_ref[0]
    _rows_to_tiles(h2_ref, h2)

    hi = h2.astype(BF16)
    lo = (h2 - hi.astype(F32)).astype(BF16)
    wrh = wrh_ref[...]
    lg = _dotbf(hi, wrh) + _dotbf(lo, wrh) + _dotbf(hi, wrl_ref[...]) + br_ref[...]

    lane = lax.broadcasted_iota(jnp.int32, lg.shape, 1).astype(F32)
    is_grp = jnp.logical_and(lane >= N_EXPERTS, lane < N_EXPERTS + N_GROUPS)
    gm = jnp.where(is_grp, lg, NEG_BIG)
    gmax = jnp.max(gm, axis=-1, keepdims=True)
    gidx = jnp.min(jnp.where(gm == gmax, lane, 1e9), axis=-1, keepdims=True) - N_EXPERTS
    p_group = 1.0 / jnp.sum(jnp.where(is_grp, jnp.exp(gm - gmax), 0.0), axis=-1, keepdims=True)
    lo_lane = gidx * EXPERTS_PER_GROUP
    in_grp = jnp.logical_and(lane >= lo_lane, lane < lo_lane + EXPERTS_PER_GROUP)
    e1v = jnp.where(in_grp, lg, NEG_BIG)
    t1 = jnp.max(e1v, axis=-1, keepdims=True)
    i1 = jnp.min(jnp.where(e1v == t1, lane, 1e9), axis=-1, keepdims=True)
    e2v = jnp.where(lane == i1, NEG_BIG, e1v)
    t2 = jnp.max(e2v, axis=-1, keepdims=True)
    i2 = jnp.min(jnp.where(e2v == t2, lane, 1e9), axis=-1, keepdims=True)
    d = jnp.exp(t2 - t1)
    w1 = p_group / (1.0 + d)
    w2 = p_group * d / (1.0 + d)
    route_ref[...] = jnp.where(lane == 0, i1, jnp.where(lane == 1, i2,
                               jnp.where(lane == 2, w1, jnp.where(lane == 3, w2, 0.0))))

    onehot = (lane == i1).astype(F32) + (lane == i2).astype(F32)

    @pl.when(pl.program_id(0) == 0)
    def _():
        cnt_ref[...] = jnp.zeros(cnt_ref.shape, F32)

    cnt_ref[...] += jnp.sum(onehot, axis=0, keepdims=True)


def _merge(attn, hy, gates, x, gate1, shift2, scale2, gain, wa, wh, wo, wrh, wrl, br, nb, seq):
    t = nb * seq
    tm = min(512, seq)
    tps = seq // tm
    row = lambda i: (i, 0)
    per_b = lambda i: (i // tps, 0, 0)
    c2 = lambda i: (0, 0)
    return pl.pallas_call(
        _merge_kernel,
        out_shape=(jax.ShapeDtypeStruct((t, D_MODEL), F32),
                   jax.ShapeDtypeStruct((t * ROW_TILES, LANES), F32),
                   jax.ShapeDtypeStruct((t, LANES), F32),
                   jax.ShapeDtypeStruct((1, LANES), F32)),
        grid=(t // tm,),
        in_specs=[pl.BlockSpec((tm, ATTN_WIDTH), row),
                  pl.BlockSpec((tm, D_HYENA), row),
                  pl.BlockSpec((tm, 2 * D_MODEL), row),
                  pl.BlockSpec((tm, D_MODEL), row),
                  pl.BlockSpec((1, 1, D_MODEL), per_b),
                  pl.BlockSpec((1, 1, D_MODEL), per_b),
                  pl.BlockSpec((1, 1, D_MODEL), per_b),
                  pl.BlockSpec((1, D_MODEL), c2),
                  pl.BlockSpec(wa.shape, c2), pl.BlockSpec(wh.shape, c2), pl.BlockSpec(wo.shape, c2),
                  pl.BlockSpec(wrh.shape, c2), pl.BlockSpec(wrl.shape, c2), pl.BlockSpec(br.shape, c2)],
        out_specs=(pl.BlockSpec((tm, D_MODEL), row),
                   pl.BlockSpec((tm * ROW_TILES, LANES), row),
                   pl.BlockSpec((tm, LANES), row),
                   pl.BlockSpec((1, LANES), c2)),
        compiler_params=_cparams(("arbitrary",)),
        name="merge_router",
    )(attn, hy, gates, x, gate1, shift2, scale2, gain, wa, wh, wo, wrh, wrl, br)


def _rank_kernel(route_ref, pstart_ref, tri_ref, dest_ref, carry_ref):
    @pl.when(pl.program_id(0) == 0)
    def _():
        carry_ref[...] = jnp.zeros(carry_ref.shape, F32)

    r = route_ref[...]
    lane = lax.broadcasted_iota(jnp.int32, r.shape, 1).astype(F32)
    oh1 = (lane == r[:, 0:1]).astype(F32)
    oh2 = (lane == r[:, 1:2]).astype(F32)
    tri = tri_ref[...]
    before1 = _dotbf(tri, oh1.astype(BF16))
    before2 = _dotbf(tri, oh2.astype(BF16))
    base1 = pstart_ref[...] + carry_ref[...]
    d1 = jnp.sum(oh1 * (base1 + before1), axis=-1, keepdims=True)
    base2 = base1 + jnp.sum(oh1, axis=0, keepdims=True)
    d2 = jnp.sum(oh2 * (base2 + before2), axis=-1, keepdims=True)
    carry_ref[...] = base2 + jnp.sum(oh2, axis=0, keepdims=True) - pstart_ref[...]
    dest_ref[...] = jnp.where(lane == 0, d1, jnp.where(lane == 1, d2, 0.0))


def _rank(route, pstart, tri):
    t = route.shape[0]
    tm = tri.shape[0]
    return pl.pallas_call(
        _rank_kernel,
        out_shape=jax.ShapeDtypeStruct((t, LANES), F32),
        grid=(t // tm,),
        in_specs=[pl.BlockSpec((tm, LANES), lambda i: (i, 0)),
                  pl.BlockSpec((1, LANES), lambda i: (0, 0)),
                  pl.BlockSpec((tm, tm), lambda i: (0, 0))],
        out_specs=pl.BlockSpec((tm, LANES), lambda i: (i, 0)),
        scratch_shapes=[pltpu.VMEM((1, LANES), F32)],
        compiler_params=_cparams(("arbitrary",)),
        name="moe_rank",
    )(route, pstart, tri)


def _row_tile(ref, r):
    return ref.at[pl.ds(pl.multiple_of(r * ROW_TILES, ROW_TILES), ROW_TILES)]


def _rows_from_tiles(ref, n):
    return jnp.concatenate([ref[pl.ds(s, n, stride=ROW_TILES), :] for s in range(ROW_TILES)], axis=1)


def _rows_to_tiles(ref, x):
    n = x.shape[0]
    for s in range(ROW_TILES):
        ref[pl.ds(s, n, stride=ROW_TILES), :] = x[:, s * LANES:(s + 1) * LANES]


def _dispatch_kernel(d1_ref, d2_ref, h_ref, init_ref, xs_ref, sem):
    del init_ref
    tm = h_ref.shape[0] // ROW_TILES
    base = pl.program_id(0) * tm

    def copies(r):
        src = _row_tile(h_ref, r)
        return (pltpu.make_async_copy(src, _row_tile(xs_ref, d1_ref[base + r]), sem),
                pltpu.make_async_copy(src, _row_tile(xs_ref, d2_ref[base + r]), sem))

    def issue(r, c):
        a, b = copies(r)
        a.start(priority=0)
        b.start(priority=1)
        return c

    def drain(r, c):
        a, b = copies(r)
        a.wait()
        b.wait()
        return c

    lax.fori_loop(0, tm, issue, 0, unroll=8)
    lax.fori_loop(0, tm, drain, 0, unroll=8)


def _dispatch(d1, d2, h2, init):
    t = h2.shape[0] // ROW_TILES
    tm = min(256, t)
    return pl.pallas_call(
        _dispatch_kernel,
        out_shape=jax.ShapeDtypeStruct(init.shape, h2.dtype),
        grid_spec=pltpu.PrefetchScalarGridSpec(
            num_scalar_prefetch=2,
            grid=(t // tm,),
            in_specs=[pl.BlockSpec((tm * ROW_TILES, LANES), lambda i, a, b: (i, 0)),
                      pl.BlockSpec(memory_space=pl.ANY)],
            out_specs=pl.BlockSpec(memory_space=pl.ANY),
            scratch_shapes=[pltpu.SemaphoreType.DMA(())]),
        input_output_aliases={3: 0},
        compiler_params=_cparams(("arbitrary",)),
        name="moe_dispatch",
    )(d1, d2, h2, init)


def _expert_kernel(be_ref, xs_ref, wgu_ref, wd_ref, ys_ref):
    del be_ref
    x = _rows_from_tiles(xs_ref, EXPERT_BLOCK).astype(BF16)
    gu = _dotbf(x, wgu_ref[0])
    g = gu[:, :D_EXPERT]
    u = gu[:, D_EXPERT:]
    a = (g * jax.nn.sigmoid(g) * u).astype(BF16)
    _rows_to_tiles(ys_ref, _dotbf(a, wd_ref[0]))


def _experts(blk_expert, xs, wgu, wd):
    cap = xs.shape[0] // ROW_TILES
    d = wgu.shape[1]
    blk = pl.BlockSpec((EXPERT_BLOCK * ROW_TILES, LANES), lambda i, be: (i, 0))
    return pl.pallas_call(
        _expert_kernel,
        out_shape=jax.ShapeDtypeStruct(xs.shape, F32),
        grid_spec=pltpu.PrefetchScalarGridSpec(
            num_scalar_prefetch=1,
            grid=(cap // EXPERT_BLOCK,),
            in_specs=[blk,
                      pl.BlockSpec((1, d, 2 * D_EXPERT), lambda i, be: (be[i], 0, 0)),
                      pl.BlockSpec((1, D_EXPERT, d), lambda i, be: (be[i], 0, 0))],
            out_specs=blk),
        compiler_params=_cparams(("arbitrary",)),
        name="moe_experts",
    )(blk_expert, xs, wgu, wd)


def _combine_kernel(d1_ref, d2_ref, ys_ref, x_ref, route_ref, gate_ref, o_ref, y1_ref, y2_ref, sem):
    tm = x_ref.shape[0]
    base = pl.program_id(0) * tm

    def copies(r):
        return (pltpu.make_async_copy(_row_tile(ys_ref, d1_ref[base + r]), _row_tile(y1_ref, r), sem),
                pltpu.make_async_copy(_row_tile(ys_ref, d2_ref[base + r]), _row_tile(y2_ref, r), sem))

    def issue(r, c):
        a, b = copies(r)
        a.start(priority=0)
        b.start(priority=1)
        return c

    def drain(r, c):
        a, b = copies(r)
        a.wait()
        b.wait()
        return c

    lax.fori_loop(0, tm, issue, 0, unroll=8)
    lax.fori_loop(0, tm, drain, 0, unroll=8)
    r = route_ref[...]
    ffn = _rows_from_tiles(y1_ref, tm) * r[:, 2:3] + _rows_from_tiles(y2_ref, tm) * r[:, 3:4]
    o_ref[...] = x_ref[...] + gate_ref[0] * ffn


def _combine(d1, d2, ys, x1, route, gate2, nb, seq):
    t, d = x1.shape
    tm = min(256, seq)
    tps = seq // tm
    return pl.pallas_call(
        _combine_kernel,
        out_shape=jax.ShapeDtypeStruct((t, d), F32),
        grid_spec=pltpu.PrefetchScalarGridSpec(
            num_scalar_prefetch=2,
            grid=(t // tm,),
            in_specs=[pl.BlockSpec(memory_space=pl.ANY),
                      pl.BlockSpec((tm, d), lambda i, a, b: (i, 0)),
                      pl.BlockSpec((tm, LANES), lambda i, a, b: (i, 0)),
                      pl.BlockSpec((1, 1, d), lambda i, a, b: (i // tps, 0, 0))],
            out_specs=pl.BlockSpec((tm, d), lambda i, a, b: (i, 0)),
            scratch_shapes=[pltpu.VMEM((tm * ROW_TILES, LANES), F32), pltpu.VMEM((tm * ROW_TILES, LANES), F32),
                            pltpu.SemaphoreType.DMA(())]),
        compiler_params=_cparams(("arbitrary",)),
        name="moe_combine",
    )(d1, d2, ys, x1, route, gate2)


def _rope_tables(seq):
    rows = seq // GRID_W
    row = jnp.broadcast_to(jnp.arange(rows, dtype=F32)[:, None], (rows, GRID_W)).reshape(-1)
    col = jnp.broadcast_to(jnp.arange(GRID_W, dtype=F32)[None, :], (rows, GRID_W)).reshape(-1)
    inv_freq = ROPE_THETA ** (-jnp.arange(ROPE_PAIRS_PER_AXIS, dtype=F32) / ROPE_PAIRS_PER_AXIS)
    ang = jnp.concatenate([row[:, None] * inv_freq, col[:, None] * inv_freq], axis=-1)
    cos = jnp.repeat(jnp.cos(ang), 2, axis=-1)
    sin = jnp.repeat(jnp.sin(ang), 2, axis=-1)
    sign = jnp.tile(jnp.array([-1.0, 1.0], F32), HEAD_DIM // 2)
    return jnp.tile(cos, (1, N_HEADS)), jnp.tile(sin * sign, (1, N_HEADS))


def _filter_features(seq):
    t = jnp.linspace(0.0, 1.0, seq, dtype=F32)[:, None]
    w = (2.0 * math.pi / seq) * jnp.arange(seq, dtype=F32)[:, None]
    bands = jnp.linspace(1e-4, FILTER_BANDS - 1, FILTER_BANDS, dtype=F32)[None, :]
    z = jnp.concatenate([t, jnp.cos(bands * w), -jnp.sin(bands * w)], axis=-1)
    pad = jnp.zeros((seq, FILTER_ORDER - FILTER_EMB), F32)
    fwd = jnp.concatenate([z, pad], axis=-1).at[:, FILTER_EMB].set(1.0)
    rev = jnp.concatenate([z[:1], z[:0:-1]], axis=0)
    sign = jnp.concatenate([jnp.zeros((1,), F32), -jnp.ones((seq - 1,), F32)])
    bwd = jnp.concatenate([rev, pad], axis=-1).at[:, FILTER_EMB].set(sign)
    deltas = jnp.abs(jnp.linspace(MIN_DECAY, MAX_DECAY, D_HYENA, dtype=F32))[None, :]
    return jnp.concatenate([fwd, bwd], axis=-1), deltas


def _run_trunk(x, mod, p):
    nb, seq, d = x.shape
    t = nb * seq
    depth = mod.shape[0]
    cos, sin = _rope_tables(seq)
    zfeat, deltas = _filter_features(seq)
    tabs = _dft_tables(seq)
    h1 = tabs["h1"]
    tm_rank = min(512, t)
    tri = jnp.tril(jnp.ones((tm_rank, tm_rank), BF16), -1)
    cap = t * TOP_K + N_EXPERTS * EXPERT_BLOCK
    n_blk = cap // EXPERT_BLOCK
    lane_e = jnp.arange(LANES)

    xf = x.reshape(t, d)
    xs = jnp.zeros((cap * ROW_TILES, LANES), F32)
    for l in range(depth):
        m6 = mod[l].reshape(nb, 6, 1, d)
        shift1, scale1, gate1, shift2, scale2, gate2 = (m6[:, j] for j in range(6))

        q, k, vt, z, x2, gates = _inproj(xf, shift1, scale1, p["norm_mix"][l], p["w_in"][l], p["w_vt"][l],
                                         p["qg"][l], p["kg"][l], p["bd"], cos, sin,
                                         p["conv_w"][l], p["conv_b"][l], nb, seq)
        z = z.reshape(nb, seq, D_HYENA)
        x2 = x2.reshape(nb, seq, D_HYENA)
        bound = HEAD_DIM * jnp.max(jnp.abs(p["qg"][l])) * jnp.max(jnp.abs(p["kg"][l]))
        attn = lax.cond(bound <= SCORE_BOUND,
                        functools.partial(_attention, online=False),
                        functools.partial(_attention, online=True), q, k, vt).reshape(t, ATTN_WIDTH)

        kern = _filters(zfeat, p["filt_w1"][l], p["filt_b1"][l], p["filt_w2"][l], p["filt_b2"][l],
                        p["filt_w3"][l], p["filt_b3"][l], p["filt_freq"][l], p["filt_w4"][l], deltas)
        ka = _fft_outer(tabs["g_full"], kern.reshape(1, 2 * seq, D_HYENA), FFT_N2)
        kf = _fft_filter_inner(tabs["mf"], ka.reshape(1, 2, h1, FFT_N2, D_HYENA))
        za = _fft_outer(tabs["g_half"], z, FFT_N2)
        zb = _fft_inner(tabs["mf"], tabs["mi"], kf, za.reshape(nb, 2, h1, FFT_N2, D_HYENA))
        hy = _fft_final(tabs["g_out"], zb.reshape(nb, 2 * h1 * FFT_N2, D_HYENA), z, x2,
                        p["skip"][l], FFT_N2).reshape(t, D_HYENA)

        x1, h2, route, counts = _merge(attn, hy, gates, xf, gate1, shift2, scale2, p["norm_ffn"][l],
                                       p["w_br_attn"][l], p["w_br_hyena"][l], p["w_out"][l],
                                       p["wr_hi"][l], p["wr_lo"][l], p["b_route"][l], nb, seq)

        cnt = counts[0].astype(jnp.int32)
        padded = jnp.where(lane_e < N_EXPERTS, (cnt + EXPERT_BLOCK - 1) // EXPERT_BLOCK * EXPERT_BLOCK, 0)
        pad_end = jnp.cumsum(padded)
        pstart = (pad_end - padded).astype(F32)[None, :]
        blk_start = jnp.arange(n_blk, dtype=jnp.int32) * EXPERT_BLOCK
        blk_expert = jnp.minimum(
            jnp.sum(blk_start[:, None] >= pad_end[None, :N_EXPERTS], axis=1), N_EXPERTS - 1).astype(jnp.int32)

        dest = _rank(route, pstart, tri)
        d1 = dest[:, 0].astype(jnp.int32)
        d2 = dest[:, 1].astype(jnp.int32)
        xs = _dispatch(d1, d2, h2, xs)
        ys = _experts(blk_expert, xs, p["w_gu"][l], p["w_down"][l])
        xf = _combine(d1, d2, ys, x1, route, gate2, nb, seq)
    return xf.reshape(nb, seq, d)


def kernel(x_prompt, x_sample, c_prompt, c_sample, w_ada, b_ada, norm_mix, norm_ffn, w_in, q_gain, k_gain, conv_w, conv_b, filt_w1, filt_b1, filt_w2, filt_b2, filt_w3, filt_b3, filt_freq, filt_w4, hyena_skip, w_br_attn, w_br_hyena, w_out, w_group, b_group, w_router, b_router, w_e_gate, w_e_up, w_e_down):
    depth = w_ada.shape[0]
    bp, bs = c_prompt.shape[0], c_sample.shape[0]
    rows = -(-(bp + bs) // SUBLANES) * SUBLANES
    c_pad = jnp.zeros((rows, D_MODEL), F32).at[:bp].set(c_prompt).at[bp:bp + bs].set(c_sample)
    mod = _ada(c_pad, w_ada, b_ada)

    scale = HEAD_DIM ** -0.5 * math.log2(math.e)
    head_id = np.arange(ATTN_WIDTH) // HEAD_DIM
    route_w = jnp.concatenate([w_router, w_group], axis=-1)
    route_w = jnp.pad(route_w, ((0, 0), (0, 0), (0, LANES - route_w.shape[-1])))
    wr_hi = route_w.astype(BF16)

    def block_diag(a, b):
        return jnp.concatenate([jnp.pad(a, ((0, 0), (0, 0), (0, b.shape[2]))),
                                jnp.pad(b, ((0, 0), (0, 0), (a.shape[2], 0)))], axis=1)

    twice = lambda w: block_diag(w, w)
    p = dict(
        norm_mix=norm_mix.reshape(depth, 1, D_MODEL),
        norm_ffn=norm_ffn.reshape(depth, 1, D_MODEL),
        w_in=w_in.astype(BF16),
        w_vt=jnp.swapaxes(w_in[:, :, COL_V:COL_U], 1, 2).astype(BF16),
        qg=(jnp.tile(q_gain, (1, N_HEADS)) * scale).reshape(depth, 1, ATTN_WIDTH),
        kg=jnp.tile(k_gain, (1, N_KV_HEADS)).reshape(depth, 1, KV_WIDTH),
        bd=jnp.asarray(head_id[:, None] == head_id[None, :], dtype=BF16),
        conv_w=conv_w, conv_b=conv_b.reshape(depth, 1, 3 * D_HYENA),
        filt_w1=twice(jnp.pad(filt_w1, ((0, 0), (0, FILTER_ORDER - FILTER_EMB), (0, 0)))),
        filt_b1=jnp.tile(filt_b1, (1, 2)).reshape(depth, 1, 2 * FILTER_ORDER),
        filt_w2=twice(filt_w2), filt_b2=jnp.tile(filt_b2, (1, 2)).reshape(depth, 1, 2 * FILTER_ORDER),
        filt_w3=twice(filt_w3), filt_b3=jnp.tile(filt_b3, (1, 2)).reshape(depth, 1, 2 * FILTER_ORDER),
        filt_freq=jnp.tile(filt_freq, (1, 2)).reshape(depth, 1, 2 * FILTER_ORDER),
        filt_w4=block_diag(filt_w4[:, :, :D_HYENA], filt_w4[:, :, D_HYENA:]),
        skip=hyena_skip.reshape(depth, 1, D_HYENA),
        w_br_attn=w_br_attn.astype(BF16), w_br_hyena=w_br_hyena.astype(BF16), w_out=w_out.astype(BF16),
        wr_hi=wr_hi, wr_lo=(route_w - wr_hi.astype(F32)).astype(BF16),
        b_route=jnp.pad(jnp.concatenate([b_router, b_group], axis=-1),
                        ((0, 0), (0, LANES - N_EXPERTS - N_GROUPS))).reshape(depth, 1, LANES),
        w_gu=jnp.concatenate([w_e_gate, w_e_up], axis=-1).astype(BF16),
        w_down=w_e_down.astype(BF16),
    )
    y_prompt = _run_trunk(x_prompt, mod[:, :bp], p)
    y_sample = _run_trunk(x_sample, mod[:, bp:bp + bs], p)
    return (y_prompt, y_sample)
```

```python
import functools
import math

import numpy as np
import jax
import jax.numpy as jnp
from jax import lax
from jax.experimental import pallas as pl
from jax.experimental.pallas import tpu as pltpu

F32 = jnp.float32
BF16 = jnp.bfloat16

D_MODEL = 1024
GRID_W = 64
N_HEADS = 8
N_KV_HEADS = 4
HEAD_DIM = 64
Q_PER_KV = N_HEADS // N_KV_HEADS
ATTN_WIDTH = N_HEADS * HEAD_DIM
KV_WIDTH = N_KV_HEADS * HEAD_DIM
ROPE_THETA = 10000.0
ROPE_PAIRS_PER_AXIS = HEAD_DIM // 4
D_HYENA = D_MODEL // 2
FILTER_EMB = 33
FILTER_BANDS = (FILTER_EMB - 1) // 2
FILTER_ORDER = 64
DECAY_TARGET = 1e-2
MAX_DECAY = math.log(DECAY_TARGET) / 0.3
MIN_DECAY = math.log(DECAY_TARGET) / 1.5
MOD_SHIFT = 0.05
N_GROUPS = 4
EXPERTS_PER_GROUP = 8
N_EXPERTS = N_GROUPS * EXPERTS_PER_GROUP
TOP_K = 2
D_EXPERT = D_MODEL // 4
RMS_EPS = 1e-6
IN_COLS = ATTN_WIDTH + 2 * KV_WIDTH + 3 * D_HYENA + 2 * D_MODEL
COL_K = ATTN_WIDTH
COL_V = ATTN_WIDTH + KV_WIDTH
COL_U = ATTN_WIDTH + 2 * KV_WIDTH
COL_G = COL_U + 3 * D_HYENA

LANES = 128
SUBLANES = 8
ROW_TILES = D_MODEL // 2 // LANES
VMEM_LIMIT = 56 * 1024 * 1024

V_ROWS = HEAD_DIM + 16
SCORE_BOUND = 60.0
FFT_N2 = 128
EXPERT_BLOCK = 512
NEG_BIG = -1e30

_HI = lax.Precision.HIGHEST


def _dot32(a, b):
    return jnp.dot(a, b, precision=_HI, preferred_element_type=F32)


def _dotbf(a, b):
    return jnp.dot(a, b, preferred_element_type=F32)


def _cparams(sem):
    return pltpu.CompilerParams(dimension_semantics=sem, vmem_limit_bytes=VMEM_LIMIT)


def _ada_kernel(c_ref, w_ref, b_ref, o_ref):
    c = c_ref[...]
    act = c * jax.nn.sigmoid(c)
    o_ref[0] = _dot32(act, w_ref[0]) + b_ref[0]


def _ada(c_pad, w_ada, b_ada):
    depth, d, n = w_ada.shape
    tn = 1536
    return pl.pallas_call(
        _ada_kernel,
        out_shape=jax.ShapeDtypeStruct((depth, c_pad.shape[0], n), F32),
        grid=(depth, n // tn),
        in_specs=[pl.BlockSpec(c_pad.shape, lambda l, j: (0, 0)),
                  pl.BlockSpec((1, d, tn), lambda l, j: (l, 0, j)),
                  pl.BlockSpec((1, 1, tn), lambda l, j: (l, 0, j))],
        out_specs=pl.BlockSpec((1, c_pad.shape[0], tn), lambda l, j: (l, 0, j)),
        compiler_params=_cparams(("arbitrary", "arbitrary")),
        name="ada_mod",
    )(c_pad, w_ada, b_ada.reshape(depth, 1, n))


def _swap_pairs(x):
    n = x.shape[-1]
    lane = lax.broadcasted_iota(jnp.int32, x.shape, 1)
    nxt = pltpu.roll(x, n - 1, 1)
    prv = pltpu.roll(x, 1, 1)
    return jnp.where(lane % 2 == 0, nxt, prv)


def _head_norm_rope(p, gain, bd, cos, sin_signed):
    sq = (p * p).astype(BF16)
    ms = _dotbf(sq, bd) * (1.0 / HEAD_DIM)
    pn = p * lax.rsqrt(ms + RMS_EPS) * gain
    return pn * cos + _swap_pairs(pn) * sin_signed


def _inproj_kernel(x_ref, xp_ref, xn_ref, shift_ref, scale_ref, gain_ref, w_ref, wvt_ref, qg_ref, kg_ref,
                   bd_ref, cos_ref, sin_ref, cw_ref, cb_ref, q_ref, k_ref, vt_ref, z_ref, x2_ref, g_ref, *, tps):
    def normed(x):
        ms = jnp.mean(x * x, axis=-1, keepdims=True)
        h = x * lax.rsqrt(ms + RMS_EPS) * gain_ref[...]
        return (h * (1.0 + scale_ref[0]) + shift_ref[0]).astype(BF16)

    hb = normed(x_ref[...])
    tm = hb.shape[0]
    cos = cos_ref[...]
    sin = sin_ref[...]
    bd = bd_ref[...]

    q = _dotbf(hb, w_ref[:, 0:COL_K])
    q = _head_norm_rope(q, qg_ref[...], bd, cos, sin)
    for hd in range(N_HEADS):
        q_ref[0, hd] = q[:, hd * HEAD_DIM:(hd + 1) * HEAD_DIM].astype(BF16)

    k = _dotbf(hb, w_ref[:, COL_K:COL_V])
    k = _head_norm_rope(k, kg_ref[...], bd[:KV_WIDTH, :KV_WIDTH], cos[:, :KV_WIDTH], sin[:, :KV_WIDTH])
    vt = lax.dot_general(wvt_ref[...], hb, (((1,), (1,)), ((), ())), preferred_element_type=F32)
    ones = jnp.ones((V_ROWS - HEAD_DIM, vt.shape[1]), BF16)
    for hd in range(N_KV_HEADS):
        k_ref[0, hd] = k[:, hd * HEAD_DIM:(hd + 1) * HEAD_DIM].astype(BF16)
        vt_ref[0, hd, 0:HEAD_DIM, :] = vt[hd * HEAD_DIM:(hd + 1) * HEAD_DIM].astype(BF16)
        vt_ref[0, hd, HEAD_DIM:V_ROWS, :] = ones

    g_ref[...] = jax.nn.sigmoid(_dotbf(hb, w_ref[:, COL_G:IN_COLS])).astype(BF16)

    halo = normed(jnp.concatenate([xp_ref[...], xn_ref[...]], axis=0))
    u_all = _dotbf(jnp.concatenate([hb, halo], axis=0), w_ref[:, COL_U:COL_G])
    u = u_all[:tm]
    pos = pl.program_id(0) % tps
    prev_row = u_all[tm + SUBLANES - 1:tm + SUBLANES] * (pos > 0).astype(F32)
    next_row = u_all[tm + SUBLANES:tm + SUBLANES + 1] * (pos < tps - 1).astype(F32)
    ridx = lax.broadcasted_iota(jnp.int32, u.shape, 0)
    u_prev = jnp.where(ridx == 0, prev_row, pltpu.roll(u, 1, 0))
    u_next = jnp.where(ridx == tm - 1, next_row, pltpu.roll(u, tm - 1, 0))
    w = cw_ref[...]
    c = cb_ref[...] + u_prev * w[0:1] + u * w[1:2] + u_next * w[2:3]
    z_ref[...] = c[:, 2 * D_HYENA:3 * D_HYENA] * c[:, 0:D_HYENA]
    x2_ref[...] = c[:, D_HYENA:2 * D_HYENA]


def _inproj(x, shift, scale, gain, w_in, w_vt, qg, kg, bd, cos, sin, conv_w, conv_b, nb, seq):
    t = nb * seq
    tm = min(512, seq)
    tps = seq // tm
    rpb = tm // SUBLANES
    row = lambda i: (i, 0)
    per_b = lambda i: (i // tps, 0, 0)
    const2 = lambda i: (0, 0)
    pos = lambda i: (i % tps, 0)
    head_out = lambda i: (i // tps, 0, i % tps, 0)
    return pl.pallas_call(
        functools.partial(_inproj_kernel, tps=tps),
        out_shape=(jax.ShapeDtypeStruct((nb, N_HEADS, seq, HEAD_DIM), BF16),
                   jax.ShapeDtypeStruct((nb, N_KV_HEADS, seq, HEAD_DIM), BF16),
                   jax.ShapeDtypeStruct((nb, N_KV_HEADS, V_ROWS, seq), BF16),
                   jax.ShapeDtypeStruct((t, D_HYENA), F32),
                   jax.ShapeDtypeStruct((t, D_HYENA), F32),
                   jax.ShapeDtypeStruct((t, 2 * D_MODEL), BF16)),
        grid=(t // tm,),
        in_specs=[pl.BlockSpec((tm, D_MODEL), row),
                  pl.BlockSpec((SUBLANES, D_MODEL), lambda i: (jnp.maximum(i * rpb - 1, 0), 0)),
                  pl.BlockSpec((SUBLANES, D_MODEL), lambda i: (jnp.minimum((i + 1) * rpb, t // SUBLANES - 1), 0)),
                  pl.BlockSpec((1, 1, D_MODEL), per_b),
                  pl.BlockSpec((1, 1, D_MODEL), per_b),
                  pl.BlockSpec((1, D_MODEL), const2),
                  pl.BlockSpec((D_MODEL, IN_COLS), const2),
                  pl.BlockSpec((KV_WIDTH, D_MODEL), const2),
                  pl.BlockSpec((1, ATTN_WIDTH), const2),
                  pl.BlockSpec((1, KV_WIDTH), const2),
                  pl.BlockSpec((ATTN_WIDTH, ATTN_WIDTH), const2),
                  pl.BlockSpec((tm, ATTN_WIDTH), pos),
                  pl.BlockSpec((tm, ATTN_WIDTH), pos),
                  pl.BlockSpec((3, 3 * D_HYENA), const2),
                  pl.BlockSpec((1, 3 * D_HYENA), const2)],
        out_specs=(pl.BlockSpec((1, N_HEADS, tm, HEAD_DIM), head_out),
                   pl.BlockSpec((1, N_KV_HEADS, tm, HEAD_DIM), head_out),
                   pl.BlockSpec((1, N_KV_HEADS, V_ROWS, tm), lambda i: (i // tps, 0, 0, i % tps)),
                   pl.BlockSpec((tm, D_HYENA), row),
                   pl.BlockSpec((tm, D_HYENA), row),
                   pl.BlockSpec((tm, 2 * D_MODEL), row)),
        compiler_params=_cparams(("arbitrary",)),
        name="in_proj",
    )(x, x, x, shift, scale, gain, w_in, w_vt, qg, kg, bd, cos, sin, conv_w, conv_b)


def _attn_kernel(q_ref, k_ref, vt_ref, o_ref, m_ref, acc_ref, *, tkc, online):
    seq = k_ref.shape[2]
    nchunk = seq // tkc
    acc_ref[...] = jnp.zeros(acc_ref.shape, F32)
    if online:
        m_ref[...] = jnp.full(m_ref.shape, NEG_BIG, F32)

    def body(c, carry):
        off = pl.multiple_of(c * tkc, tkc)
        kc = k_ref[0, 0, pl.ds(off, tkc), :]
        vc = vt_ref[0, 0, :, pl.ds(off, tkc)]
        for h in range(Q_PER_KV):
            st = lax.dot_general(kc, q_ref[0, h], (((1,), (1,)), ((), ())), preferred_element_type=F32)
            if online:
                m_prev = m_ref[h]
                m_new = jnp.maximum(m_prev, jnp.max(st, axis=0, keepdims=True))
                p = jnp.exp2(st - m_new).astype(BF16)
                acc_ref[h] = jnp.exp2(m_prev - m_new) * acc_ref[h] + _dotbf(vc, p)
                m_ref[h] = m_new
            else:
                acc_ref[h] += _dotbf(vc, jnp.exp2(st).astype(BF16))
        return carry

    lax.fori_loop(0, nchunk, body, 0, unroll=2 if nchunk % 2 == 0 else 1)
    outs = []
    for h in range(Q_PER_KV):
        a = acc_ref[h]
        outs.append((a[:HEAD_DIM] / a[HEAD_DIM:HEAD_DIM + 1]).T)
    o_ref[0] = jnp.concatenate(outs, axis=1).astype(BF16)


def _attention(q, k, vt, online):
    nb, _, seq, _ = q.shape
    tq = min(512 if online else 1024, seq)
    tkc = min(512 if online else 1024, seq)
    return pl.pallas_call(
        functools.partial(_attn_kernel, tkc=tkc, online=online),
        out_shape=jax.ShapeDtypeStruct((nb, seq, ATTN_WIDTH), BF16),
        grid=(nb, N_KV_HEADS, seq // tq),
        in_specs=[pl.BlockSpec((1, Q_PER_KV, tq, HEAD_DIM), lambda b, g, i: (b, g, i, 0)),
                  pl.BlockSpec((1, 1, seq, HEAD_DIM), lambda b, g, i: (b, g, 0, 0)),
                  pl.BlockSpec((1, 1, V_ROWS, seq), lambda b, g, i: (b, g, 0, 0))],
        out_specs=pl.BlockSpec((1, tq, Q_PER_KV * HEAD_DIM), lambda b, g, i: (b, i, g)),
        scratch_shapes=[pltpu.VMEM((Q_PER_KV, 1, tq), F32),
                        pltpu.VMEM((Q_PER_KV, V_ROWS, tq), F32)],
        compiler_params=_cparams(("arbitrary", "arbitrary", "arbitrary")),
        name="attn_online" if online else "attn_bounded",
    )(q, k, vt)


def _filter_kernel(z_ref, w1_ref, b1_ref, w2_ref, b2_ref, w3_ref, b3_ref, fr_ref, w4_ref,
                   dl_ref, o_ref):
    z = z_ref[...]
    fr = fr_ref[...]
    h = jnp.sin(fr * (_dot32(z, w1_ref[...]) + b1_ref[...]))
    h = jnp.sin(fr * (_dot32(h, w2_ref[...]) + b2_ref[...]))
    h = jnp.sin(fr * (_dot32(h, w3_ref[...]) + b3_ref[...]))
    h = _dot32(h, w4_ref[...])
    dl = dl_ref[...]
    for d in range(2):
        t = z[:, d * FILTER_ORDER:d * FILTER_ORDER + 1]
        sign = z[:, d * FILTER_ORDER + FILTER_EMB:d * FILTER_ORDER + FILTER_EMB + 1]
        o_ref[d] = h[:, d * D_HYENA:(d + 1) * D_HYENA] * (jnp.exp(-t * dl) + MOD_SHIFT) * sign


def _filters(zfeat, w1, b1, w2, b2, w3, b3, fr, w4, deltas):
    seq, fe = zfeat.shape
    tm = min(512, seq)
    c2 = lambda i: (0, 0)
    return pl.pallas_call(
        _filter_kernel,
        out_shape=jax.ShapeDtypeStruct((2, seq, D_HYENA), F32),
        grid=(seq // tm,),
        in_specs=[pl.BlockSpec((tm, fe), lambda i: (i, 0)),
                  pl.BlockSpec(w1.shape, c2), pl.BlockSpec(b1.shape, c2),
                  pl.BlockSpec(w2.shape, c2), pl.BlockSpec(b2.shape, c2),
                  pl.BlockSpec(w3.shape, c2), pl.BlockSpec(b3.shape, c2),
                  pl.BlockSpec(fr.shape, c2), pl.BlockSpec(w4.shape, c2),
                  pl.BlockSpec(deltas.shape, c2)],
        out_specs=pl.BlockSpec((2, tm, D_HYENA), lambda i: (0, i, 0)),
        compiler_params=_cparams(("arbitrary",)),
        name="hyena_filters",
    )(zfeat, w1, b1, w2, b2, w3, b3, fr, w4, deltas)


def _fft_outer_kernel(g_ref, x_ref, o_ref, *, n2):
    g = g_ref[...]
    m, kk = g.shape

    def body(jp, c):
        j = 2 * jp
        xj = jnp.concatenate([x_ref[pl.ds(j, kk, stride=n2), :], x_ref[pl.ds(j + 1, kk, stride=n2), :]], axis=1)
        r = _dotbf(g, xj.astype(BF16))
        o_ref[pl.ds(j, m, stride=n2), :] = r[:, :LANES]
        o_ref[pl.ds(j + 1, m, stride=n2), :] = r[:, LANES:]
        return c

    lax.fori_loop(0, n2 // 2, body, 0, unroll=4)


def _fft_outer(gmat, x, n2):
    nb, rows, ch = x.shape
    m, kk = gmat.shape
    return pl.pallas_call(
        functools.partial(_fft_outer_kernel, n2=n2),
        out_shape=jax.ShapeDtypeStruct((nb, m * n2, ch), F32),
        grid=(nb, ch // LANES),
        in_specs=[pl.BlockSpec((m, kk), lambda b, c: (0, 0)),
                  pl.BlockSpec((None, rows, LANES), lambda b, c: (b, 0, c))],
        out_specs=pl.BlockSpec((None, m * n2, LANES), lambda b, c: (b, 0, c)),
        compiler_params=_cparams(("arbitrary", "arbitrary")),
        name="fft_outer",
    )(gmat, x)


FFT_K1_PER_STEP = 4


def _fft_filter_inner_kernel(mf_ref, a_ref, o_ref):
    n2, ch = a_ref.shape[3], a_ref.shape[4]
    for r in range(a_ref.shape[2]):
        a = a_ref[0, :, r].reshape(2 * n2, ch)
        o_ref[r] = _dotbf(mf_ref[r], a.astype(BF16)).reshape(2, n2, ch)


def _fft_filter_inner(mf, a5):
    _, _, h1, n2, ch = a5.shape
    kr = FFT_K1_PER_STEP
    return pl.pallas_call(
        _fft_filter_inner_kernel,
        out_shape=jax.ShapeDtypeStruct((h1, 2, n2, ch), F32),
        grid=(h1 // kr,),
        in_specs=[pl.BlockSpec((kr, 2 * n2, 2 * n2), lambda k: (k, 0, 0)),
                  pl.BlockSpec((1, 2, kr, n2, ch), lambda k: (0, 0, k, 0, 0))],
        out_specs=pl.BlockSpec((kr, 2, n2, ch), lambda k: (k, 0, 0, 0)),
        compiler_params=_cparams(("arbitrary",)),
        name="fft_filter_inner",
    )(mf, a5)


def _fft_inner_kernel(mf_ref, mi_ref, kf_ref, a_ref, o_ref):
    n2, ch = a_ref.shape[3], a_ref.shape[4]
    for r in range(a_ref.shape[2]):
        a = a_ref[0, :, r].reshape(2 * n2, ch)
        xs = _dotbf(mf_ref[r], a.astype(BF16))
        xr, xi = xs[:n2], xs[n2:]
        kr, ki = kf_ref[r, 0], kf_ref[r, 1]
        p = jnp.concatenate([xr * kr - xi * ki, xr * ki + xi * kr], axis=0)
        o_ref[0, :, r] = _dotbf(mi_ref[r], p.astype(BF16)).reshape(2, n2, ch)


def _fft_inner(mf, mi, kf, a5):
    nb, _, h1, n2, ch = a5.shape
    kr = FFT_K1_PER_STEP
    return pl.pallas_call(
        _fft_inner_kernel,
        out_shape=jax.ShapeDtypeStruct(a5.shape, F32),
        grid=(h1 // kr, nb),
        in_specs=[pl.BlockSpec((kr, 2 * n2, 2 * n2), lambda k, b: (k, 0, 0)),
                  pl.BlockSpec((kr, 2 * n2, 2 * n2), lambda k, b: (k, 0, 0)),
                  pl.BlockSpec((kr, 2, n2, ch), lambda k, b: (k, 0, 0, 0)),
                  pl.BlockSpec((1, 2, kr, n2, ch), lambda k, b: (b, 0, k, 0, 0))],
        out_specs=pl.BlockSpec((1, 2, kr, n2, ch), lambda k, b: (b, 0, k, 0, 0)),
        compiler_params=_cparams(("arbitrary", "arbitrary")),
        name="fft_inner",
    )(mf, mi, kf, a5)


def _fft_final_kernel(g_ref, b_ref, z_ref, x2_ref, skip_ref, o_ref, *, n2):
    g = g_ref[...]
    h1, m = g.shape
    skip = skip_ref[...]

    def body(jp, c):
        j = 2 * jp
        bj = jnp.concatenate([b_ref[pl.ds(j, m, stride=n2), :], b_ref[pl.ds(j + 1, m, stride=n2), :]], axis=1)
        y = _dotbf(g, bj.astype(BF16))
        for d in range(2):
            zj = z_ref[pl.ds(j + d, h1, stride=n2), :]
            xj = x2_ref[pl.ds(j + d, h1, stride=n2), :]
            o_ref[pl.ds(j + d, h1, stride=n2), :] = xj * (y[:, d * LANES:(d + 1) * LANES] + zj * skip)
        return c

    lax.fori_loop(0, n2 // 2, body, 0, unroll=4)


def _fft_final(gc, bm, z, x2, skip, n2):
    nb, seq, ch = z.shape
    h1, m = gc.shape
    blk = lambda rows: pl.BlockSpec((None, rows, LANES), lambda b, c: (b, 0, c))
    return pl.pallas_call(
        functools.partial(_fft_final_kernel, n2=n2),
        out_shape=jax.ShapeDtypeStruct((nb, seq, ch), F32),
        grid=(nb, ch // LANES),
        in_specs=[pl.BlockSpec((h1, m), lambda b, c: (0, 0)), blk(m * n2), blk(seq), blk(seq),
                  pl.BlockSpec((1, LANES), lambda b, c: (0, c))],
        out_specs=blk(seq),
        compiler_params=_cparams(("arbitrary", "arbitrary")),
        name="fft_final",
    )(gc, bm, z, x2, skip)


def _dft_tables(seq):
    n = 2 * seq
    n2 = FFT_N2
    n1 = n // n2
    h1 = n1 // 2
    i32 = jnp.int32
    k1 = jnp.arange(h1, dtype=i32)[:, None]
    a1 = jnp.arange(n1, dtype=i32)[None, :]
    th = (math.pi / n1) * ((a1 * (2 * k1 + 1)) % (2 * n1)).astype(F32)
    g_re, g_im = jnp.cos(th), -jnp.sin(th)
    g_full = jnp.concatenate([g_re, g_im], axis=0)
    g_half = g_full[:, :h1]
    g_out = (2.0 / n) * jnp.concatenate([g_re[:, :h1].T, g_im[:, :h1].T], axis=1)
    k2 = jnp.arange(n2, dtype=i32)[None, :, None]
    b2 = jnp.arange(n2, dtype=i32)[None, None, :]
    kk = jnp.arange(h1, dtype=i32)[:, None, None]
    ph = (math.pi / n) * ((b2 * (2 * kk + 1 + 2 * n1 * k2)) % (2 * n)).astype(F32)
    m_re, m_im = jnp.cos(ph), -jnp.sin(ph)
    mf = jnp.concatenate([jnp.concatenate([m_re, -m_im], axis=2),
                          jnp.concatenate([m_im, m_re], axis=2)], axis=1)
    mt_re, mt_im = jnp.transpose(m_re, (0, 2, 1)), jnp.transpose(m_im, (0, 2, 1))
    mi = jnp.concatenate([jnp.concatenate([mt_re, mt_im], axis=2),
                          jnp.concatenate([-mt_im, mt_re], axis=2)], axis=1)
    b16 = lambda a: a.astype(BF16)
    return dict(n1=n1, h1=h1, g_full=b16(g_full), g_half=b16(g_half), g_out=b16(g_out), mf=b16(mf), mi=b16(mi))


def _merge_kernel(attn_ref, hy_ref, g_ref, x_ref, gate_ref, shift_ref, scale_ref, gain_ref,
                  wa_ref, wh_ref, wo_ref, wrh_ref, wrl_ref, br_ref,
                  x1_ref, h2_ref, route_ref, cnt_ref):
    g = g_ref[...].astype(F32)
    merged = (g[:, :D_MODEL] * _dotbf(attn_ref[...], wa_ref[...])
              + g[:, D_MODEL:] * _dotbf(hy_ref[...].astype(BF16), wh_ref[...]))
    mix = _dotbf(merged.astype(BF16), wo_ref[...])
    x1 = x_ref[...] + gate_ref[0] * mix
    x1_ref[...] = x1

    ms = jnp.mean(x1 * x1, axis=-1, keepdims=True)
    h2 = x1 * lax.rsqrt(ms + RMS_EPS) * gain_ref[...]
    h2 = h2 * (1.0 + scale_ref[0]) + shift_ref[0]
    _rows_to_tiles(h2_ref, _pack_halves(h2))

    hi = h2.astype(BF16)
    lo = (h2 - hi.astype(F32)).astype(BF16)
    wrh = wrh_ref[...]
    lg = _dotbf(hi, wrh) + _dotbf(lo, wrh) + _dotbf(hi, wrl_ref[...]) + br_ref[...]

    lane = lax.broadcasted_iota(jnp.int32, lg.shape, 1).astype(F32)
    is_grp = jnp.logical_and(lane >= N_EXPERTS, lane < N_EXPERTS + N_GROUPS)
    gm = jnp.where(is_grp, lg, NEG_BIG)
    gmax = jnp.max(gm, axis=-1, keepdims=True)
    gidx = jnp.min(jnp.where(gm == gmax, lane, 1e9), axis=-1, keepdims=True) - N_EXPERTS
    p_group = 1.0 / jnp.sum(jnp.where(is_grp, jnp.exp(gm - gmax), 0.0), axis=-1, keepdims=True)
    lo_lane = gidx * EXPERTS_PER_GROUP
    in_grp = jnp.logical_and(lane >= lo_lane, lane < lo_lane + EXPERTS_PER_GROUP)
    e1v = jnp.where(in_grp, lg, NEG_BIG)
    t1 = jnp.max(e1v, axis=-1, keepdims=True)
    i1 = jnp.min(jnp.where(e1v == t1, lane, 1e9), axis=-1, keepdims=True)
    e2v = jnp.where(lane == i1, NEG_BIG, e1v)
    t2 = jnp.max(e2v, axis=-1, keepdims=True)
    i2 = jnp.min(jnp.where(e2v == t2, lane, 1e9), axis=-1, keepdims=True)
    d = jnp.exp(t2 - t1)
    w1 = p_group / (1.0 + d)
    w2 = p_group * d / (1.0 + d)
    route_ref[...] = jnp.where(lane == 0, i1, jnp.where(lane == 1, i2,
                               jnp.where(lane == 2, w1, jnp.where(lane == 3, w2, 0.0))))

    onehot = (lane == i1).astype(F32) + (lane == i2).astype(F32)

    @pl.when(pl.program_id(0) == 0)
    def _():
        cnt_ref[...] = jnp.zeros(cnt_ref.shape, F32)

    cnt_ref[...] += jnp.sum(onehot, axis=0, keepdims=True)


def _merge(attn, hy, gates, x, gate1, shift2, scale2, gain, wa, wh, wo, wrh, wrl, br, nb, seq):
    t = nb * seq
    tm = min(512, seq)
    tps = seq // tm
    row = lambda i: (i, 0)
    per_b = lambda i: (i // tps, 0, 0)
    c2 = lambda i: (0, 0)
    return pl.pallas_call(
        _merge_kernel,
        out_shape=(jax.ShapeDtypeStruct((t, D_MODEL), F32),
                   jax.ShapeDtypeStruct((t * ROW_TILES, LANES), F32),
                   jax.ShapeDtypeStruct((t, LANES), F32),
                   jax.ShapeDtypeStruct((1, LANES), F32)),
        grid=(t // tm,),
        in_specs=[pl.BlockSpec((tm, ATTN_WIDTH), row),
                  pl.BlockSpec((tm, D_HYENA), row),
                  pl.BlockSpec((tm, 2 * D_MODEL), row),
                  pl.BlockSpec((tm, D_MODEL), row),
                  pl.BlockSpec((1, 1, D_MODEL), per_b),
                  pl.BlockSpec((1, 1, D_MODEL), per_b),
                  pl.BlockSpec((1, 1, D_MODEL), per_b),
                  pl.BlockSpec((1, D_MODEL), c2),
                  pl.BlockSpec(wa.shape, c2), pl.BlockSpec(wh.shape, c2), pl.BlockSpec(wo.shape, c2),
                  pl.BlockSpec(wrh.shape, c2), pl.BlockSpec(wrl.shape, c2), pl.BlockSpec(br.shape, c2)],
        out_specs=(pl.BlockSpec((tm, D_MODEL), row),
                   pl.BlockSpec((tm * ROW_TILES, LANES), row),
                   pl.BlockSpec((tm, LANES), row),
                   pl.BlockSpec((1, LANES), c2)),
        compiler_params=_cparams(("arbitrary",)),
        name="merge_router",
    )(attn, hy, gates, x, gate1, shift2, scale2, gain, wa, wh, wo, wrh, wrl, br)


def _rank_kernel(route_ref, pstart_ref, tri_ref, dest_ref, carry_ref):
    @pl.when(pl.program_id(0) == 0)
    def _():
        carry_ref[...] = jnp.zeros(carry_ref.shape, F32)

    r = route_ref[...]
    lane = lax.broadcasted_iota(jnp.int32, r.shape, 1).astype(F32)
    oh1 = (lane == r[:, 0:1]).astype(F32)
    oh2 = (lane == r[:, 1:2]).astype(F32)
    tri = tri_ref[...]
    before1 = _dotbf(tri, oh1.astype(BF16))
    before2 = _dotbf(tri, oh2.astype(BF16))
    base1 = pstart_ref[...] + carry_ref[...]
    d1 = jnp.sum(oh1 * (base1 + before1), axis=-1, keepdims=True)
    base2 = base1 + jnp.sum(oh1, axis=0, keepdims=True)
    d2 = jnp.sum(oh2 * (base2 + before2), axis=-1, keepdims=True)
    carry_ref[...] = base2 + jnp.sum(oh2, axis=0, keepdims=True) - pstart_ref[...]
    dest_ref[...] = jnp.where(lane == 0, d1, jnp.where(lane == 1, d2, 0.0))


def _rank(route, pstart, tri):
    t = route.shape[0]
    tm = tri.shape[0]
    return pl.pallas_call(
        _rank_kernel,
        out_shape=jax.ShapeDtypeStruct((t, LANES), F32),
        grid=(t // tm,),
        in_specs=[pl.BlockSpec((tm, LANES), lambda i: (i, 0)),
                  pl.BlockSpec((1, LANES), lambda i: (0, 0)),
                  pl.BlockSpec((tm, tm), lambda i: (0, 0))],
        out_specs=pl.BlockSpec((tm, LANES), lambda i: (i, 0)),
        scratch_shapes=[pltpu.VMEM((1, LANES), F32)],
        compiler_params=_cparams(("arbitrary",)),
        name="moe_rank",
    )(route, pstart, tri)


def _row_tile(ref, r):
    return ref.at[pl.ds(pl.multiple_of(r * ROW_TILES, ROW_TILES), ROW_TILES)]


def _pack_halves(x):
    half = x.shape[1] // 2
    hi = lax.bitcast_convert_type(x[:, :half].astype(BF16).astype(F32), jnp.uint32)
    lo = lax.bitcast_convert_type(x[:, half:].astype(BF16).astype(F32), jnp.uint32)
    return lax.bitcast_convert_type(hi | (lo >> 16), F32)


def _unpack_halves(p):
    u = lax.bitcast_convert_type(p, jnp.uint32)
    hi = lax.bitcast_convert_type(u & jnp.uint32(0xFFFF0000), F32)
    lo = lax.bitcast_convert_type(u << 16, F32)
    return jnp.concatenate([hi, lo], axis=1)


def _rows_from_tiles(ref, n):
    return jnp.concatenate([ref[pl.ds(s, n, stride=ROW_TILES), :] for s in range(ROW_TILES)], axis=1)


def _rows_to_tiles(ref, x):
    n = x.shape[0]
    for s in range(ROW_TILES):
        ref[pl.ds(s, n, stride=ROW_TILES), :] = x[:, s * LANES:(s + 1) * LANES]


def _dispatch_kernel(d1_ref, d2_ref, h_ref, init_ref, xs_ref, sem):
    del init_ref
    tm = h_ref.shape[0] // ROW_TILES
    base = pl.program_id(0) * tm

    def copies(r):
        src = _row_tile(h_ref, r)
        return (pltpu.make_async_copy(src, _row_tile(xs_ref, d1_ref[base + r]), sem),
                pltpu.make_async_copy(src, _row_tile(xs_ref, d2_ref[base + r]), sem))

    def issue(r, c):
        a, b = copies(r)
        a.start(priority=0)
        b.start(priority=1)
        return c

    def drain(r, c):
        a, b = copies(r)
        a.wait()
        b.wait()
        return c

    lax.fori_loop(0, tm, issue, 0, unroll=8)
    lax.fori_loop(0, tm, drain, 0, unroll=8)


def _dispatch(d1, d2, h2, init):
    t = h2.shape[0] // ROW_TILES
    tm = min(256, t)
    return pl.pallas_call(
        _dispatch_kernel,
        out_shape=jax.ShapeDtypeStruct(init.shape, h2.dtype),
        grid_spec=pltpu.PrefetchScalarGridSpec(
            num_scalar_prefetch=2,
            grid=(t // tm,),
            in_specs=[pl.BlockSpec((tm * ROW_TILES, LANES), lambda i, a, b: (i, 0)),
                      pl.BlockSpec(memory_space=pl.ANY)],
            out_specs=pl.BlockSpec(memory_space=pl.ANY),
            scratch_shapes=[pltpu.SemaphoreType.DMA(())]),
        input_output_aliases={3: 0},
        compiler_params=_cparams(("arbitrary",)),
        name="moe_dispatch",
    )(d1, d2, h2, init)


def _expert_kernel(be_ref, xs_ref, wgu_ref, wd_ref, ys_ref):
    del be_ref
    x = _unpack_halves(_rows_from_tiles(xs_ref, EXPERT_BLOCK)).astype(BF16)
    gu = _dotbf(x, wgu_ref[0])
    g = gu[:, :D_EXPERT]
    u = gu[:, D_EXPERT:]
    a = (g * jax.nn.sigmoid(g) * u).astype(BF16)
    _rows_to_tiles(ys_ref, _pack_halves(_dotbf(a, wd_ref[0])))


def _experts(blk_expert, xs, wgu, wd):
    cap = xs.shape[0] // ROW_TILES
    d = wgu.shape[1]
    blk = pl.BlockSpec((EXPERT_BLOCK * ROW_TILES, LANES), lambda i, be: (i, 0))
    return pl.pallas_call(
        _expert_kernel,
        out_shape=jax.ShapeDtypeStruct(xs.shape, F32),
        grid_spec=pltpu.PrefetchScalarGridSpec(
            num_scalar_prefetch=1,
            grid=(cap // EXPERT_BLOCK,),
            in_specs=[blk,
                      pl.BlockSpec((1, d, 2 * D_EXPERT), lambda i, be: (be[i], 0, 0)),
                      pl.BlockSpec((1, D_EXPERT, d), lambda i, be: (be[i], 0, 0))],
            out_specs=blk),
        compiler_params=_cparams(("arbitrary",)),
        name="moe_experts",
    )(blk_expert, xs, wgu, wd)


def _combine_kernel(d1_ref, d2_ref, ys_ref, x_ref, route_ref, gate_ref, o_ref, y1_ref, y2_ref, sem):
    tm = x_ref.shape[0]
    base = pl.program_id(0) * tm

    def copies(r):
        return (pltpu.make_async_copy(_row_tile(ys_ref, d1_ref[base + r]), _row_tile(y1_ref, r), sem),
                pltpu.make_async_copy(_row_tile(ys_ref, d2_ref[base + r]), _row_tile(y2_ref, r), sem))

    def issue(r, c):
        a, b = copies(r)
        a.start(priority=0)
        b.start(priority=1)
        return c

    def drain(r, c):
        a, b = copies(r)
        a.wait()
        b.wait()
        return c

    lax.fori_loop(0, tm, issue, 0, unroll=8)
    lax.fori_loop(0, tm, drain, 0, unroll=8)
    r = route_ref[...]
    ffn = (_unpack_halves(_rows_from_tiles(y1_ref, tm)) * r[:, 2:3]
           + _unpack_halves(_rows_from_tiles(y2_ref, tm)) * r[:, 3:4])
    o_ref[...] = x_ref[...] + gate_ref[0] * ffn


def _combine(d1, d2, ys, x1, route, gate2, nb, seq):
    t, d = x1.shape
    tm = min(256, seq)
    tps = seq // tm
    return pl.pallas_call(
        _combine_kernel,
        out_shape=jax.ShapeDtypeStruct((t, d), F32),
        grid_spec=pltpu.PrefetchScalarGridSpec(
            num_scalar_prefetch=2,
            grid=(t // tm,),
            in_specs=[pl.BlockSpec(memory_space=pl.ANY),
                      pl.BlockSpec((tm, d), lambda i, a, b: (i, 0)),
                      pl.BlockSpec((tm, LANES), lambda i, a, b: (i, 0)),
                      pl.BlockSpec((1, 1, d), lambda i, a, b: (i // tps, 0, 0))],
            out_specs=pl.BlockSpec((tm, d), lambda i, a, b: (i, 0)),
            scratch_shapes=[pltpu.VMEM((tm * ROW_TILES, LANES), F32), pltpu.VMEM((tm * ROW_TILES, LANES), F32),
                            pltpu.SemaphoreType.DMA(())]),
        compiler_params=_cparams(("arbitrary",)),
        name="moe_combine",
    )(d1, d2, ys, x1, route, gate2)


def _rope_tables(seq):
    rows = seq // GRID_W
    row = jnp.broadcast_to(jnp.arange(rows, dtype=F32)[:, None], (rows, GRID_W)).reshape(-1)
    col = jnp.broadcast_to(jnp.arange(GRID_W, dtype=F32)[None, :], (rows, GRID_W)).reshape(-1)
    inv_freq = ROPE_THETA ** (-jnp.arange(ROPE_PAIRS_PER_AXIS, dtype=F32) / ROPE_PAIRS_PER_AXIS)
    ang = jnp.concatenate([row[:, None] * inv_freq, col[:, None] * inv_freq], axis=-1)
    cos = jnp.repeat(jnp.cos(ang), 2, axis=-1)
    sin = jnp.repeat(jnp.sin(ang), 2, axis=-1)
    sign = jnp.tile(jnp.array([-1.0, 1.0], F32), HEAD_DIM // 2)
    return jnp.tile(cos, (1, N_HEADS)), jnp.tile(sin * sign, (1, N_HEADS))


def _filter_features(seq):
    t = jnp.linspace(0.0, 1.0, seq, dtype=F32)[:, None]
    w = (2.0 * math.pi / seq) * jnp.arange(seq, dtype=F32)[:, None]
    bands = jnp.linspace(1e-4, FILTER_BANDS - 1, FILTER_BANDS, dtype=F32)[None, :]
    z = jnp.concatenate([t, jnp.cos(bands * w), -jnp.sin(bands * w)], axis=-1)
    pad = jnp.zeros((seq, FILTER_ORDER - FILTER_EMB), F32)
    fwd = jnp.concatenate([z, pad], axis=-1).at[:, FILTER_EMB].set(1.0)
    rev = jnp.concatenate([z[:1], z[:0:-1]], axis=0)
    sign = jnp.concatenate([jnp.zeros((1,), F32), -jnp.ones((seq - 1,), F32)])
    bwd = jnp.concatenate([rev, pad], axis=-1).at[:, FILTER_EMB].set(sign)
    deltas = jnp.abs(jnp.linspace(MIN_DECAY, MAX_DECAY, D_HYENA, dtype=F32))[None, :]
    return jnp.concatenate([fwd, bwd], axis=-1), deltas


def _run_trunk(x, mod, p):
    nb, seq, d = x.shape
    t = nb * seq
    depth = mod.shape[0]
    cos, sin = _rope_tables(seq)
    zfeat, deltas = _filter_features(seq)
    tabs = _dft_tables(seq)
    h1 = tabs["h1"]
    tm_rank = min(512, t)
    tri = jnp.tril(jnp.ones((tm_rank, tm_rank), BF16), -1)
    cap = t * TOP_K + N_EXPERTS * EXPERT_BLOCK
    n_blk = cap // EXPERT_BLOCK
    lane_e = jnp.arange(LANES)

    xf = x.reshape(t, d)
    xs = jnp.zeros((cap * ROW_TILES, LANES), F32)
    for l in range(depth):
        m6 = mod[l].reshape(nb, 6, 1, d)
        shift1, scale1, gate1, shift2, scale2, gate2 = (m6[:, j] for j in range(6))

        q, k, vt, z, x2, gates = _inproj(xf, shift1, scale1, p["norm_mix"][l], p["w_in"][l], p["w_vt"][l],
                                         p["qg"][l], p["kg"][l], p["bd"], cos, sin,
                                         p["conv_w"][l], p["conv_b"][l], nb, seq)
        z = z.reshape(nb, seq, D_HYENA)
        x2 = x2.reshape(nb, seq, D_HYENA)
        bound = HEAD_DIM * jnp.max(jnp.abs(p["qg"][l])) * jnp.max(jnp.abs(p["kg"][l]))
        attn = lax.cond(bound <= SCORE_BOUND,
                        functools.partial(_attention, online=False),
                        functools.partial(_attention, online=True), q, k, vt).reshape(t, ATTN_WIDTH)

        kern = _filters(zfeat, p["filt_w1"][l], p["filt_b1"][l], p["filt_w2"][l], p["filt_b2"][l],
                        p["filt_w3"][l], p["filt_b3"][l], p["filt_freq"][l], p["filt_w4"][l], deltas)
        ka = _fft_outer(tabs["g_full"], kern.reshape(1, 2 * seq, D_HYENA), FFT_N2)
        kf = _fft_filter_inner(tabs["mf"], ka.reshape(1, 2, h1, FFT_N2, D_HYENA))
        za = _fft_outer(tabs["g_half"], z, FFT_N2)
        zb = _fft_inner(tabs["mf"], tabs["mi"], kf, za.reshape(nb, 2, h1, FFT_N2, D_HYENA))
        hy = _fft_final(tabs["g_out"], zb.reshape(nb, 2 * h1 * FFT_N2, D_HYENA), z, x2,
                        p["skip"][l], FFT_N2).reshape(t, D_HYENA)

        x1, h2, route, counts = _merge(attn, hy, gates, xf, gate1, shift2, scale2, p["norm_ffn"][l],
                                       p["w_br_attn"][l], p["w_br_hyena"][l], p["w_out"][l],
                                       p["wr_hi"][l], p["wr_lo"][l], p["b_route"][l], nb, seq)

        cnt = counts[0].astype(jnp.int32)
        padded = jnp.where(lane_e < N_EXPERTS, (cnt + EXPERT_BLOCK - 1) // EXPERT_BLOCK * EXPERT_BLOCK, 0)
        pad_end = jnp.cumsum(padded)
        pstart = (pad_end - padded).astype(F32)[None, :]
        blk_start = jnp.arange(n_blk, dtype=jnp.int32) * EXPERT_BLOCK
        blk_expert = jnp.minimum(
            jnp.sum(blk_start[:, None] >= pad_end[None, :N_EXPERTS], axis=1), N_EXPERTS - 1).astype(jnp.int32)

        dest = _rank(route, pstart, tri)
        d1 = dest[:, 0].astype(jnp.int32)
        d2 = dest[:, 1].astype(jnp.int32)
        xs = _dispatch(d1, d2, h2, xs)
        ys = _experts(blk_expert, xs, p["w_gu"][l], p["w_down"][l])
        xf = _combine(d1, d2, ys, x1, route, gate2, nb, seq)
    return xf.reshape(nb, seq, d)


def kernel(x_prompt, x_sample, c_prompt, c_sample, w_ada, b_ada, norm_mix, norm_ffn, w_in, q_gain, k_gain, conv_w, conv_b, filt_w1, filt_b1, filt_w2, filt_b2, filt_w3, filt_b3, filt_freq, filt_w4, hyena_skip, w_br_attn, w_br_hyena, w_out, w_group, b_group, w_router, b_router, w_e_gate, w_e_up, w_e_down):
    depth = w_ada.shape[0]
    bp, bs = c_prompt.shape[0], c_sample.shape[0]
    rows = -(-(bp + bs) // SUBLANES) * SUBLANES
    c_pad = jnp.zeros((rows, D_MODEL), F32).at[:bp].set(c_prompt).at[bp:bp + bs].set(c_sample)
    mod = _ada(c_pad, w_ada, b_ada)

    scale = HEAD_DIM ** -0.5 * math.log2(math.e)
    head_id = np.arange(ATTN_WIDTH) // HEAD_DIM
    route_w = jnp.concatenate([w_router, w_group], axis=-1)
    route_w = jnp.pad(route_w, ((0, 0), (0, 0), (0, LANES - route_w.shape[-1])))
    wr_hi = route_w.astype(BF16)

    def block_diag(a, b):
        return jnp.concatenate([jnp.pad(a, ((0, 0), (0, 0), (0, b.shape[2]))),
                                jnp.pad(b, ((0, 0), (0, 0), (a.shape[2], 0)))], axis=1)

    twice = lambda w: block_diag(w, w)
    p = dict(
        norm_mix=norm_mix.reshape(depth, 1, D_MODEL),
        norm_ffn=norm_ffn.reshape(depth, 1, D_MODEL),
        w_in=w_in.astype(BF16),
        w_vt=jnp.swapaxes(w_in[:, :, COL_V:COL_U], 1, 2).astype(BF16),
        qg=(jnp.tile(q_gain, (1, N_HEADS)) * scale).reshape(depth, 1, ATTN_WIDTH),
        kg=jnp.tile(k_gain, (1, N_KV_HEADS)).reshape(depth, 1, KV_WIDTH),
        bd=jnp.asarray(head_id[:, None] == head_id[None, :], dtype=BF16),
        conv_w=conv_w, conv_b=conv_b.reshape(depth, 1, 3 * D_HYENA),
        filt_w1=twice(jnp.pad(filt_w1, ((0, 0), (0, FILTER_ORDER - FILTER_EMB), (0, 0)))),
        filt_b1=jnp.tile(filt_b1, (1, 2)).reshape(depth, 1, 2 * FILTER_ORDER),
        filt_w2=twice(filt_w2), filt_b2=jnp.tile(filt_b2, (1, 2)).reshape(depth, 1, 2 * FILTER_ORDER),
        filt_w3=twice(filt_w3), filt_b3=jnp.tile(filt_b3, (1, 2)).reshape(depth, 1, 2 * FILTER_ORDER),
        filt_freq=jnp.tile(filt_freq, (1, 2)).reshape(depth, 1, 2 * FILTER_ORDER),
        filt_w4=block_diag(filt_w4[:, :, :D_HYENA], filt_w4[:, :, D_HYENA:]),
        skip=hyena_skip.reshape(depth, 1, D_HYENA),
        w_br_attn=w_br_attn.astype(BF16), w_br_hyena=w_br_hyena.astype(BF16), w_out=w_out.astype(BF16),
        wr_hi=wr_hi, wr_lo=(route_w - wr_hi.astype(F32)).astype(BF16),
        b_route=jnp.pad(jnp.concatenate([b_router, b_group], axis=-1),
                        ((0, 0), (0, LANES - N_EXPERTS - N_GROUPS))).reshape(depth, 1, LANES),
        w_gu=jnp.concatenate([w_e_gate, w_e_up], axis=-1).astype(BF16),
        w_down=w_e_down.astype(BF16),
    )
    y_prompt = _run_trunk(x_prompt, mod[:, :bp], p)
    y_sample = _run_trunk(x_sample, mod[:, bp:bp + bs], p)
    return (y_prompt, y_sample)
```

```python
import functools
import math

import numpy as np
import jax
import jax.numpy as jnp
from jax import lax
from jax.experimental import pallas as pl
from jax.experimental.pallas import tpu as pltpu

F32 = jnp.float32
BF16 = jnp.bfloat16

D_MODEL = 1024
GRID_W = 64
N_HEADS = 8
N_KV_HEADS = 4
HEAD_DIM = 64
Q_PER_KV = N_HEADS // N_KV_HEADS
ATTN_WIDTH = N_HEADS * HEAD_DIM
KV_WIDTH = N_KV_HEADS * HEAD_DIM
ROPE_THETA = 10000.0
ROPE_PAIRS_PER_AXIS = HEAD_DIM // 4
D_HYENA = D_MODEL // 2
FILTER_EMB = 33
FILTER_BANDS = (FILTER_EMB - 1) // 2
FILTER_ORDER = 64
DECAY_TARGET = 1e-2
MAX_DECAY = math.log(DECAY_TARGET) / 0.3
MIN_DECAY = math.log(DECAY_TARGET) / 1.5
MOD_SHIFT = 0.05
N_GROUPS = 4
EXPERTS_PER_GROUP = 8
N_EXPERTS = N_GROUPS * EXPERTS_PER_GROUP
TOP_K = 2
D_EXPERT = D_MODEL // 4
RMS_EPS = 1e-6
IN_COLS = ATTN_WIDTH + 2 * KV_WIDTH + 3 * D_HYENA + 2 * D_MODEL
COL_K = ATTN_WIDTH
COL_V = ATTN_WIDTH + KV_WIDTH
COL_U = ATTN_WIDTH + 2 * KV_WIDTH
COL_G = COL_U + 3 * D_HYENA

LANES = 128
SUBLANES = 8
ROW_TILES = D_MODEL // 2 // LANES
VMEM_LIMIT = 56 * 1024 * 1024

V_ROWS = HEAD_DIM + 16
SCORE_BOUND = 60.0
FFT_N2 = 128
FFT_PITCH = FFT_N2 + SUBLANES
EXPERT_BLOCK = 512
NEG_BIG = -1e30

_HI = lax.Precision.HIGHEST


def _dot32(a, b):
    return jnp.dot(a, b, precision=_HI, preferred_element_type=F32)


def _dotbf(a, b):
    return jnp.dot(a, b, preferred_element_type=F32)


def _cparams(sem):
    return pltpu.CompilerParams(dimension_semantics=sem, vmem_limit_bytes=VMEM_LIMIT)


def _ada_kernel(c_ref, w_ref, b_ref, o_ref):
    c = c_ref[...]
    act = c * jax.nn.sigmoid(c)
    o_ref[0] = _dot32(act, w_ref[0]) + b_ref[0]


def _ada(c_pad, w_ada, b_ada):
    depth, d, n = w_ada.shape
    tn = 1536
    return pl.pallas_call(
        _ada_kernel,
        out_shape=jax.ShapeDtypeStruct((depth, c_pad.shape[0], n), F32),
        grid=(depth, n // tn),
        in_specs=[pl.BlockSpec(c_pad.shape, lambda l, j: (0, 0)),
                  pl.BlockSpec((1, d, tn), lambda l, j: (l, 0, j)),
                  pl.BlockSpec((1, 1, tn), lambda l, j: (l, 0, j))],
        out_specs=pl.BlockSpec((1, c_pad.shape[0], tn), lambda l, j: (l, 0, j)),
        compiler_params=_cparams(("arbitrary", "arbitrary")),
        name="ada_mod",
    )(c_pad, w_ada, b_ada.reshape(depth, 1, n))


def _swap_pairs(x):
    n = x.shape[-1]
    lane = lax.broadcasted_iota(jnp.int32, x.shape, 1)
    nxt = pltpu.roll(x, n - 1, 1)
    prv = pltpu.roll(x, 1, 1)
    return jnp.where(lane % 2 == 0, nxt, prv)


def _head_norm_rope(p, gain, bd, cos, sin_signed):
    sq = (p * p).astype(BF16)
    ms = _dotbf(sq, bd) * (1.0 / HEAD_DIM)
    pn = p * lax.rsqrt(ms + RMS_EPS) * gain
    return pn * cos + _swap_pairs(pn) * sin_signed


def _store_groups(ref, x):
    pad = jnp.zeros((FFT_PITCH - FFT_N2, x.shape[1]), x.dtype)
    for g in range(ref.shape[0]):
        ref[g, 0:FFT_N2, :] = x[g * FFT_N2:(g + 1) * FFT_N2]
        ref[g, FFT_N2:FFT_PITCH, :] = pad


def _inproj_kernel(x_ref, xp_ref, xn_ref, shift_ref, scale_ref, gain_ref, w_ref, wvt_ref, qg_ref, kg_ref,
                   bd_ref, cos_ref, sin_ref, cw_ref, cb_ref, q_ref, k_ref, vt_ref, z_ref, x2_ref, g_ref, *, tps):
    def normed(x):
        ms = jnp.mean(x * x, axis=-1, keepdims=True)
        h = x * lax.rsqrt(ms + RMS_EPS) * gain_ref[...]
        return (h * (1.0 + scale_ref[0]) + shift_ref[0]).astype(BF16)

    hb = normed(x_ref[...])
    tm = hb.shape[0]
    cos = cos_ref[...]
    sin = sin_ref[...]
    bd = bd_ref[...]

    q = _dotbf(hb, w_ref[:, 0:COL_K])
    q = _head_norm_rope(q, qg_ref[...], bd, cos, sin)
    zpad = jnp.zeros((tm, LANES - HEAD_DIM), BF16)
    for hd in range(N_HEADS):
        q_ref[0, hd, :, 0:HEAD_DIM] = q[:, hd * HEAD_DIM:(hd + 1) * HEAD_DIM].astype(BF16)
        q_ref[0, hd, :, HEAD_DIM:LANES] = zpad

    k = _dotbf(hb, w_ref[:, COL_K:COL_V])
    k = _head_norm_rope(k, kg_ref[...], bd[:KV_WIDTH, :KV_WIDTH], cos[:, :KV_WIDTH], sin[:, :KV_WIDTH])
    vt = lax.dot_general(wvt_ref[...], hb, (((1,), (1,)), ((), ())), preferred_element_type=F32)
    ones = jnp.ones((V_ROWS - HEAD_DIM, vt.shape[1]), BF16)
    for hd in range(N_KV_HEADS):
        k_ref[0, hd, :, 0:HEAD_DIM] = k[:, hd * HEAD_DIM:(hd + 1) * HEAD_DIM].astype(BF16)
        k_ref[0, hd, :, HEAD_DIM:LANES] = zpad
        vt_ref[0, hd, 0:HEAD_DIM, :] = vt[hd * HEAD_DIM:(hd + 1) * HEAD_DIM].astype(BF16)
        vt_ref[0, hd, HEAD_DIM:V_ROWS, :] = ones

    g_ref[...] = jax.nn.sigmoid(_dotbf(hb, w_ref[:, COL_G:IN_COLS])).astype(BF16)

    halo = normed(jnp.concatenate([xp_ref[...], xn_ref[...]], axis=0))
    u_all = _dotbf(jnp.concatenate([hb, halo], axis=0), w_ref[:, COL_U:COL_G])
    u = u_all[:tm]
    pos = pl.program_id(0) % tps
    prev_row = u_all[tm + SUBLANES - 1:tm + SUBLANES] * (pos > 0).astype(F32)
    next_row = u_all[tm + SUBLANES:tm + SUBLANES + 1] * (pos < tps - 1).astype(F32)
    ridx = lax.broadcasted_iota(jnp.int32, u.shape, 0)
    u_prev = jnp.where(ridx == 0, prev_row, pltpu.roll(u, 1, 0))
    u_next = jnp.where(ridx == tm - 1, next_row, pltpu.roll(u, tm - 1, 0))
    w = cw_ref[...]
    c = cb_ref[...] + u_prev * w[0:1] + u * w[1:2] + u_next * w[2:3]
    _store_groups(z_ref.at[0], c[:, 2 * D_HYENA:3 * D_HYENA] * c[:, 0:D_HYENA])
    _store_groups(x2_ref.at[0], c[:, D_HYENA:2 * D_HYENA])


def _inproj(x, shift, scale, gain, w_in, w_vt, qg, kg, bd, cos, sin, conv_w, conv_b, nb, seq):
    t = nb * seq
    tm = min(512, seq)
    tps = seq // tm
    rpb = tm // SUBLANES
    row = lambda i: (i, 0)
    per_b = lambda i: (i // tps, 0, 0)
    const2 = lambda i: (0, 0)
    pos = lambda i: (i % tps, 0)
    head_out = lambda i: (i // tps, 0, i % tps, 0)
    return pl.pallas_call(
        functools.partial(_inproj_kernel, tps=tps),
        out_shape=(jax.ShapeDtypeStruct((nb, N_HEADS, seq, LANES), BF16),
                   jax.ShapeDtypeStruct((nb, N_KV_HEADS, seq, LANES), BF16),
                   jax.ShapeDtypeStruct((nb, N_KV_HEADS, V_ROWS, seq), BF16),
                   jax.ShapeDtypeStruct((nb, seq // FFT_N2, FFT_PITCH, D_HYENA), F32),
                   jax.ShapeDtypeStruct((nb, seq // FFT_N2, FFT_PITCH, D_HYENA), F32),
                   jax.ShapeDtypeStruct((t, 2 * D_MODEL), BF16)),
        grid=(t // tm,),
        in_specs=[pl.BlockSpec((tm, D_MODEL), row),
                  pl.BlockSpec((SUBLANES, D_MODEL), lambda i: (jnp.maximum(i * rpb - 1, 0), 0)),
                  pl.BlockSpec((SUBLANES, D_MODEL), lambda i: (jnp.minimum((i + 1) * rpb, t // SUBLANES - 1), 0)),
                  pl.BlockSpec((1, 1, D_MODEL), per_b),
                  pl.BlockSpec((1, 1, D_MODEL), per_b),
                  pl.BlockSpec((1, D_MODEL), const2),
                  pl.BlockSpec((D_MODEL, IN_COLS), const2),
                  pl.BlockSpec((KV_WIDTH, D_MODEL), const2),
                  pl.BlockSpec((1, ATTN_WIDTH), const2),
                  pl.BlockSpec((1, KV_WIDTH), const2),
                  pl.BlockSpec((ATTN_WIDTH, ATTN_WIDTH), const2),
                  pl.BlockSpec((tm, ATTN_WIDTH), pos),
                  pl.BlockSpec((tm, ATTN_WIDTH), pos),
                  pl.BlockSpec((3, 3 * D_HYENA), const2),
                  pl.BlockSpec((1, 3 * D_HYENA), const2)],
        out_specs=(pl.BlockSpec((1, N_HEADS, tm, LANES), head_out),
                   pl.BlockSpec((1, N_KV_HEADS, tm, LANES), head_out),
                   pl.BlockSpec((1, N_KV_HEADS, V_ROWS, tm), lambda i: (i // tps, 0, 0, i % tps)),
                   pl.BlockSpec((1, tm // FFT_N2, FFT_PITCH, D_HYENA), lambda i: (i // tps, i % tps, 0, 0)),
                   pl.BlockSpec((1, tm // FFT_N2, FFT_PITCH, D_HYENA), lambda i: (i // tps, i % tps, 0, 0)),
                   pl.BlockSpec((tm, 2 * D_MODEL), row)),
        compiler_params=_cparams(("arbitrary",)),
        name="in_proj",
    )(x, x, x, shift, scale, gain, w_in, w_vt, qg, kg, bd, cos, sin, conv_w, conv_b)


def _attn_kernel(q_ref, k_ref, vt_ref, o_ref, m_ref, acc_ref, *, tkc, online):
    seq = k_ref.shape[2]
    nchunk = seq // tkc
    acc_ref[...] = jnp.zeros(acc_ref.shape, F32)
    if online:
        m_ref[...] = jnp.full(m_ref.shape, NEG_BIG, F32)

    def body(c, carry):
        off = pl.multiple_of(c * tkc, tkc)
        kc = k_ref[0, 0, pl.ds(off, tkc), :]
        vc = vt_ref[0, 0, :, pl.ds(off, tkc)]
        for h in range(Q_PER_KV):
            st = lax.dot_general(kc, q_ref[0, h], (((1,), (1,)), ((), ())), preferred_element_type=F32)
            if online:
                m_prev = m_ref[h]
                m_new = jnp.maximum(m_prev, jnp.max(st, axis=0, keepdims=True))
                p = jnp.exp2(st - m_new).astype(BF16)
                acc_ref[h] = jnp.exp2(m_prev - m_new) * acc_ref[h] + _dotbf(vc, p)
                m_ref[h] = m_new
            else:
                acc_ref[h] += _dotbf(vc, jnp.exp2(st).astype(BF16))
        return carry

    lax.fori_loop(0, nchunk, body, 0, unroll=2 if nchunk % 2 == 0 else 1)
    outs = []
    for h in range(Q_PER_KV):
        a = acc_ref[h]
        outs.append((a[:HEAD_DIM] / a[HEAD_DIM:HEAD_DIM + 1]).T)
    o_ref[0] = jnp.concatenate(outs, axis=1).astype(BF16)


def _attention(q, k, vt, online):
    nb, _, seq, _ = q.shape
    tq = min(512 if online else 1024, seq)
    tkc = min(512 if online else 1024, seq)
    return pl.pallas_call(
        functools.partial(_attn_kernel, tkc=tkc, online=online),
        out_shape=jax.ShapeDtypeStruct((nb, seq, ATTN_WIDTH), BF16),
        grid=(nb, N_KV_HEADS, seq // tq),
        in_specs=[pl.BlockSpec((1, Q_PER_KV, tq, LANES), lambda b, g, i: (b, g, i, 0)),
                  pl.BlockSpec((1, 1, seq, LANES), lambda b, g, i: (b, g, 0, 0)),
                  pl.BlockSpec((1, 1, V_ROWS, seq), lambda b, g, i: (b, g, 0, 0))],
        out_specs=pl.BlockSpec((1, tq, Q_PER_KV * HEAD_DIM), lambda b, g, i: (b, i, g)),
        scratch_shapes=[pltpu.VMEM((Q_PER_KV, 1, tq), F32),
                        pltpu.VMEM((Q_PER_KV, V_ROWS, tq), F32)],
        compiler_params=_cparams(("arbitrary", "arbitrary", "arbitrary")),
        name="attn_online" if online else "attn_bounded",
    )(q, k, vt)


def _filter_kernel(z_ref, w1_ref, b1_ref, w2_ref, b2_ref, w3_ref, b3_ref, fr_ref, w4_ref,
                   dl_ref, o_ref):
    z = z_ref[...]
    fr = fr_ref[...]
    h = jnp.sin(fr * (_dot32(z, w1_ref[...]) + b1_ref[...]))
    h = jnp.sin(fr * (_dot32(h, w2_ref[...]) + b2_ref[...]))
    h = jnp.sin(fr * (_dot32(h, w3_ref[...]) + b3_ref[...]))
    h = _dot32(h, w4_ref[...])
    dl = dl_ref[...]
    for d in range(2):
        t = z[:, d * FILTER_ORDER:d * FILTER_ORDER + 1]
        sign = z[:, d * FILTER_ORDER + FILTER_EMB:d * FILTER_ORDER + FILTER_EMB + 1]
        _store_groups(o_ref.at[d], h[:, d * D_HYENA:(d + 1) * D_HYENA] * (jnp.exp(-t * dl) + MOD_SHIFT) * sign)


def _filters(zfeat, w1, b1, w2, b2, w3, b3, fr, w4, deltas):
    seq, fe = zfeat.shape
    tm = min(512, seq)
    c2 = lambda i: (0, 0)
    return pl.pallas_call(
        _filter_kernel,
        out_shape=jax.ShapeDtypeStruct((2, seq // FFT_N2, FFT_PITCH, D_HYENA), F32),
        grid=(seq // tm,),
        in_specs=[pl.BlockSpec((tm, fe), lambda i: (i, 0)),
                  pl.BlockSpec(w1.shape, c2), pl.BlockSpec(b1.shape, c2),
                  pl.BlockSpec(w2.shape, c2), pl.BlockSpec(b2.shape, c2),
                  pl.BlockSpec(w3.shape, c2), pl.BlockSpec(b3.shape, c2),
                  pl.BlockSpec(fr.shape, c2), pl.BlockSpec(w4.shape, c2),
                  pl.BlockSpec(deltas.shape, c2)],
        out_specs=pl.BlockSpec((2, tm // FFT_N2, FFT_PITCH, D_HYENA), lambda i: (0, i, 0, 0)),
        compiler_params=_cparams(("arbitrary",)),
        name="hyena_filters",
    )(zfeat, w1, b1, w2, b2, w3, b3, fr, w4, deltas)


def _zero_pad_rows(o_ref, groups):
    zeros = jnp.zeros((groups, o_ref.shape[1]), o_ref.dtype)
    for s in range(FFT_N2, FFT_PITCH):
        o_ref[pl.ds(s, groups, stride=FFT_PITCH), :] = zeros


def _fft_outer_kernel(g_ref, x_ref, o_ref):
    g = g_ref[...]
    m, kk = g.shape

    def body(jp, c):
        j = 2 * jp
        xj = jnp.concatenate([x_ref[pl.ds(j, kk, stride=FFT_PITCH), :],
                              x_ref[pl.ds(j + 1, kk, stride=FFT_PITCH), :]], axis=1)
        r = _dotbf(g, xj.astype(BF16))
        o_ref[pl.ds(j, m, stride=FFT_PITCH), :] = r[:, :LANES]
        o_ref[pl.ds(j + 1, m, stride=FFT_PITCH), :] = r[:, LANES:]
        return c

    lax.fori_loop(0, FFT_N2 // 2, body, 0, unroll=4)
    _zero_pad_rows(o_ref, m)


def _fft_outer(gmat, x):
    nb, rows, ch = x.shape
    m, kk = gmat.shape
    return pl.pallas_call(
        _fft_outer_kernel,
        out_shape=jax.ShapeDtypeStruct((nb, m * FFT_PITCH, ch), F32),
        grid=(nb, ch // LANES),
        in_specs=[pl.BlockSpec((m, kk), lambda b, c: (0, 0)),
                  pl.BlockSpec((None, rows, LANES), lambda b, c: (b, 0, c))],
        out_specs=pl.BlockSpec((None, m * FFT_PITCH, LANES), lambda b, c: (b, 0, c)),
        compiler_params=_cparams(("arbitrary", "arbitrary")),
        name="fft_outer",
    )(gmat, x)


FFT_K1_PER_STEP = 4


def _fft_filter_inner_kernel(mf_ref, a_ref, o_ref):
    n2, ch = FFT_N2, a_ref.shape[4]
    for r in range(a_ref.shape[2]):
        a = a_ref[0, :, r, 0:n2, :].reshape(2 * n2, ch)
        o_ref[r] = _dotbf(mf_ref[r], a.astype(BF16)).reshape(2, n2, ch)


def _fft_filter_inner(mf, a5):
    _, _, h1, pitch, ch = a5.shape
    n2 = FFT_N2
    kr = FFT_K1_PER_STEP
    return pl.pallas_call(
        _fft_filter_inner_kernel,
        out_shape=jax.ShapeDtypeStruct((h1, 2, n2, ch), F32),
        grid=(h1 // kr,),
        in_specs=[pl.BlockSpec((kr, 2 * n2, 2 * n2), lambda k: (k, 0, 0)),
                  pl.BlockSpec((1, 2, kr, pitch, ch), lambda k: (0, 0, k, 0, 0))],
        out_specs=pl.BlockSpec((kr, 2, n2, ch), lambda k: (k, 0, 0, 0)),
        compiler_params=_cparams(("arbitrary",)),
        name="fft_filter_inner",
    )(mf, a5)


def _fft_inner_kernel(mf_ref, mi_ref, kf_ref, a_ref, o_ref):
    n2, pitch, ch = FFT_N2, a_ref.shape[3], a_ref.shape[4]
    for r in range(a_ref.shape[2]):
        a = a_ref[0, :, r, 0:n2, :].reshape(2 * n2, ch)
        xs = _dotbf(mf_ref[r], a.astype(BF16))
        xr, xi = xs[:n2], xs[n2:]
        kr, ki = kf_ref[r, 0], kf_ref[r, 1]
        p = jnp.concatenate([xr * kr - xi * ki, xr * ki + xi * kr], axis=0)
        o_ref[0, :, r, 0:n2, :] = _dotbf(mi_ref[r], p.astype(BF16)).reshape(2, n2, ch)
        o_ref[0, :, r, n2:pitch, :] = jnp.zeros((2, pitch - n2, ch), F32)


def _fft_inner(mf, mi, kf, a5):
    nb, _, h1, pitch, ch = a5.shape
    n2 = FFT_N2
    kr = FFT_K1_PER_STEP
    return pl.pallas_call(
        _fft_inner_kernel,
        out_shape=jax.ShapeDtypeStruct(a5.shape, F32),
        grid=(h1 // kr, nb),
        in_specs=[pl.BlockSpec((kr, 2 * n2, 2 * n2), lambda k, b: (k, 0, 0)),
                  pl.BlockSpec((kr, 2 * n2, 2 * n2), lambda k, b: (k, 0, 0)),
                  pl.BlockSpec((kr, 2, n2, ch), lambda k, b: (k, 0, 0, 0)),
                  pl.BlockSpec((1, 2, kr, pitch, ch), lambda k, b: (b, 0, k, 0, 0))],
        out_specs=pl.BlockSpec((1, 2, kr, pitch, ch), lambda k, b: (b, 0, k, 0, 0)),
        compiler_params=_cparams(("arbitrary", "arbitrary")),
        name="fft_inner",
    )(mf, mi, kf, a5)


def _fft_final_kernel(g_ref, b_ref, z_ref, x2_ref, skip_ref, o_ref):
    g = g_ref[...]
    h1, m = g.shape
    skip = skip_ref[...]

    def body(jp, c):
        j = 2 * jp
        bj = jnp.concatenate([b_ref[pl.ds(j, m, stride=FFT_PITCH), :],
                              b_ref[pl.ds(j + 1, m, stride=FFT_PITCH), :]], axis=1)
        y = _dotbf(g, bj.astype(BF16))
        for d in range(2):
            zj = z_ref[pl.ds(j + d, h1, stride=FFT_PITCH), :]
            xj = x2_ref[pl.ds(j + d, h1, stride=FFT_PITCH), :]
            o_ref[pl.ds(j + d, h1, stride=FFT_PITCH), :] = xj * (y[:, d * LANES:(d + 1) * LANES] + zj * skip)
        return c

    lax.fori_loop(0, FFT_N2 // 2, body, 0, unroll=4)
    _zero_pad_rows(o_ref, h1)


def _fft_final(gc, bm, z, x2, skip):
    nb, rows, ch = z.shape
    h1, m = gc.shape
    blk = lambda r: pl.BlockSpec((None, r, LANES), lambda b, c: (b, 0, c))
    return pl.pallas_call(
        _fft_final_kernel,
        out_shape=jax.ShapeDtypeStruct((nb, rows, ch), F32),
        grid=(nb, ch // LANES),
        in_specs=[pl.BlockSpec((h1, m), lambda b, c: (0, 0)), blk(m * FFT_PITCH), blk(rows), blk(rows),
                  pl.BlockSpec((1, LANES), lambda b, c: (0, c))],
        out_specs=blk(rows),
        compiler_params=_cparams(("arbitrary", "arbitrary")),
        name="fft_final",
    )(gc, bm, z, x2, skip)


def _dft_tables(seq):
    n = 2 * seq
    n2 = FFT_N2
    n1 = n // n2
    h1 = n1 // 2
    i32 = jnp.int32
    k1 = jnp.arange(h1, dtype=i32)[:, None]
    a1 = jnp.arange(n1, dtype=i32)[None, :]
    th = (math.pi / n1) * ((a1 * (2 * k1 + 1)) % (2 * n1)).astype(F32)
    g_re, g_im = jnp.cos(th), -jnp.sin(th)
    g_full = jnp.concatenate([g_re, g_im], axis=0)
    g_half = g_full[:, :h1]
    g_out = (2.0 / n) * jnp.concatenate([g_re[:, :h1].T, g_im[:, :h1].T], axis=1)
    k2 = jnp.arange(n2, dtype=i32)[None, :, None]
    b2 = jnp.arange(n2, dtype=i32)[None, None, :]
    kk = jnp.arange(h1, dtype=i32)[:, None, None]
    ph = (math.pi / n) * ((b2 * (2 * kk + 1 + 2 * n1 * k2)) % (2 * n)).astype(F32)
    m_re, m_im = jnp.cos(ph), -jnp.sin(ph)
    mf = jnp.concatenate([jnp.concatenate([m_re, -m_im], axis=2),
                          jnp.concatenate([m_im, m_re], axis=2)], axis=1)
    mt_re, mt_im = jnp.transpose(m_re, (0, 2, 1)), jnp.transpose(m_im, (0, 2, 1))
    mi = jnp.concatenate([jnp.concatenate([mt_re, mt_im], axis=2),
                          jnp.concatenate([-mt_im, mt_re], axis=2)], axis=1)
    b16 = lambda a: a.astype(BF16)
    return dict(n1=n1, h1=h1, g_full=b16(g_full), g_half=b16(g_half), g_out=b16(g_out), mf=b16(mf), mi=b16(mi))


def _merge_kernel(attn_ref, hy_ref, g_ref, x_ref, gate_ref, shift_ref, scale_ref, gain_ref,
                  wa_ref, wh_ref, wo_ref, wrh_ref, wrl_ref, br_ref,
                  x1_ref, h2_ref, route_ref, cnt_ref):
    g = g_ref[...].astype(F32)
    hy = jnp.concatenate([hy_ref[grp, 0:FFT_N2, :] for grp in range(hy_ref.shape[0])], axis=0)
    merged = (g[:, :D_MODEL] * _dotbf(attn_ref[...], wa_ref[...])
              + g[:, D_MODEL:] * _dotbf(hy.astype(BF16), wh_ref[...]))
    mix = _dotbf(merged.astype(BF16), wo_ref[...])
    x1 = x_ref[...] + gate_ref[0] * mix
    x1_ref[...] = x1

    ms = jnp.mean(x1 * x1, axis=-1, keepdims=True)
    h2 = x1 * lax.rsqrt(ms + RMS_EPS) * gain_ref[...]
    h2 = h2 * (1.0 + scale_ref[0]) + shift_ref[0]
    _rows_to_tiles(h2_ref, _pack_halves(h2))

    hi = h2.astype(BF16)
    lo = (h2 - hi.astype(F32)).astype(BF16)
    wrh = wrh_ref[...]
    lg = _dotbf(hi, wrh) + _dotbf(lo, wrh) + _dotbf(hi, wrl_ref[...]) + br_ref[...]

    lane = lax.broadcasted_iota(jnp.int32, lg.shape, 1).astype(F32)
    is_grp = jnp.logical_and(lane >= N_EXPERTS, lane < N_EXPERTS + N_GROUPS)
    gm = jnp.where(is_grp, lg, NEG_BIG)
    gmax = jnp.max(gm, axis=-1, keepdims=True)
    gidx = jnp.min(jnp.where(gm == gmax, lane, 1e9), axis=-1, keepdims=True) - N_EXPERTS
    p_group = 1.0 / jnp.sum(jnp.where(is_grp, jnp.exp(gm - gmax), 0.0), axis=-1, keepdims=True)
    lo_lane = gidx * EXPERTS_PER_GROUP
    in_grp = jnp.logical_and(lane >= lo_lane, lane < lo_lane + EXPERTS_PER_GROUP)
    e1v = jnp.where(in_grp, lg, NEG_BIG)
    t1 = jnp.max(e1v, axis=-1, keepdims=True)
    i1 = jnp.min(jnp.where(e1v == t1, lane, 1e9), axis=-1, keepdims=True)
    e2v = jnp.where(lane == i1, NEG_BIG, e1v)
    t2 = jnp.max(e2v, axis=-1, keepdims=True)
    i2 = jnp.min(jnp.where(e2v == t2, lane, 1e9), axis=-1, keepdims=True)
    d = jnp.exp(t2 - t1)
    w1 = p_group / (1.0 + d)
    w2 = p_group * d / (1.0 + d)
    route_ref[...] = jnp.where(lane == 0, i1, jnp.where(lane == 1, i2,
                               jnp.where(lane == 2, w1, jnp.where(lane == 3, w2, 0.0))))

    onehot = (lane == i1).astype(F32) + (lane == i2).astype(F32)

    @pl.when(pl.program_id(0) == 0)
    def _():
        cnt_ref[...] = jnp.zeros(cnt_ref.shape, F32)

    cnt_ref[...] += jnp.sum(onehot, axis=0, keepdims=True)


def _merge(attn, hy, gates, x, gate1, shift2, scale2, gain, wa, wh, wo, wrh, wrl, br, nb, seq):
    t = nb * seq
    tm = min(512, seq)
    tps = seq // tm
    row = lambda i: (i, 0)
    per_b = lambda i: (i // tps, 0, 0)
    c2 = lambda i: (0, 0)
    return pl.pallas_call(
        _merge_kernel,
        out_shape=(jax.ShapeDtypeStruct((t, D_MODEL), F32),
                   jax.ShapeDtypeStruct((t * ROW_TILES, LANES), F32),
                   jax.ShapeDtypeStruct((t, LANES), F32),
                   jax.ShapeDtypeStruct((1, LANES), F32)),
        grid=(t // tm,),
        in_specs=[pl.BlockSpec((tm, ATTN_WIDTH), row),
                  pl.BlockSpec((tm // FFT_N2, FFT_PITCH, D_HYENA), lambda i: (i, 0, 0)),
                  pl.BlockSpec((tm, 2 * D_MODEL), row),
                  pl.BlockSpec((tm, D_MODEL), row),
                  pl.BlockSpec((1, 1, D_MODEL), per_b),
                  pl.BlockSpec((1, 1, D_MODEL), per_b),
                  pl.BlockSpec((1, 1, D_MODEL), per_b),
                  pl.BlockSpec((1, D_MODEL), c2),
                  pl.BlockSpec(wa.shape, c2), pl.BlockSpec(wh.shape, c2), pl.BlockSpec(wo.shape, c2),
                  pl.BlockSpec(wrh.shape, c2), pl.BlockSpec(wrl.shape, c2), pl.BlockSpec(br.shape, c2)],
        out_specs=(pl.BlockSpec((tm, D_MODEL), row),
                   pl.BlockSpec((tm * ROW_TILES, LANES), row),
                   pl.BlockSpec((tm, LANES), row),
                   pl.BlockSpec((1, LANES), c2)),
        compiler_params=_cparams(("arbitrary",)),
        name="merge_router",
    )(attn, hy, gates, x, gate1, shift2, scale2, gain, wa, wh, wo, wrh, wrl, br)


def _rank_kernel(route_ref, pstart_ref, tri_ref, dest_ref, carry_ref):
    @pl.when(pl.program_id(0) == 0)
    def _():
        carry_ref[...] = jnp.zeros(carry_ref.shape, F32)

    r = route_ref[...]
    lane = lax.broadcasted_iota(jnp.int32, r.shape, 1).astype(F32)
    oh1 = (lane == r[:, 0:1]).astype(F32)
    oh2 = (lane == r[:, 1:2]).astype(F32)
    tri = tri_ref[...]
    before1 = _dotbf(tri, oh1.astype(BF16))
    before2 = _dotbf(tri, oh2.astype(BF16))
    base1 = pstart_ref[...] + carry_ref[...]
    d1 = jnp.sum(oh1 * (base1 + before1), axis=-1, keepdims=True)
    base2 = base1 + jnp.sum(oh1, axis=0, keepdims=True)
    d2 = jnp.sum(oh2 * (base2 + before2), axis=-1, keepdims=True)
    carry_ref[...] = base2 + jnp.sum(oh2, axis=0, keepdims=True) - pstart_ref[...]
    dest_ref[...] = jnp.where(lane == 0, d1, jnp.where(lane == 1, d2, 0.0))


def _rank(route, pstart, tri):
    t = route.shape[0]
    tm = tri.shape[0]
    return pl.pallas_call(
        _rank_kernel,
        out_shape=jax.ShapeDtypeStruct((t, LANES), F32),
        grid=(t // tm,),
        in_specs=[pl.BlockSpec((tm, LANES), lambda i: (i, 0)),
                  pl.BlockSpec((1, LANES), lambda i: (0, 0)),
                  pl.BlockSpec((tm, tm), lambda i: (0, 0))],
        out_specs=pl.BlockSpec((tm, LANES), lambda i: (i, 0)),
        scratch_shapes=[pltpu.VMEM((1, LANES), F32)],
        compiler_params=_cparams(("arbitrary",)),
        name="moe_rank",
    )(route, pstart, tri)


def _row_tile(ref, r):
    return ref.at[pl.ds(pl.multiple_of(r * ROW_TILES, ROW_TILES), ROW_TILES)]


def _pack_halves(x):
    half = x.shape[1] // 2
    hi = lax.bitcast_convert_type(x[:, :half].astype(BF16).astype(F32), jnp.uint32)
    lo = lax.bitcast_convert_type(x[:, half:].astype(BF16).astype(F32), jnp.uint32)
    return lax.bitcast_convert_type(hi | (lo >> 16), F32)


def _unpack_halves(p):
    u = lax.bitcast_convert_type(p, jnp.uint32)
    hi = lax.bitcast_convert_type(u & jnp.uint32(0xFFFF0000), F32)
    lo = lax.bitcast_convert_type(u << 16, F32)
    return jnp.concatenate([hi, lo], axis=1)


def _rows_from_tiles(ref, n):
    return jnp.concatenate([ref[pl.ds(s, n, stride=ROW_TILES), :] for s in range(ROW_TILES)], axis=1)


def _rows_to_tiles(ref, x):
    n = x.shape[0]
    for s in range(ROW_TILES):
        ref[pl.ds(s, n, stride=ROW_TILES), :] = x[:, s * LANES:(s + 1) * LANES]


def _dispatch_kernel(d1_ref, d2_ref, h_ref, init_ref, xs_ref, sem):
    del init_ref
    tm = h_ref.shape[0] // ROW_TILES
    base = pl.program_id(0) * tm

    def copies(r):
        src = _row_tile(h_ref, r)
        return (pltpu.make_async_copy(src, _row_tile(xs_ref, d1_ref[base + r]), sem),
                pltpu.make_async_copy(src, _row_tile(xs_ref, d2_ref[base + r]), sem))

    def issue(r, c):
        a, b = copies(r)
        a.start(priority=0)
        b.start(priority=1)
        return c

    def drain(r, c):
        a, b = copies(r)
        a.wait()
        b.wait()
        return c

    lax.fori_loop(0, tm, issue, 0, unroll=8)
    lax.fori_loop(0, tm, drain, 0, unroll=8)


def _dispatch(d1, d2, h2, init):
    t = h2.shape[0] // ROW_TILES
    tm = min(256, t)
    return pl.pallas_call(
        _dispatch_kernel,
        out_shape=jax.ShapeDtypeStruct(init.shape, h2.dtype),
        grid_spec=pltpu.PrefetchScalarGridSpec(
            num_scalar_prefetch=2,
            grid=(t // tm,),
            in_specs=[pl.BlockSpec((tm * ROW_TILES, LANES), lambda i, a, b: (i, 0)),
                      pl.BlockSpec(memory_space=pl.ANY)],
            out_specs=pl.BlockSpec(memory_space=pl.ANY),
            scratch_shapes=[pltpu.SemaphoreType.DMA(())]),
        input_output_aliases={3: 0},
        compiler_params=_cparams(("arbitrary",)),
        name="moe_dispatch",
    )(d1, d2, h2, init)


def _expert_kernel(be_ref, xs_ref, wgu_ref, wd_ref, ys_ref):
    del be_ref
    x = _unpack_halves(_rows_from_tiles(xs_ref, EXPERT_BLOCK)).astype(BF16)
    gu = _dotbf(x, wgu_ref[0])
    g = gu[:, :D_EXPERT]
    u = gu[:, D_EXPERT:]
    a = (g * jax.nn.sigmoid(g) * u).astype(BF16)
    _rows_to_tiles(ys_ref, _pack_halves(_dotbf(a, wd_ref[0])))


def _experts(blk_expert, xs, wgu, wd):
    cap = xs.shape[0] // ROW_TILES
    d = wgu.shape[1]
    blk = pl.BlockSpec((EXPERT_BLOCK * ROW_TILES, LANES), lambda i, be: (i, 0))
    return pl.pallas_call(
        _expert_kernel,
        out_shape=jax.ShapeDtypeStruct(xs.shape, F32),
        grid_spec=pltpu.PrefetchScalarGridSpec(
            num_scalar_prefetch=1,
            grid=(cap // EXPERT_BLOCK,),
            in_specs=[blk,
                      pl.BlockSpec((1, d, 2 * D_EXPERT), lambda i, be: (be[i], 0, 0)),
                      pl.BlockSpec((1, D_EXPERT, d), lambda i, be: (be[i], 0, 0))],
            out_specs=blk),
        compiler_params=_cparams(("arbitrary",)),
        name="moe_experts",
    )(blk_expert, xs, wgu, wd)


def _combine_kernel(d1_ref, d2_ref, ys_ref, x_ref, route_ref, gate_ref, o_ref, y1_ref, y2_ref, sem):
    tm = x_ref.shape[0]
    base = pl.program_id(0) * tm

    def copies(r):
        return (pltpu.make_async_copy(_row_tile(ys_ref, d1_ref[base + r]), _row_tile(y1_ref, r), sem),
                pltpu.make_async_copy(_row_tile(ys_ref, d2_ref[base + r]), _row_tile(y2_ref, r), sem))

    def issue(r, c):
        a, b = copies(r)
        a.start(priority=0)
        b.start(priority=1)
        return c

    def drain(r, c):
        a, b = copies(r)
        a.wait()
        b.wait()
        return c

    lax.fori_loop(0, tm, issue, 0, unroll=8)
    lax.fori_loop(0, tm, drain, 0, unroll=8)
    r = route_ref[...]
    ffn = (_unpack_halves(_rows_from_tiles(y1_ref, tm)) * r[:, 2:3]
           + _unpack_halves(_rows_from_tiles(y2_ref, tm)) * r[:, 3:4])
    o_ref[...] = x_ref[...] + gate_ref[0] * ffn


def _combine(d1, d2, ys, x1, route, gate2, nb, seq):
    t, d = x1.shape
    tm = min(256, seq)
    tps = seq // tm
    return pl.pallas_call(
        _combine_kernel,
        out_shape=jax.ShapeDtypeStruct((t, d), F32),
        grid_spec=pltpu.PrefetchScalarGridSpec(
            num_scalar_prefetch=2,
            grid=(t // tm,),
            in_specs=[pl.BlockSpec(memory_space=pl.ANY),
                      pl.BlockSpec((tm, d), lambda i, a, b: (i, 0)),
                      pl.BlockSpec((tm, LANES), lambda i, a, b: (i, 0)),
                      pl.BlockSpec((1, 1, d), lambda i, a, b: (i // tps, 0, 0))],
            out_specs=pl.BlockSpec((tm, d), lambda i, a, b: (i, 0)),
            scratch_shapes=[pltpu.VMEM((tm * ROW_TILES, LANES), F32), pltpu.VMEM((tm * ROW_TILES, LANES), F32),
                            pltpu.SemaphoreType.DMA(())]),
        compiler_params=_cparams(("arbitrary",)),
        name="moe_combine",
    )(d1, d2, ys, x1, route, gate2)


def _rope_tables(seq):
    rows = seq // GRID_W
    row = jnp.broadcast_to(jnp.arange(rows, dtype=F32)[:, None], (rows, GRID_W)).reshape(-1)
    col = jnp.broadcast_to(jnp.arange(GRID_W, dtype=F32)[None, :], (rows, GRID_W)).reshape(-1)
    inv_freq = ROPE_THETA ** (-jnp.arange(ROPE_PAIRS_PER_AXIS, dtype=F32) / ROPE_PAIRS_PER_AXIS)
    ang = jnp.concatenate([row[:, None] * inv_freq, col[:, None] * inv_freq], axis=-1)
    cos = jnp.repeat(jnp.cos(ang), 2, axis=-1)
    sin = jnp.repeat(jnp.sin(ang), 2, axis=-1)
    sign = jnp.tile(jnp.array([-1.0, 1.0], F32), HEAD_DIM // 2)
    return jnp.tile(cos, (1, N_HEADS)), jnp.tile(sin * sign, (1, N_HEADS))


def _filter_features(seq):
    t = jnp.linspace(0.0, 1.0, seq, dtype=F32)[:, None]
    w = (2.0 * math.pi / seq) * jnp.arange(seq, dtype=F32)[:, None]
    bands = jnp.linspace(1e-4, FILTER_BANDS - 1, FILTER_BANDS, dtype=F32)[None, :]
    z = jnp.concatenate([t, jnp.cos(bands * w), -jnp.sin(bands * w)], axis=-1)
    pad = jnp.zeros((seq, FILTER_ORDER - FILTER_EMB), F32)
    fwd = jnp.concatenate([z, pad], axis=-1).at[:, FILTER_EMB].set(1.0)
    rev = jnp.concatenate([z[:1], z[:0:-1]], axis=0)
    sign = jnp.concatenate([jnp.zeros((1,), F32), -jnp.ones((seq - 1,), F32)])
    bwd = jnp.concatenate([rev, pad], axis=-1).at[:, FILTER_EMB].set(sign)
    deltas = jnp.abs(jnp.linspace(MIN_DECAY, MAX_DECAY, D_HYENA, dtype=F32))[None, :]
    return jnp.concatenate([fwd, bwd], axis=-1), deltas


def _run_trunk(x, mod, p):
    nb, seq, d = x.shape
    t = nb * seq
    depth = mod.shape[0]
    cos, sin = _rope_tables(seq)
    zfeat, deltas = _filter_features(seq)
    tabs = _dft_tables(seq)
    h1 = tabs["h1"]
    tm_rank = min(512, t)
    tri = jnp.tril(jnp.ones((tm_rank, tm_rank), BF16), -1)
    cap = t * TOP_K + N_EXPERTS * EXPERT_BLOCK
    n_blk = cap // EXPERT_BLOCK
    lane_e = jnp.arange(LANES)

    xf = x.reshape(t, d)
    xs = jnp.zeros((cap * ROW_TILES, LANES), F32)
    for l in range(depth):
        m6 = mod[l].reshape(nb, 6, 1, d)
        shift1, scale1, gate1, shift2, scale2, gate2 = (m6[:, j] for j in range(6))

        q, k, vt, z, x2, gates = _inproj(xf, shift1, scale1, p["norm_mix"][l], p["w_in"][l], p["w_vt"][l],
                                         p["qg"][l], p["kg"][l], p["bd"], cos, sin,
                                         p["conv_w"][l], p["conv_b"][l], nb, seq)
        z = z.reshape(nb, h1 * FFT_PITCH, D_HYENA)
        x2 = x2.reshape(nb, h1 * FFT_PITCH, D_HYENA)
        bound = HEAD_DIM * jnp.max(jnp.abs(p["qg"][l])) * jnp.max(jnp.abs(p["kg"][l]))
        attn = lax.cond(bound <= SCORE_BOUND,
                        functools.partial(_attention, online=False),
                        functools.partial(_attention, online=True), q, k, vt).reshape(t, ATTN_WIDTH)

        kern = _filters(zfeat, p["filt_w1"][l], p["filt_b1"][l], p["filt_w2"][l], p["filt_b2"][l],
                        p["filt_w3"][l], p["filt_b3"][l], p["filt_freq"][l], p["filt_w4"][l], deltas)
        ka = _fft_outer(tabs["g_full"], kern.reshape(1, 2 * h1 * FFT_PITCH, D_HYENA))
        kf = _fft_filter_inner(tabs["mf"], ka.reshape(1, 2, h1, FFT_PITCH, D_HYENA))
        za = _fft_outer(tabs["g_half"], z)
        zb = _fft_inner(tabs["mf"], tabs["mi"], kf, za.reshape(nb, 2, h1, FFT_PITCH, D_HYENA))
        hy = _fft_final(tabs["g_out"], zb.reshape(nb, 2 * h1 * FFT_PITCH, D_HYENA), z, x2,
                        p["skip"][l]).reshape(nb * h1, FFT_PITCH, D_HYENA)

        x1, h2, route, counts = _merge(attn, hy, gates, xf, gate1, shift2, scale2, p["norm_ffn"][l],
                                       p["w_br_attn"][l], p["w_br_hyena"][l], p["w_out"][l],
                                       p["wr_hi"][l], p["wr_lo"][l], p["b_route"][l], nb, seq)

        cnt = counts[0].astype(jnp.int32)
        padded = jnp.where(lane_e < N_EXPERTS, (cnt + EXPERT_BLOCK - 1) // EXPERT_BLOCK * EXPERT_BLOCK, 0)
        pad_end = jnp.cumsum(padded)
        pstart = (pad_end - padded).astype(F32)[None, :]
        blk_start = jnp.arange(n_blk, dtype=jnp.int32) * EXPERT_BLOCK
        blk_expert = jnp.minimum(
            jnp.sum(blk_start[:, None] >= pad_end[None, :N_EXPERTS], axis=1), N_EXPERTS - 1).astype(jnp.int32)

        dest = _rank(route, pstart, tri)
        d1 = dest[:, 0].astype(jnp.int32)
        d2 = dest[:, 1].astype(jnp.int32)
        xs = _dispatch(d1, d2, h2, xs)
        ys = _experts(blk_expert, xs, p["w_gu"][l], p["w_down"][l])
        xf = _combine(d1, d2, ys, x1, route, gate2, nb, seq)
    return xf.reshape(nb, seq, d)


def kernel(x_prompt, x_sample, c_prompt, c_sample, w_ada, b_ada, norm_mix, norm_ffn, w_in, q_gain, k_gain, conv_w, conv_b, filt_w1, filt_b1, filt_w2, filt_b2, filt_w3, filt_b3, filt_freq, filt_w4, hyena_skip, w_br_attn, w_br_hyena, w_out, w_group, b_group, w_router, b_router, w_e_gate, w_e_up, w_e_down):
    depth = w_ada.shape[0]
    bp, bs = c_prompt.shape[0], c_sample.shape[0]
    rows = -(-(bp + bs) // SUBLANES) * SUBLANES
    c_pad = jnp.zeros((rows, D_MODEL), F32).at[:bp].set(c_prompt).at[bp:bp + bs].set(c_sample)
    mod = _ada(c_pad, w_ada, b_ada)

    scale = HEAD_DIM ** -0.5 * math.log2(math.e)
    head_id = np.arange(ATTN_WIDTH) // HEAD_DIM
    route_w = jnp.concatenate([w_router, w_group], axis=-1)
    route_w = jnp.pad(route_w, ((0, 0), (0, 0), (0, LANES - route_w.shape[-1])))
    wr_hi = route_w.astype(BF16)

    def block_diag(a, b):
        return jnp.concatenate([jnp.pad(a, ((0, 0), (0, 0), (0, b.shape[2]))),
                                jnp.pad(b, ((0, 0), (0, 0), (a.shape[2], 0)))], axis=1)

    twice = lambda w: block_diag(w, w)
    p = dict(
        norm_mix=norm_mix.reshape(depth, 1, D_MODEL),
        norm_ffn=norm_ffn.reshape(depth, 1, D_MODEL),
        w_in=w_in.astype(BF16),
        w_vt=jnp.swapaxes(w_in[:, :, COL_V:COL_U], 1, 2).astype(BF16),
        qg=(jnp.tile(q_gain, (1, N_HEADS)) * scale).reshape(depth, 1, ATTN_WIDTH),
        kg=jnp.tile(k_gain, (1, N_KV_HEADS)).reshape(depth, 1, KV_WIDTH),
        bd=jnp.asarray(head_id[:, None] == head_id[None, :], dtype=BF16),
        conv_w=conv_w, conv_b=conv_b.reshape(depth, 1, 3 * D_HYENA),
        filt_w1=twice(jnp.pad(filt_w1, ((0, 0), (0, FILTER_ORDER - FILTER_EMB), (0, 0)))),
        filt_b1=jnp.tile(filt_b1, (1, 2)).reshape(depth, 1, 2 * FILTER_ORDER),
        filt_w2=twice(filt_w2), filt_b2=jnp.tile(filt_b2, (1, 2)).reshape(depth, 1, 2 * FILTER_ORDER),
        filt_w3=twice(filt_w3), filt_b3=jnp.tile(filt_b3, (1, 2)).reshape(depth, 1, 2 * FILTER_ORDER),
        filt_freq=jnp.tile(filt_freq, (1, 2)).reshape(depth, 1, 2 * FILTER_ORDER),
        filt_w4=block_diag(filt_w4[:, :, :D_HYENA], filt_w4[:, :, D_HYENA:]),
        skip=hyena_skip.reshape(depth, 1, D_HYENA),
        w_br_attn=w_br_attn.astype(BF16), w_br_hyena=w_br_hyena.astype(BF16), w_out=w_out.astype(BF16),
        wr_hi=wr_hi, wr_lo=(route_w - wr_hi.astype(F32)).astype(BF16),
        b_route=jnp.pad(jnp.concatenate([b_router, b_group], axis=-1),
                        ((0, 0), (0, LANES - N_EXPERTS - N_GROUPS))).reshape(depth, 1, LANES),
        w_gu=jnp.concatenate([w_e_gate, w_e_up], axis=-1).astype(BF16),
        w_down=w_e_down.astype(BF16),
    )
    y_prompt = _run_trunk(x_prompt, mod[:, :bp], p)
    y_sample = _run_trunk(x_sample, mod[:, bp:bp + bs], p)
    return (y_prompt, y_sample)
```

```python
import functools
import math

import numpy as np
import jax
import jax.numpy as jnp
from jax import lax
from jax.experimental import pallas as pl
from jax.experimental.pallas import tpu as pltpu

F32 = jnp.float32
BF16 = jnp.bfloat16

D_MODEL = 1024
GRID_W = 64
N_HEADS = 8
N_KV_HEADS = 4
HEAD_DIM = 64
Q_PER_KV = N_HEADS // N_KV_HEADS
ATTN_WIDTH = N_HEADS * HEAD_DIM
KV_WIDTH = N_KV_HEADS * HEAD_DIM
ROPE_THETA = 10000.0
ROPE_PAIRS_PER_AXIS = HEAD_DIM // 4
D_HYENA = D_MODEL // 2
FILTER_EMB = 33
FILTER_BANDS = (FILTER_EMB - 1) // 2
FILTER_ORDER = 64
DECAY_TARGET = 1e-2
MAX_DECAY = math.log(DECAY_TARGET) / 0.3
MIN_DECAY = math.log(DECAY_TARGET) / 1.5
MOD_SHIFT = 0.05
N_GROUPS = 4
EXPERTS_PER_GROUP = 8
N_EXPERTS = N_GROUPS * EXPERTS_PER_GROUP
TOP_K = 2
D_EXPERT = D_MODEL // 4
RMS_EPS = 1e-6
IN_COLS = ATTN_WIDTH + 2 * KV_WIDTH + 3 * D_HYENA + 2 * D_MODEL
COL_K = ATTN_WIDTH
COL_V = ATTN_WIDTH + KV_WIDTH
COL_U = ATTN_WIDTH + 2 * KV_WIDTH
COL_G = COL_U + 3 * D_HYENA

LANES = 128
SUBLANES = 8
ROW_TILES = D_MODEL // 2 // LANES
VMEM_LIMIT = 56 * 1024 * 1024

V_ROWS = HEAD_DIM + 16
SCORE_BOUND = 60.0
FFT_N2 = 128
FFT_PITCH = FFT_N2 + SUBLANES
EXPERT_BLOCK = 512
NEG_BIG = -1e30

_HI = lax.Precision.HIGHEST


def _dot32(a, b):
    return jnp.dot(a, b, precision=_HI, preferred_element_type=F32)


def _dotbf(a, b):
    return jnp.dot(a, b, preferred_element_type=F32)


def _sigmoid(x):
    return 0.5 * jnp.tanh(0.5 * x) + 0.5


def _cparams(sem):
    return pltpu.CompilerParams(dimension_semantics=sem, vmem_limit_bytes=VMEM_LIMIT)


def _ada_kernel(c_ref, w_ref, b_ref, o_ref):
    c = c_ref[...]
    act = c * jax.nn.sigmoid(c)
    o_ref[0] = _dot32(act, w_ref[0]) + b_ref[0]


def _ada(c_pad, w_ada, b_ada):
    depth, d, n = w_ada.shape
    tn = 1536
    return pl.pallas_call(
        _ada_kernel,
        out_shape=jax.ShapeDtypeStruct((depth, c_pad.shape[0], n), F32),
        grid=(depth, n // tn),
        in_specs=[pl.BlockSpec(c_pad.shape, lambda l, j: (0, 0)),
                  pl.BlockSpec((1, d, tn), lambda l, j: (l, 0, j)),
                  pl.BlockSpec((1, 1, tn), lambda l, j: (l, 0, j))],
        out_specs=pl.BlockSpec((1, c_pad.shape[0], tn), lambda l, j: (l, 0, j)),
        compiler_params=_cparams(("arbitrary", "arbitrary")),
        name="ada_mod",
    )(c_pad, w_ada, b_ada.reshape(depth, 1, n))


def _swap_pairs(x):
    n = x.shape[-1]
    lane = lax.broadcasted_iota(jnp.int32, x.shape, 1)
    nxt = pltpu.roll(x, n - 1, 1)
    prv = pltpu.roll(x, 1, 1)
    return jnp.where(lane % 2 == 0, nxt, prv)


def _head_norm_rope(p, gain, bd, cos, sin_signed):
    sq = (p * p).astype(BF16)
    ms = _dotbf(sq, bd) * (1.0 / HEAD_DIM)
    pn = p * lax.rsqrt(ms + RMS_EPS) * gain
    return pn * cos + _swap_pairs(pn) * sin_signed


def _store_groups(ref, x):
    pad = jnp.zeros((FFT_PITCH - FFT_N2, x.shape[1]), x.dtype)
    for g in range(ref.shape[0]):
        ref[g, 0:FFT_N2, :] = x[g * FFT_N2:(g + 1) * FFT_N2]
        ref[g, FFT_N2:FFT_PITCH, :] = pad


def _inproj_kernel(x_ref, xp_ref, xn_ref, shift_ref, scale_ref, gain_ref, w_ref, wvt_ref, qg_ref, kg_ref,
                   bd_ref, cos_ref, sin_ref, cw_ref, cb_ref, q_ref, k_ref, vt_ref, z_ref, x2_ref, g_ref, *, tps):
    def normed(x):
        ms = jnp.mean(x * x, axis=-1, keepdims=True)
        h = x * lax.rsqrt(ms + RMS_EPS) * gain_ref[...]
        return (h * (1.0 + scale_ref[0]) + shift_ref[0]).astype(BF16)

    hb = normed(x_ref[...])
    tm = hb.shape[0]
    cos = cos_ref[...]
    sin = sin_ref[...]
    bd = bd_ref[...]

    q = _dotbf(hb, w_ref[:, 0:COL_K])
    q = _head_norm_rope(q, qg_ref[...], bd, cos, sin)
    zpad = jnp.zeros((tm, LANES - HEAD_DIM), BF16)
    for hd in range(N_HEADS):
        q_ref[0, hd, :, 0:HEAD_DIM] = q[:, hd * HEAD_DIM:(hd + 1) * HEAD_DIM].astype(BF16)
        q_ref[0, hd, :, HEAD_DIM:LANES] = zpad

    k = _dotbf(hb, w_ref[:, COL_K:COL_V])
    k = _head_norm_rope(k, kg_ref[...], bd[:KV_WIDTH, :KV_WIDTH], cos[:, :KV_WIDTH], sin[:, :KV_WIDTH])
    vt = lax.dot_general(wvt_ref[...], hb, (((1,), (1,)), ((), ())), preferred_element_type=F32)
    ones = jnp.ones((V_ROWS - HEAD_DIM, vt.shape[1]), BF16)
    for hd in range(N_KV_HEADS):
        k_ref[0, hd, :, 0:HEAD_DIM] = k[:, hd * HEAD_DIM:(hd + 1) * HEAD_DIM].astype(BF16)
        k_ref[0, hd, :, HEAD_DIM:LANES] = zpad
        vt_ref[0, hd, 0:HEAD_DIM, :] = vt[hd * HEAD_DIM:(hd + 1) * HEAD_DIM].astype(BF16)
        vt_ref[0, hd, HEAD_DIM:V_ROWS, :] = ones

    g_ref[...] = _sigmoid(_dotbf(hb, w_ref[:, COL_G:IN_COLS])).astype(BF16)

    halo = normed(jnp.concatenate([xp_ref[...], xn_ref[...]], axis=0))
    u_all = _dotbf(jnp.concatenate([hb, halo], axis=0), w_ref[:, COL_U:COL_G])
    u = u_all[:tm]
    pos = pl.program_id(0) % tps
    prev_row = u_all[tm + SUBLANES - 1:tm + SUBLANES] * (pos > 0).astype(F32)
    next_row = u_all[tm + SUBLANES:tm + SUBLANES + 1] * (pos < tps - 1).astype(F32)
    ridx = lax.broadcasted_iota(jnp.int32, u.shape, 0)
    u_prev = jnp.where(ridx == 0, prev_row, pltpu.roll(u, 1, 0))
    u_next = jnp.where(ridx == tm - 1, next_row, pltpu.roll(u, tm - 1, 0))
    w = cw_ref[...]
    c = cb_ref[...] + u_prev * w[0:1] + u * w[1:2] + u_next * w[2:3]
    _store_groups(z_ref.at[0], c[:, 2 * D_HYENA:3 * D_HYENA] * c[:, 0:D_HYENA])
    _store_groups(x2_ref.at[0], c[:, D_HYENA:2 * D_HYENA])


def _inproj(x, shift, scale, gain, w_in, w_vt, qg, kg, bd, cos, sin, conv_w, conv_b, nb, seq):
    t = nb * seq
    tm = min(512, seq)
    tps = seq // tm
    rpb = tm // SUBLANES
    row = lambda i: (i, 0)
    per_b = lambda i: (i // tps, 0, 0)
    const2 = lambda i: (0, 0)
    pos = lambda i: (i % tps, 0)
    head_out = lambda i: (i // tps, 0, i % tps, 0)
    return pl.pallas_call(
        functools.partial(_inproj_kernel, tps=tps),
        out_shape=(jax.ShapeDtypeStruct((nb, N_HEADS, seq, LANES), BF16),
                   jax.ShapeDtypeStruct((nb, N_KV_HEADS, seq, LANES), BF16),
                   jax.ShapeDtypeStruct((nb, N_KV_HEADS, V_ROWS, seq), BF16),
                   jax.ShapeDtypeStruct((nb, seq // FFT_N2, FFT_PITCH, D_HYENA), F32),
                   jax.ShapeDtypeStruct((nb, seq // FFT_N2, FFT_PITCH, D_HYENA), F32),
                   jax.ShapeDtypeStruct((t, 2 * D_MODEL), BF16)),
        grid=(t // tm,),
        in_specs=[pl.BlockSpec((tm, D_MODEL), row),
                  pl.BlockSpec((SUBLANES, D_MODEL), lambda i: (jnp.maximum(i * rpb - 1, 0), 0)),
                  pl.BlockSpec((SUBLANES, D_MODEL), lambda i: (jnp.minimum((i + 1) * rpb, t // SUBLANES - 1), 0)),
                  pl.BlockSpec((1, 1, D_MODEL), per_b),
                  pl.BlockSpec((1, 1, D_MODEL), per_b),
                  pl.BlockSpec((1, D_MODEL), const2),
                  pl.BlockSpec((D_MODEL, IN_COLS), const2),
                  pl.BlockSpec((KV_WIDTH, D_MODEL), const2),
                  pl.BlockSpec((1, ATTN_WIDTH), const2),
                  pl.BlockSpec((1, KV_WIDTH), const2),
                  pl.BlockSpec((ATTN_WIDTH, ATTN_WIDTH), const2),
                  pl.BlockSpec((tm, ATTN_WIDTH), pos),
                  pl.BlockSpec((tm, ATTN_WIDTH), pos),
                  pl.BlockSpec((3, 3 * D_HYENA), const2),
                  pl.BlockSpec((1, 3 * D_HYENA), const2)],
        out_specs=(pl.BlockSpec((1, N_HEADS, tm, LANES), head_out),
                   pl.BlockSpec((1, N_KV_HEADS, tm, LANES), head_out),
                   pl.BlockSpec((1, N_KV_HEADS, V_ROWS, tm), lambda i: (i // tps, 0, 0, i % tps)),
                   pl.BlockSpec((1, tm // FFT_N2, FFT_PITCH, D_HYENA), lambda i: (i // tps, i % tps, 0, 0)),
                   pl.BlockSpec((1, tm // FFT_N2, FFT_PITCH, D_HYENA), lambda i: (i // tps, i % tps, 0, 0)),
                   pl.BlockSpec((tm, 2 * D_MODEL), row)),
        compiler_params=_cparams(("arbitrary",)),
        name="in_proj",
    )(x, x, x, shift, scale, gain, w_in, w_vt, qg, kg, bd, cos, sin, conv_w, conv_b)


def _attn_kernel(q_ref, k_ref, vt_ref, o_ref, m_ref, acc_ref, *, tkc, online):
    seq = k_ref.shape[2]
    nchunk = seq // tkc
    acc_ref[...] = jnp.zeros(acc_ref.shape, F32)
    if online:
        m_ref[...] = jnp.full(m_ref.shape, NEG_BIG, F32)

    def body(c, carry):
        off = pl.multiple_of(c * tkc, tkc)
        kc = k_ref[0, 0, pl.ds(off, tkc), :]
        vc = vt_ref[0, 0, :, pl.ds(off, tkc)]
        for h in range(Q_PER_KV):
            st = lax.dot_general(kc, q_ref[0, h], (((1,), (1,)), ((), ())), preferred_element_type=F32)
            if online:
                m_prev = m_ref[h]
                m_new = jnp.maximum(m_prev, jnp.max(st, axis=0, keepdims=True))
                p = jnp.exp2(st - m_new).astype(BF16)
                acc_ref[h] = jnp.exp2(m_prev - m_new) * acc_ref[h] + _dotbf(vc, p)
                m_ref[h] = m_new
            else:
                acc_ref[h] += _dotbf(vc, jnp.exp2(st).astype(BF16))
        return carry

    lax.fori_loop(0, nchunk, body, 0, unroll=2 if nchunk % 2 == 0 else 1)
    outs = []
    for h in range(Q_PER_KV):
        a = acc_ref[h]
        outs.append((a[:HEAD_DIM] / a[HEAD_DIM:HEAD_DIM + 1]).T)
    o_ref[0] = jnp.concatenate(outs, axis=1).astype(BF16)


def _attention(q, k, vt, online):
    nb, _, seq, _ = q.shape
    tq = min(512 if online else 1024, seq)
    tkc = min(512 if online else 1024, seq)
    return pl.pallas_call(
        functools.partial(_attn_kernel, tkc=tkc, online=online),
        out_shape=jax.ShapeDtypeStruct((nb, seq, ATTN_WIDTH), BF16),
        grid=(nb, N_KV_HEADS, seq // tq),
        in_specs=[pl.BlockSpec((1, Q_PER_KV, tq, LANES), lambda b, g, i: (b, g, i, 0)),
                  pl.BlockSpec((1, 1, seq, LANES), lambda b, g, i: (b, g, 0, 0)),
                  pl.BlockSpec((1, 1, V_ROWS, seq), lambda b, g, i: (b, g, 0, 0))],
        out_specs=pl.BlockSpec((1, tq, Q_PER_KV * HEAD_DIM), lambda b, g, i: (b, i, g)),
        scratch_shapes=[pltpu.VMEM((Q_PER_KV, 1, tq), F32),
                        pltpu.VMEM((Q_PER_KV, V_ROWS, tq), F32)],
        compiler_params=_cparams(("arbitrary", "arbitrary", "arbitrary")),
        name="attn_online" if online else "attn_bounded",
    )(q, k, vt)


def _filter_kernel(z_ref, w1_ref, b1_ref, w2_ref, b2_ref, w3_ref, b3_ref, fr_ref, w4_ref,
                   dl_ref, o_ref):
    z = z_ref[...]
    fr = fr_ref[...]
    h = jnp.sin(fr * (_dot32(z, w1_ref[...]) + b1_ref[...]))
    h = jnp.sin(fr * (_dot32(h, w2_ref[...]) + b2_ref[...]))
    h = jnp.sin(fr * (_dot32(h, w3_ref[...]) + b3_ref[...]))
    h = _dot32(h, w4_ref[...])
    dl = dl_ref[...]
    for d in range(2):
        t = z[:, d * FILTER_ORDER:d * FILTER_ORDER + 1]
        sign = z[:, d * FILTER_ORDER + FILTER_EMB:d * FILTER_ORDER + FILTER_EMB + 1]
        _store_groups(o_ref.at[d], h[:, d * D_HYENA:(d + 1) * D_HYENA] * (jnp.exp(-t * dl) + MOD_SHIFT) * sign)


def _filters(zfeat, w1, b1, w2, b2, w3, b3, fr, w4, deltas):
    seq, fe = zfeat.shape
    tm = min(512, seq)
    c2 = lambda i: (0, 0)
    return pl.pallas_call(
        _filter_kernel,
        out_shape=jax.ShapeDtypeStruct((2, seq // FFT_N2, FFT_PITCH, D_HYENA), F32),
        grid=(seq // tm,),
        in_specs=[pl.BlockSpec((tm, fe), lambda i: (i, 0)),
                  pl.BlockSpec(w1.shape, c2), pl.BlockSpec(b1.shape, c2),
                  pl.BlockSpec(w2.shape, c2), pl.BlockSpec(b2.shape, c2),
                  pl.BlockSpec(w3.shape, c2), pl.BlockSpec(b3.shape, c2),
                  pl.BlockSpec(fr.shape, c2), pl.BlockSpec(w4.shape, c2),
                  pl.BlockSpec(deltas.shape, c2)],
        out_specs=pl.BlockSpec((2, tm // FFT_N2, FFT_PITCH, D_HYENA), lambda i: (0, i, 0, 0)),
        compiler_params=_cparams(("arbitrary",)),
        name="hyena_filters",
    )(zfeat, w1, b1, w2, b2, w3, b3, fr, w4, deltas)


def _zero_pad_rows(o_ref, groups):
    zeros = jnp.zeros((groups, o_ref.shape[1]), o_ref.dtype)
    for s in range(FFT_N2, FFT_PITCH):
        o_ref[pl.ds(s, groups, stride=FFT_PITCH), :] = zeros


def _fft_outer_kernel(g_ref, x_ref, o_ref):
    g = g_ref[...]
    m, kk = g.shape

    def body(jp, c):
        j = 2 * jp
        xj = jnp.concatenate([x_ref[pl.ds(j, kk, stride=FFT_PITCH), :],
                              x_ref[pl.ds(j + 1, kk, stride=FFT_PITCH), :]], axis=1)
        r = _dotbf(g, xj.astype(BF16))
        o_ref[pl.ds(j, m, stride=FFT_PITCH), :] = r[:, :LANES]
        o_ref[pl.ds(j + 1, m, stride=FFT_PITCH), :] = r[:, LANES:]
        return c

    lax.fori_loop(0, FFT_N2 // 2, body, 0, unroll=4)
    _zero_pad_rows(o_ref, m)


def _fft_outer(gmat, x):
    nb, rows, ch = x.shape
    m, kk = gmat.shape
    return pl.pallas_call(
        _fft_outer_kernel,
        out_shape=jax.ShapeDtypeStruct((nb, m * FFT_PITCH, ch), F32),
        grid=(nb, ch // LANES),
        in_specs=[pl.BlockSpec((m, kk), lambda b, c: (0, 0)),
                  pl.BlockSpec((None, rows, LANES), lambda b, c: (b, 0, c))],
        out_specs=pl.BlockSpec((None, m * FFT_PITCH, LANES), lambda b, c: (b, 0, c)),
        compiler_params=_cparams(("arbitrary", "arbitrary")),
        name="fft_outer",
    )(gmat, x)


FFT_K1_PER_STEP = 4


def _fft_filter_inner_kernel(mf_ref, a_ref, o_ref):
    n2, ch = FFT_N2, a_ref.shape[4]
    for r in range(a_ref.shape[2]):
        a = a_ref[0, :, r, 0:n2, :].reshape(2 * n2, ch)
        o_ref[r] = _dotbf(mf_ref[r], a.astype(BF16)).reshape(2, n2, ch)


def _fft_filter_inner(mf, a5):
    _, _, h1, pitch, ch = a5.shape
    n2 = FFT_N2
    kr = FFT_K1_PER_STEP
    return pl.pallas_call(
        _fft_filter_inner_kernel,
        out_shape=jax.ShapeDtypeStruct((h1, 2, n2, ch), F32),
        grid=(h1 // kr,),
        in_specs=[pl.BlockSpec((kr, 2 * n2, 2 * n2), lambda k: (k, 0, 0)),
                  pl.BlockSpec((1, 2, kr, pitch, ch), lambda k: (0, 0, k, 0, 0))],
        out_specs=pl.BlockSpec((kr, 2, n2, ch), lambda k: (k, 0, 0, 0)),
        compiler_params=_cparams(("arbitrary",)),
        name="fft_filter_inner",
    )(mf, a5)


def _fft_inner_kernel(mf_ref, mi_ref, kf_ref, a_ref, o_ref):
    n2, pitch, ch = FFT_N2, a_ref.shape[3], a_ref.shape[4]
    for r in range(a_ref.shape[2]):
        a = a_ref[0, :, r, 0:n2, :].reshape(2 * n2, ch)
        xs = _dotbf(mf_ref[r], a.astype(BF16))
        xr, xi = xs[:n2], xs[n2:]
        kr, ki = kf_ref[r, 0], kf_ref[r, 1]
        p = jnp.concatenate([xr * kr - xi * ki, xr * ki + xi * kr], axis=0)
        o_ref[0, :, r, 0:n2, :] = _dotbf(mi_ref[r], p.astype(BF16)).reshape(2, n2, ch)
        o_ref[0, :, r, n2:pitch, :] = jnp.zeros((2, pitch - n2, ch), F32)


def _fft_inner(mf, mi, kf, a5):
    nb, _, h1, pitch, ch = a5.shape
    n2 = FFT_N2
    kr = FFT_K1_PER_STEP
    return pl.pallas_call(
        _fft_inner_kernel,
        out_shape=jax.ShapeDtypeStruct(a5.shape, F32),
        grid=(h1 // kr, nb),
        in_specs=[pl.BlockSpec((kr, 2 * n2, 2 * n2), lambda k, b: (k, 0, 0)),
                  pl.BlockSpec((kr, 2 * n2, 2 * n2), lambda k, b: (k, 0, 0)),
                  pl.BlockSpec((kr, 2, n2, ch), lambda k, b: (k, 0, 0, 0)),
                  pl.BlockSpec((1, 2, kr, pitch, ch), lambda k, b: (b, 0, k, 0, 0))],
        out_specs=pl.BlockSpec((1, 2, kr, pitch, ch), lambda k, b: (b, 0, k, 0, 0)),
        compiler_params=_cparams(("arbitrary", "arbitrary")),
        name="fft_inner",
    )(mf, mi, kf, a5)


def _fft_final_kernel(g_ref, b_ref, z_ref, x2_ref, skip_ref, o_ref):
    g = g_ref[...]
    h1, m = g.shape
    skip = skip_ref[...]

    def body(jp, c):
        j = 2 * jp
        bj = jnp.concatenate([b_ref[pl.ds(j, m, stride=FFT_PITCH), :],
                              b_ref[pl.ds(j + 1, m, stride=FFT_PITCH), :]], axis=1)
        y = _dotbf(g, bj.astype(BF16))
        for d in range(2):
            zj = z_ref[pl.ds(j + d, h1, stride=FFT_PITCH), :]
            xj = x2_ref[pl.ds(j + d, h1, stride=FFT_PITCH), :]
            o_ref[pl.ds(j + d, h1, stride=FFT_PITCH), :] = xj * (y[:, d * LANES:(d + 1) * LANES] + zj * skip)
        return c

    lax.fori_loop(0, FFT_N2 // 2, body, 0, unroll=4)
    _zero_pad_rows(o_ref, h1)


def _fft_final(gc, bm, z, x2, skip):
    nb, rows, ch = z.shape
    h1, m = gc.shape
    blk = lambda r: pl.BlockSpec((None, r, LANES), lambda b, c: (b, 0, c))
    return pl.pallas_call(
        _fft_final_kernel,
        out_shape=jax.ShapeDtypeStruct((nb, rows, ch), F32),
        grid=(nb, ch // LANES),
        in_specs=[pl.BlockSpec((h1, m), lambda b, c: (0, 0)), blk(m * FFT_PITCH), blk(rows), blk(rows),
                  pl.BlockSpec((1, LANES), lambda b, c: (0, c))],
        out_specs=blk(rows),
        compiler_params=_cparams(("arbitrary", "arbitrary")),
        name="fft_final",
    )(gc, bm, z, x2, skip)


def _dft_tables(seq):
    n = 2 * seq
    n2 = FFT_N2
    n1 = n // n2
    h1 = n1 // 2
    i32 = jnp.int32
    k1 = jnp.arange(h1, dtype=i32)[:, None]
    a1 = jnp.arange(n1, dtype=i32)[None, :]
    th = (math.pi / n1) * ((a1 * (2 * k1 + 1)) % (2 * n1)).astype(F32)
    g_re, g_im = jnp.cos(th), -jnp.sin(th)
    g_full = jnp.concatenate([g_re, g_im], axis=0)
    g_half = g_full[:, :h1]
    g_out = (2.0 / n) * jnp.concatenate([g_re[:, :h1].T, g_im[:, :h1].T], axis=1)
    k2 = jnp.arange(n2, dtype=i32)[None, :, None]
    b2 = jnp.arange(n2, dtype=i32)[None, None, :]
    kk = jnp.arange(h1, dtype=i32)[:, None, None]
    ph = (math.pi / n) * ((b2 * (2 * kk + 1 + 2 * n1 * k2)) % (2 * n)).astype(F32)
    m_re, m_im = jnp.cos(ph), -jnp.sin(ph)
    mf = jnp.concatenate([jnp.concatenate([m_re, -m_im], axis=2),
                          jnp.concatenate([m_im, m_re], axis=2)], axis=1)
    mt_re, mt_im = jnp.transpose(m_re, (0, 2, 1)), jnp.transpose(m_im, (0, 2, 1))
    mi = jnp.concatenate([jnp.concatenate([mt_re, mt_im], axis=2),
                          jnp.concatenate([-mt_im, mt_re], axis=2)], axis=1)
    b16 = lambda a: a.astype(BF16)
    return dict(n1=n1, h1=h1, g_full=b16(g_full), g_half=b16(g_half), g_out=b16(g_out), mf=b16(mf), mi=b16(mi))


def _merge_kernel(attn_ref, hy_ref, g_ref, x_ref, gate_ref, shift_ref, scale_ref, gain_ref,
                  wa_ref, wh_ref, wo_ref, wrh_ref, wrl_ref, br_ref,
                  x1_ref, h2_ref, route_ref, cnt_ref):
    g = g_ref[...].astype(F32)
    hy = jnp.concatenate([hy_ref[grp, 0:FFT_N2, :] for grp in range(hy_ref.shape[0])], axis=0)
    merged = (g[:, :D_MODEL] * _dotbf(attn_ref[...], wa_ref[...])
              + g[:, D_MODEL:] * _dotbf(hy.astype(BF16), wh_ref[...]))
    mix = _dotbf(merged.astype(BF16), wo_ref[...])
    x1 = x_ref[...] + gate_ref[0] * mix
    x1_ref[...] = x1

    ms = jnp.mean(x1 * x1, axis=-1, keepdims=True)
    h2 = x1 * lax.rsqrt(ms + RMS_EPS) * gain_ref[...]
    h2 = h2 * (1.0 + scale_ref[0]) + shift_ref[0]
    _rows_to_tiles(h2_ref, _pack_halves(h2))

    hi = h2.astype(BF16)
    lo = (h2 - hi.astype(F32)).astype(BF16)
    wrh = wrh_ref[...]
    lg = _dotbf(hi, wrh) + _dotbf(lo, wrh) + _dotbf(hi, wrl_ref[...]) + br_ref[...]

    lane = lax.broadcasted_iota(jnp.int32, lg.shape, 1).astype(F32)
    is_grp = jnp.logical_and(lane >= N_EXPERTS, lane < N_EXPERTS + N_GROUPS)
    gm = jnp.where(is_grp, lg, NEG_BIG)
    gmax = jnp.max(gm, axis=-1, keepdims=True)
    gidx = jnp.min(jnp.where(gm == gmax, lane, 1e9), axis=-1, keepdims=True) - N_EXPERTS
    p_group = 1.0 / jnp.sum(jnp.where(is_grp, jnp.exp(gm - gmax), 0.0), axis=-1, keepdims=True)
    lo_lane = gidx * EXPERTS_PER_GROUP
    in_grp = jnp.logical_and(lane >= lo_lane, lane < lo_lane + EXPERTS_PER_GROUP)
    e1v = jnp.where(in_grp, lg, NEG_BIG)
    t1 = jnp.max(e1v, axis=-1, keepdims=True)
    i1 = jnp.min(jnp.where(e1v == t1, lane, 1e9), axis=-1, keepdims=True)
    e2v = jnp.where(lane == i1, NEG_BIG, e1v)
    t2 = jnp.max(e2v, axis=-1, keepdims=True)
    i2 = jnp.min(jnp.where(e2v == t2, lane, 1e9), axis=-1, keepdims=True)
    d = jnp.exp(t2 - t1)
    w1 = p_group / (1.0 + d)
    w2 = p_group * d / (1.0 + d)
    route_ref[...] = jnp.where(lane == 0, i1, jnp.where(lane == 1, i2,
                               jnp.where(lane == 2, w1, jnp.where(lane == 3, w2, 0.0))))

    onehot = (lane == i1).astype(F32) + (lane == i2).astype(F32)

    @pl.when(pl.program_id(0) == 0)
    def _():
        cnt_ref[...] = jnp.zeros(cnt_ref.shape, F32)

    cnt_ref[...] += jnp.sum(onehot, axis=0, keepdims=True)


def _merge(attn, hy, gates, x, gate1, shift2, scale2, gain, wa, wh, wo, wrh, wrl, br, nb, seq):
    t = nb * seq
    tm = min(512, seq)
    tps = seq // tm
    row = lambda i: (i, 0)
    per_b = lambda i: (i // tps, 0, 0)
    c2 = lambda i: (0, 0)
    return pl.pallas_call(
        _merge_kernel,
        out_shape=(jax.ShapeDtypeStruct((t, D_MODEL), F32),
                   jax.ShapeDtypeStruct((t * ROW_TILES, LANES), F32),
                   jax.ShapeDtypeStruct((t, LANES), F32),
                   jax.ShapeDtypeStruct((1, LANES), F32)),
        grid=(t // tm,),
        in_specs=[pl.BlockSpec((tm, ATTN_WIDTH), row),
                  pl.BlockSpec((tm // FFT_N2, FFT_PITCH, D_HYENA), lambda i: (i, 0, 0)),
                  pl.BlockSpec((tm, 2 * D_MODEL), row),
                  pl.BlockSpec((tm, D_MODEL), row),
                  pl.BlockSpec((1, 1, D_MODEL), per_b),
                  pl.BlockSpec((1, 1, D_MODEL), per_b),
                  pl.BlockSpec((1, 1, D_MODEL), per_b),
                  pl.BlockSpec((1, D_MODEL), c2),
                  pl.BlockSpec(wa.shape, c2), pl.BlockSpec(wh.shape, c2), pl.BlockSpec(wo.shape, c2),
                  pl.BlockSpec(wrh.shape, c2), pl.BlockSpec(wrl.shape, c2), pl.BlockSpec(br.shape, c2)],
        out_specs=(pl.BlockSpec((tm, D_MODEL), row),
                   pl.BlockSpec((tm * ROW_TILES, LANES), row),
                   pl.BlockSpec((tm, LANES), row),
                   pl.BlockSpec((1, LANES), c2)),
        compiler_params=_cparams(("arbitrary",)),
        name="merge_router",
    )(attn, hy, gates, x, gate1, shift2, scale2, gain, wa, wh, wo, wrh, wrl, br)


def _rank_kernel(route_ref, pstart_ref, tri_ref, dest_ref, carry_ref):
    @pl.when(pl.program_id(0) == 0)
    def _():
        carry_ref[...] = jnp.zeros(carry_ref.shape, F32)

    r = route_ref[...]
    lane = lax.broadcasted_iota(jnp.int32, r.shape, 1).astype(F32)
    oh1 = (lane == r[:, 0:1]).astype(F32)
    oh2 = (lane == r[:, 1:2]).astype(F32)
    tri = tri_ref[...]
    before1 = _dotbf(tri, oh1.astype(BF16))
    before2 = _dotbf(tri, oh2.astype(BF16))
    base1 = pstart_ref[...] + carry_ref[...]
    d1 = jnp.sum(oh1 * (base1 + before1), axis=-1, keepdims=True)
    base2 = base1 + jnp.sum(oh1, axis=0, keepdims=True)
    d2 = jnp.sum(oh2 * (base2 + before2), axis=-1, keepdims=True)
    carry_ref[...] = base2 + jnp.sum(oh2, axis=0, keepdims=True) - pstart_ref[...]
    dest_ref[...] = jnp.where(lane == 0, d1, jnp.where(lane == 1, d2, 0.0))


def _rank(route, pstart, tri):
    t = route.shape[0]
    tm = tri.shape[0]
    return pl.pallas_call(
        _rank_kernel,
        out_shape=jax.ShapeDtypeStruct((t, LANES), F32),
        grid=(t // tm,),
        in_specs=[pl.BlockSpec((tm, LANES), lambda i: (i, 0)),
                  pl.BlockSpec((1, LANES), lambda i: (0, 0)),
                  pl.BlockSpec((tm, tm), lambda i: (0, 0))],
        out_specs=pl.BlockSpec((tm, LANES), lambda i: (i, 0)),
        scratch_shapes=[pltpu.VMEM((1, LANES), F32)],
        compiler_params=_cparams(("arbitrary",)),
        name="moe_rank",
    )(route, pstart, tri)


def _row_tile(ref, r):
    return ref.at[pl.ds(pl.multiple_of(r * ROW_TILES, ROW_TILES), ROW_TILES)]


def _pack_halves(x):
    half = x.shape[1] // 2
    hi = lax.bitcast_convert_type(x[:, :half].astype(BF16).astype(F32), jnp.uint32)
    lo = lax.bitcast_convert_type(x[:, half:].astype(BF16).astype(F32), jnp.uint32)
    return lax.bitcast_convert_type(hi | (lo >> 16), F32)


def _unpack_halves(p):
    u = lax.bitcast_convert_type(p, jnp.uint32)
    hi = lax.bitcast_convert_type(u & jnp.uint32(0xFFFF0000), F32)
    lo = lax.bitcast_convert_type(u << 16, F32)
    return jnp.concatenate([hi, lo], axis=1)


def _rows_from_tiles(ref, n):
    return jnp.concatenate([ref[pl.ds(s, n, stride=ROW_TILES), :] for s in range(ROW_TILES)], axis=1)


def _rows_to_tiles(ref, x):
    n = x.shape[0]
    for s in range(ROW_TILES):
        ref[pl.ds(s, n, stride=ROW_TILES), :] = x[:, s * LANES:(s + 1) * LANES]


def _dispatch_kernel(d1_ref, d2_ref, h_ref, init_ref, xs_ref, sem):
    del init_ref
    tm = h_ref.shape[0] // ROW_TILES
    base = pl.program_id(0) * tm

    def copies(r):
        src = _row_tile(h_ref, r)
        return (pltpu.make_async_copy(src, _row_tile(xs_ref, d1_ref[base + r]), sem),
                pltpu.make_async_copy(src, _row_tile(xs_ref, d2_ref[base + r]), sem))

    def issue(r, c):
        a, b = copies(r)
        a.start(priority=0)
        b.start(priority=1)
        return c

    def drain(r, c):
        a, b = copies(r)
        a.wait()
        b.wait()
        return c

    lax.fori_loop(0, tm, issue, 0, unroll=8)
    lax.fori_loop(0, tm, drain, 0, unroll=8)


def _dispatch(d1, d2, h2, init):
    t = h2.shape[0] // ROW_TILES
    tm = min(256, t)
    return pl.pallas_call(
        _dispatch_kernel,
        out_shape=jax.ShapeDtypeStruct(init.shape, h2.dtype),
        grid_spec=pltpu.PrefetchScalarGridSpec(
            num_scalar_prefetch=2,
            grid=(t // tm,),
            in_specs=[pl.BlockSpec((tm * ROW_TILES, LANES), lambda i, a, b: (i, 0)),
                      pl.BlockSpec(memory_space=pl.ANY)],
            out_specs=pl.BlockSpec(memory_space=pl.ANY),
            scratch_shapes=[pltpu.SemaphoreType.DMA(())]),
        input_output_aliases={3: 0},
        compiler_params=_cparams(("arbitrary",)),
        name="moe_dispatch",
    )(d1, d2, h2, init)


def _expert_kernel(be_ref, xs_ref, wgu_ref, wd_ref, ys_ref):
    del be_ref
    x = _unpack_halves(_rows_from_tiles(xs_ref, EXPERT_BLOCK)).astype(BF16)
    gu = _dotbf(x, wgu_ref[0])
    g = gu[:, :D_EXPERT]
    u = gu[:, D_EXPERT:]
    a = (g * _sigmoid(g) * u).astype(BF16)
    _rows_to_tiles(ys_ref, _pack_halves(_dotbf(a, wd_ref[0])))


def _experts(blk_expert, xs, wgu, wd):
    cap = xs.shape[0] // ROW_TILES
    d = wgu.shape[1]
    blk = pl.BlockSpec((EXPERT_BLOCK * ROW_TILES, LANES), lambda i, be: (i, 0))
    return pl.pallas_call(
        _expert_kernel,
        out_shape=jax.ShapeDtypeStruct(xs.shape, F32),
        grid_spec=pltpu.PrefetchScalarGridSpec(
            num_scalar_prefetch=1,
            grid=(cap // EXPERT_BLOCK,),
            in_specs=[blk,
                      pl.BlockSpec((1, d, 2 * D_EXPERT), lambda i, be: (be[i], 0, 0)),
                      pl.BlockSpec((1, D_EXPERT, d), lambda i, be: (be[i], 0, 0))],
            out_specs=blk),
        compiler_params=_cparams(("arbitrary",)),
        name="moe_experts",
    )(blk_expert, xs, wgu, wd)


def _combine_kernel(d1_ref, d2_ref, ys_ref, x_ref, route_ref, gate_ref, o_ref, y1_ref, y2_ref, sem):
    tm = x_ref.shape[0]
    i = pl.program_id(0)
    slot = i % 2

    def copies(step, s, r):
        base = step * tm
        return (pltpu.make_async_copy(_row_tile(ys_ref, d1_ref[base + r]), _row_tile(y1_ref.at[s], r), sem.at[s]),
                pltpu.make_async_copy(_row_tile(ys_ref, d2_ref[base + r]), _row_tile(y2_ref.at[s], r), sem.at[s]))

    def issue(step, s):
        def body(r, c):
            a, b = copies(step, s, r)
            a.start(priority=0)
            b.start(priority=1)
            return c

        lax.fori_loop(0, tm, body, 0, unroll=8)

    def drain(step, s):
        def body(r, c):
            a, b = copies(step, s, r)
            a.wait()
            b.wait()
            return c

        lax.fori_loop(0, tm, body, 0, unroll=8)

    @pl.when(i == 0)
    def _():
        issue(0, 0)

    @pl.when(i + 1 < pl.num_programs(0))
    def _():
        issue(i + 1, 1 - slot)

    drain(i, slot)
    r = route_ref[...]
    ffn = (_unpack_halves(_rows_from_tiles(y1_ref.at[slot], tm)) * r[:, 2:3]
           + _unpack_halves(_rows_from_tiles(y2_ref.at[slot], tm)) * r[:, 3:4])
    o_ref[...] = x_ref[...] + gate_ref[0] * ffn


def _combine(d1, d2, ys, x1, route, gate2, nb, seq):
    t, d = x1.shape
    tm = min(256, seq)
    tps = seq // tm
    return pl.pallas_call(
        _combine_kernel,
        out_shape=jax.ShapeDtypeStruct((t, d), F32),
        grid_spec=pltpu.PrefetchScalarGridSpec(
            num_scalar_prefetch=2,
            grid=(t // tm,),
            in_specs=[pl.BlockSpec(memory_space=pl.ANY),
                      pl.BlockSpec((tm, d), lambda i, a, b: (i, 0)),
                      pl.BlockSpec((tm, LANES), lambda i, a, b: (i, 0)),
                      pl.BlockSpec((1, 1, d), lambda i, a, b: (i // tps, 0, 0))],
            out_specs=pl.BlockSpec((tm, d), lambda i, a, b: (i, 0)),
            scratch_shapes=[pltpu.VMEM((2, tm * ROW_TILES, LANES), F32),
                            pltpu.VMEM((2, tm * ROW_TILES, LANES), F32),
                            pltpu.SemaphoreType.DMA((2,))]),
        compiler_params=_cparams(("arbitrary",)),
        name="moe_combine",
    )(d1, d2, ys, x1, route, gate2)


def _rope_tables(seq):
    rows = seq // GRID_W
    row = jnp.broadcast_to(jnp.arange(rows, dtype=F32)[:, None], (rows, GRID_W)).reshape(-1)
    col = jnp.broadcast_to(jnp.arange(GRID_W, dtype=F32)[None, :], (rows, GRID_W)).reshape(-1)
    inv_freq = ROPE_THETA ** (-jnp.arange(ROPE_PAIRS_PER_AXIS, dtype=F32) / ROPE_PAIRS_PER_AXIS)
    ang = jnp.concatenate([row[:, None] * inv_freq, col[:, None] * inv_freq], axis=-1)
    cos = jnp.repeat(jnp.cos(ang), 2, axis=-1)
    sin = jnp.repeat(jnp.sin(ang), 2, axis=-1)
    sign = jnp.tile(jnp.array([-1.0, 1.0], F32), HEAD_DIM // 2)
    return jnp.tile(cos, (1, N_HEADS)), jnp.tile(sin * sign, (1, N_HEADS))


def _filter_features(seq):
    t = jnp.linspace(0.0, 1.0, seq, dtype=F32)[:, None]
    w = (2.0 * math.pi / seq) * jnp.arange(seq, dtype=F32)[:, None]
    bands = jnp.linspace(1e-4, FILTER_BANDS - 1, FILTER_BANDS, dtype=F32)[None, :]
    z = jnp.concatenate([t, jnp.cos(bands * w), -jnp.sin(bands * w)], axis=-1)
    pad = jnp.zeros((seq, FILTER_ORDER - FILTER_EMB), F32)
    fwd = jnp.concatenate([z, pad], axis=-1).at[:, FILTER_EMB].set(1.0)
    rev = jnp.concatenate([z[:1], z[:0:-1]], axis=0)
    sign = jnp.concatenate([jnp.zeros((1,), F32), -jnp.ones((seq - 1,), F32)])
    bwd = jnp.concatenate([rev, pad], axis=-1).at[:, FILTER_EMB].set(sign)
    deltas = jnp.abs(jnp.linspace(MIN_DECAY, MAX_DECAY, D_HYENA, dtype=F32))[None, :]
    return jnp.concatenate([fwd, bwd], axis=-1), deltas


def _run_trunk(x, mod, p):
    nb, seq, d = x.shape
    t = nb * seq
    depth = mod.shape[0]
    cos, sin = _rope_tables(seq)
    zfeat, deltas = _filter_features(seq)
    tabs = _dft_tables(seq)
    h1 = tabs["h1"]
    tm_rank = min(512, t)
    tri = jnp.tril(jnp.ones((tm_rank, tm_rank), BF16), -1)
    cap = t * TOP_K + N_EXPERTS * EXPERT_BLOCK
    n_blk = cap // EXPERT_BLOCK
    lane_e = jnp.arange(LANES)

    xf = x.reshape(t, d)
    xs = jnp.zeros((cap * ROW_TILES, LANES), F32)
    for l in range(depth):
        m6 = mod[l].reshape(nb, 6, 1, d)
        shift1, scale1, gate1, shift2, scale2, gate2 = (m6[:, j] for j in range(6))

        q, k, vt, z, x2, gates = _inproj(xf, shift1, scale1, p["norm_mix"][l], p["w_in"][l], p["w_vt"][l],
                                         p["qg"][l], p["kg"][l], p["bd"], cos, sin,
                                         p["conv_w"][l], p["conv_b"][l], nb, seq)
        z = z.reshape(nb, h1 * FFT_PITCH, D_HYENA)
        x2 = x2.reshape(nb, h1 * FFT_PITCH, D_HYENA)
        bound = HEAD_DIM * jnp.max(jnp.abs(p["qg"][l])) * jnp.max(jnp.abs(p["kg"][l]))
        attn = lax.cond(bound <= SCORE_BOUND,
                        functools.partial(_attention, online=False),
                        functools.partial(_attention, online=True), q, k, vt).reshape(t, ATTN_WIDTH)

        kern = _filters(zfeat, p["filt_w1"][l], p["filt_b1"][l], p["filt_w2"][l], p["filt_b2"][l],
                        p["filt_w3"][l], p["filt_b3"][l], p["filt_freq"][l], p["filt_w4"][l], deltas)
        ka = _fft_outer(tabs["g_full"], kern.reshape(1, 2 * h1 * FFT_PITCH, D_HYENA))
        kf = _fft_filter_inner(tabs["mf"], ka.reshape(1, 2, h1, FFT_PITCH, D_HYENA))
        za = _fft_outer(tabs["g_half"], z)
        zb = _fft_inner(tabs["mf"], tabs["mi"], kf, za.reshape(nb, 2, h1, FFT_PITCH, D_HYENA))
        hy = _fft_final(tabs["g_out"], zb.reshape(nb, 2 * h1 * FFT_PITCH, D_HYENA), z, x2,
                        p["skip"][l]).reshape(nb * h1, FFT_PITCH, D_HYENA)

        x1, h2, route, counts = _merge(attn, hy, gates, xf, gate1, shift2, scale2, p["norm_ffn"][l],
                                       p["w_br_attn"][l], p["w_br_hyena"][l], p["w_out"][l],
                                       p["wr_hi"][l], p["wr_lo"][l], p["b_route"][l], nb, seq)

        cnt = counts[0].astype(jnp.int32)
        padded = jnp.where(lane_e < N_EXPERTS, (cnt + EXPERT_BLOCK - 1) // EXPERT_BLOCK * EXPERT_BLOCK, 0)
        pad_end = jnp.cumsum(padded)
        pstart = (pad_end - padded).astype(F32)[None, :]
        blk_start = jnp.arange(n_blk, dtype=jnp.int32) * EXPERT_BLOCK
        blk_expert = jnp.minimum(
            jnp.sum(blk_start[:, None] >= pad_end[None, :N_EXPERTS], axis=1), N_EXPERTS - 1).astype(jnp.int32)

        dest = _rank(route, pstart, tri)
        d1 = dest[:, 0].astype(jnp.int32)
        d2 = dest[:, 1].astype(jnp.int32)
        xs = _dispatch(d1, d2, h2, xs)
        ys = _experts(blk_expert, xs, p["w_gu"][l], p["w_down"][l])
        xf = _combine(d1, d2, ys, x1, route, gate2, nb, seq)
    return xf.reshape(nb, seq, d)


def kernel(x_prompt, x_sample, c_prompt, c_sample, w_ada, b_ada, norm_mix, norm_ffn, w_in, q_gain, k_gain, conv_w, conv_b, filt_w1, filt_b1, filt_w2, filt_b2, filt_w3, filt_b3, filt_freq, filt_w4, hyena_skip, w_br_attn, w_br_hyena, w_out, w_group, b_group, w_router, b_router, w_e_gate, w_e_up, w_e_down):
    depth = w_ada.shape[0]
    bp, bs = c_prompt.shape[0], c_sample.shape[0]
    rows = -(-(bp + bs) // SUBLANES) * SUBLANES
    c_pad = jnp.zeros((rows, D_MODEL), F32).at[:bp].set(c_prompt).at[bp:bp + bs].set(c_sample)
    mod = _ada(c_pad, w_ada, b_ada)

    scale = HEAD_DIM ** -0.5 * math.log2(math.e)
    head_id = np.arange(ATTN_WIDTH) // HEAD_DIM
    route_w = jnp.concatenate([w_router, w_group], axis=-1)
    route_w = jnp.pad(route_w, ((0, 0), (0, 0), (0, LANES - route_w.shape[-1])))
    wr_hi = route_w.astype(BF16)

    def block_diag(a, b):
        return jnp.concatenate([jnp.pad(a, ((0, 0), (0, 0), (0, b.shape[2]))),
                                jnp.pad(b, ((0, 0), (0, 0), (a.shape[2], 0)))], axis=1)

    twice = lambda w: block_diag(w, w)
    p = dict(
        norm_mix=norm_mix.reshape(depth, 1, D_MODEL),
        norm_ffn=norm_ffn.reshape(depth, 1, D_MODEL),
        w_in=w_in.astype(BF16),
        w_vt=jnp.swapaxes(w_in[:, :, COL_V:COL_U], 1, 2).astype(BF16),
        qg=(jnp.tile(q_gain, (1, N_HEADS)) * scale).reshape(depth, 1, ATTN_WIDTH),
        kg=jnp.tile(k_gain, (1, N_KV_HEADS)).reshape(depth, 1, KV_WIDTH),
        bd=jnp.asarray(head_id[:, None] == head_id[None, :], dtype=BF16),
        conv_w=conv_w, conv_b=conv_b.reshape(depth, 1, 3 * D_HYENA),
        filt_w1=twice(jnp.pad(filt_w1, ((0, 0), (0, FILTER_ORDER - FILTER_EMB), (0, 0)))),
        filt_b1=jnp.tile(filt_b1, (1, 2)).reshape(depth, 1, 2 * FILTER_ORDER),
        filt_w2=twice(filt_w2), filt_b2=jnp.tile(filt_b2, (1, 2)).reshape(depth, 1, 2 * FILTER_ORDER),
        filt_w3=twice(filt_w3), filt_b3=jnp.tile(filt_b3, (1, 2)).reshape(depth, 1, 2 * FILTER_ORDER),
        filt_freq=jnp.tile(filt_freq, (1, 2)).reshape(depth, 1, 2 * FILTER_ORDER),
        filt_w4=block_diag(filt_w4[:, :, :D_HYENA], filt_w4[:, :, D_HYENA:]),
        skip=hyena_skip.reshape(depth, 1, D_HYENA),
        w_br_attn=w_br_attn.astype(BF16), w_br_hyena=w_br_hyena.astype(BF16), w_out=w_out.astype(BF16),
        wr_hi=wr_hi, wr_lo=(route_w - wr_hi.astype(F32)).astype(BF16),
        b_route=jnp.pad(jnp.concatenate([b_router, b_group], axis=-1),
                        ((0, 0), (0, LANES - N_EXPERTS - N_GROUPS))).reshape(depth, 1, LANES),
        w_gu=jnp.concatenate([w_e_gate, w_e_up], axis=-1).astype(BF16),
        w_down=w_e_down.astype(BF16),
    )
    y_prompt = _run_trunk(x_prompt, mod[:, :bp], p)
    y_sample = _run_trunk(x_sample, mod[:, bp:bp + bs], p)
    return (y_prompt, y_sample)
```

```python
import functools
import math

import numpy as np
import jax
import jax.numpy as jnp
from jax import lax
from jax.experimental import pallas as pl
from jax.experimental.pallas import tpu as pltpu

F32 = jnp.float32
BF16 = jnp.bfloat16

D_MODEL = 1024
GRID_W = 64
N_HEADS = 8
N_KV_HEADS = 4
HEAD_DIM = 64
Q_PER_KV = N_HEADS // N_KV_HEADS
ATTN_WIDTH = N_HEADS * HEAD_DIM
KV_WIDTH = N_KV_HEADS * HEAD_DIM
ROPE_THETA = 10000.0
ROPE_PAIRS_PER_AXIS = HEAD_DIM // 4
D_HYENA = D_MODEL // 2
FILTER_EMB = 33
FILTER_BANDS = (FILTER_EMB - 1) // 2
FILTER_ORDER = 64
DECAY_TARGET = 1e-2
MAX_DECAY = math.log(DECAY_TARGET) / 0.3
MIN_DECAY = math.log(DECAY_TARGET) / 1.5
MOD_SHIFT = 0.05
N_GROUPS = 4
EXPERTS_PER_GROUP = 8
N_EXPERTS = N_GROUPS * EXPERTS_PER_GROUP
TOP_K = 2
D_EXPERT = D_MODEL // 4
RMS_EPS = 1e-6
IN_COLS = ATTN_WIDTH + 2 * KV_WIDTH + 3 * D_HYENA + 2 * D_MODEL
COL_K = ATTN_WIDTH
COL_V = ATTN_WIDTH + KV_WIDTH
COL_U = ATTN_WIDTH + 2 * KV_WIDTH
COL_G = COL_U + 3 * D_HYENA

LANES = 128
SUBLANES = 8
ROW_TILES = D_MODEL // 2 // LANES
VMEM_LIMIT = 56 * 1024 * 1024

V_ROWS = HEAD_DIM + 16
SCORE_BOUND = 60.0
FFT_N2 = 128
FFT_PITCH = FFT_N2 + SUBLANES
EXPERT_BLOCK = 512
NEG_BIG = -1e30

_HI = lax.Precision.HIGHEST


def _dot32(a, b):
    return jnp.dot(a, b, precision=_HI, preferred_element_type=F32)


def _dotbf(a, b):
    return jnp.dot(a, b, preferred_element_type=F32)


def _sigmoid(x):
    return 0.5 * jnp.tanh(0.5 * x) + 0.5


def _cparams(sem):
    return pltpu.CompilerParams(dimension_semantics=sem, vmem_limit_bytes=VMEM_LIMIT)


def _ada_kernel(c_ref, w_ref, b_ref, o_ref):
    c = c_ref[...]
    act = c * jax.nn.sigmoid(c)
    o_ref[0] = _dot32(act, w_ref[0]) + b_ref[0]


def _ada(c_pad, w_ada, b_ada):
    depth, d, n = w_ada.shape
    tn = 1536
    return pl.pallas_call(
        _ada_kernel,
        out_shape=jax.ShapeDtypeStruct((depth, c_pad.shape[0], n), F32),
        grid=(depth, n // tn),
        in_specs=[pl.BlockSpec(c_pad.shape, lambda l, j: (0, 0)),
                  pl.BlockSpec((1, d, tn), lambda l, j: (l, 0, j)),
                  pl.BlockSpec((1, 1, tn), lambda l, j: (l, 0, j))],
        out_specs=pl.BlockSpec((1, c_pad.shape[0], tn), lambda l, j: (l, 0, j)),
        compiler_params=_cparams(("arbitrary", "arbitrary")),
        name="ada_mod",
    )(c_pad, w_ada, b_ada.reshape(depth, 1, n))


def _swap_pairs(x):
    n = x.shape[-1]
    lane = lax.broadcasted_iota(jnp.int32, x.shape, 1)
    nxt = pltpu.roll(x, n - 1, 1)
    prv = pltpu.roll(x, 1, 1)
    return jnp.where(lane % 2 == 0, nxt, prv)


def _head_norm_rope(p, gain, bd, cos, sin_signed):
    sq = (p * p).astype(BF16)
    ms = _dotbf(sq, bd) * (1.0 / HEAD_DIM)
    pn = p * lax.rsqrt(ms + RMS_EPS) * gain
    return pn * cos + _swap_pairs(pn) * sin_signed


def _store_groups(ref, x):
    pad = jnp.zeros((FFT_PITCH - FFT_N2, x.shape[1]), x.dtype)
    for g in range(ref.shape[0]):
        ref[g, 0:FFT_N2, :] = x[g * FFT_N2:(g + 1) * FFT_N2]
        ref[g, FFT_N2:FFT_PITCH, :] = pad


def _inproj_kernel(x_ref, xp_ref, xn_ref, shift_ref, scale_ref, gain_ref, w_ref, wvt_ref, qg_ref, kg_ref,
                   bd_ref, cos_ref, sin_ref, cw_ref, cb_ref, q_ref, k_ref, vt_ref, z_ref, x2_ref, g_ref, *, tps):
    def normed(x):
        ms = jnp.mean(x * x, axis=-1, keepdims=True)
        h = x * lax.rsqrt(ms + RMS_EPS) * gain_ref[...]
        return (h * (1.0 + scale_ref[0]) + shift_ref[0]).astype(BF16)

    hb = normed(x_ref[...])
    tm = hb.shape[0]
    cos = cos_ref[...]
    sin = sin_ref[...]
    bd = bd_ref[...]

    q = _dotbf(hb, w_ref[:, 0:COL_K])
    q = _head_norm_rope(q, qg_ref[...], bd, cos, sin)
    zpad = jnp.zeros((tm, LANES - HEAD_DIM), BF16)
    for hd in range(N_HEADS):
        q_ref[0, hd, :, 0:HEAD_DIM] = q[:, hd * HEAD_DIM:(hd + 1) * HEAD_DIM].astype(BF16)
        q_ref[0, hd, :, HEAD_DIM:LANES] = zpad

    k = _dotbf(hb, w_ref[:, COL_K:COL_V])
    k = _head_norm_rope(k, kg_ref[...], bd[:KV_WIDTH, :KV_WIDTH], cos[:, :KV_WIDTH], sin[:, :KV_WIDTH])
    vt = lax.dot_general(wvt_ref[...], hb, (((1,), (1,)), ((), ())), preferred_element_type=F32)
    ones = jnp.ones((V_ROWS - HEAD_DIM, vt.shape[1]), BF16)
    for hd in range(N_KV_HEADS):
        k_ref[0, hd, :, 0:HEAD_DIM] = k[:, hd * HEAD_DIM:(hd + 1) * HEAD_DIM].astype(BF16)
        k_ref[0, hd, :, HEAD_DIM:LANES] = zpad
        vt_ref[0, hd, 0:HEAD_DIM, :] = vt[hd * HEAD_DIM:(hd + 1) * HEAD_DIM].astype(BF16)
        vt_ref[0, hd, HEAD_DIM:V_ROWS, :] = ones

    g_ref[...] = _sigmoid(_dotbf(hb, w_ref[:, COL_G:IN_COLS])).astype(BF16)

    halo = normed(jnp.concatenate([xp_ref[...], xn_ref[...]], axis=0))
    u_all = _dotbf(jnp.concatenate([hb, halo], axis=0), w_ref[:, COL_U:COL_G])
    u = u_all[:tm]
    pos = pl.program_id(0) % tps
    prev_row = u_all[tm + SUBLANES - 1:tm + SUBLANES] * (pos > 0).astype(F32)
    next_row = u_all[tm + SUBLANES:tm + SUBLANES + 1] * (pos < tps - 1).astype(F32)
    ridx = lax.broadcasted_iota(jnp.int32, u.shape, 0)
    u_prev = jnp.where(ridx == 0, prev_row, pltpu.roll(u, 1, 0))
    u_next = jnp.where(ridx == tm - 1, next_row, pltpu.roll(u, tm - 1, 0))
    w = cw_ref[...]
    c = cb_ref[...] + u_prev * w[0:1] + u * w[1:2] + u_next * w[2:3]
    _store_groups(z_ref.at[0], c[:, 2 * D_HYENA:3 * D_HYENA] * c[:, 0:D_HYENA])
    _store_groups(x2_ref.at[0], c[:, D_HYENA:2 * D_HYENA])


def _inproj(x, shift, scale, gain, w_in, w_vt, qg, kg, bd, cos, sin, conv_w, conv_b, nb, seq):
    t = nb * seq
    tm = min(512, seq)
    tps = seq // tm
    rpb = tm // SUBLANES
    row = lambda i: (i, 0)
    per_b = lambda i: (i // tps, 0, 0)
    const2 = lambda i: (0, 0)
    pos = lambda i: (i % tps, 0)
    head_out = lambda i: (i // tps, 0, i % tps, 0)
    return pl.pallas_call(
        functools.partial(_inproj_kernel, tps=tps),
        out_shape=(jax.ShapeDtypeStruct((nb, N_HEADS, seq, LANES), BF16),
                   jax.ShapeDtypeStruct((nb, N_KV_HEADS, seq, LANES), BF16),
                   jax.ShapeDtypeStruct((nb, N_KV_HEADS, V_ROWS, seq), BF16),
                   jax.ShapeDtypeStruct((nb, seq // FFT_N2, FFT_PITCH, D_HYENA), F32),
                   jax.ShapeDtypeStruct((nb, seq // FFT_N2, FFT_PITCH, D_HYENA), F32),
                   jax.ShapeDtypeStruct((t, 2 * D_MODEL), BF16)),
        grid=(t // tm,),
        in_specs=[pl.BlockSpec((tm, D_MODEL), row),
                  pl.BlockSpec((SUBLANES, D_MODEL), lambda i: (jnp.maximum(i * rpb - 1, 0), 0)),
                  pl.BlockSpec((SUBLANES, D_MODEL), lambda i: (jnp.minimum((i + 1) * rpb, t // SUBLANES - 1), 0)),
                  pl.BlockSpec((1, 1, D_MODEL), per_b),
                  pl.BlockSpec((1, 1, D_MODEL), per_b),
                  pl.BlockSpec((1, D_MODEL), const2),
                  pl.BlockSpec((D_MODEL, IN_COLS), const2),
                  pl.BlockSpec((KV_WIDTH, D_MODEL), const2),
                  pl.BlockSpec((1, ATTN_WIDTH), const2),
                  pl.BlockSpec((1, KV_WIDTH), const2),
                  pl.BlockSpec((ATTN_WIDTH, ATTN_WIDTH), const2),
                  pl.BlockSpec((tm, ATTN_WIDTH), pos),
                  pl.BlockSpec((tm, ATTN_WIDTH), pos),
                  pl.BlockSpec((3, 3 * D_HYENA), const2),
                  pl.BlockSpec((1, 3 * D_HYENA), const2)],
        out_specs=(pl.BlockSpec((1, N_HEADS, tm, LANES), head_out),
                   pl.BlockSpec((1, N_KV_HEADS, tm, LANES), head_out),
                   pl.BlockSpec((1, N_KV_HEADS, V_ROWS, tm), lambda i: (i // tps, 0, 0, i % tps)),
                   pl.BlockSpec((1, tm // FFT_N2, FFT_PITCH, D_HYENA), lambda i: (i // tps, i % tps, 0, 0)),
                   pl.BlockSpec((1, tm // FFT_N2, FFT_PITCH, D_HYENA), lambda i: (i // tps, i % tps, 0, 0)),
                   pl.BlockSpec((tm, 2 * D_MODEL), row)),
        compiler_params=_cparams(("arbitrary",)),
        name="in_proj",
    )(x, x, x, shift, scale, gain, w_in, w_vt, qg, kg, bd, cos, sin, conv_w, conv_b)


def _attn_kernel(q_ref, k_ref, vt_ref, o_ref, m_ref, acc_ref, *, tkc, online):
    seq = k_ref.shape[2]
    nchunk = seq // tkc
    acc_ref[...] = jnp.zeros(acc_ref.shape, F32)
    if online:
        m_ref[...] = jnp.full(m_ref.shape, NEG_BIG, F32)

    def body(c, carry):
        off = pl.multiple_of(c * tkc, tkc)
        kc = k_ref[0, 0, pl.ds(off, tkc), :]
        vc = vt_ref[0, 0, :, pl.ds(off, tkc)]
        for h in range(Q_PER_KV):
            st = lax.dot_general(kc, q_ref[0, h], (((1,), (1,)), ((), ())), preferred_element_type=F32)
            if online:
                m_prev = m_ref[h]
                m_new = jnp.maximum(m_prev, jnp.max(st, axis=0, keepdims=True))
                p = jnp.exp2(st - m_new).astype(BF16)
                acc_ref[h] = jnp.exp2(m_prev - m_new) * acc_ref[h] + _dotbf(vc, p)
                m_ref[h] = m_new
            else:
                acc_ref[h] += _dotbf(vc, jnp.exp2(st).astype(BF16))
        return carry

    lax.fori_loop(0, nchunk, body, 0, unroll=2 if nchunk % 2 == 0 else 1)
    outs = []
    for h in range(Q_PER_KV):
        a = acc_ref[h]
        outs.append((a[:HEAD_DIM] / a[HEAD_DIM:HEAD_DIM + 1]).T)
    o_ref[0] = jnp.concatenate(outs, axis=1).astype(BF16)


def _attention(q, k, vt, online):
    nb, _, seq, _ = q.shape
    tq = min(512 if online else 1024, seq)
    tkc = min(512 if online else 1024, seq)
    return pl.pallas_call(
        functools.partial(_attn_kernel, tkc=tkc, online=online),
        out_shape=jax.ShapeDtypeStruct((nb, seq, ATTN_WIDTH), BF16),
        grid=(nb, N_KV_HEADS, seq // tq),
        in_specs=[pl.BlockSpec((1, Q_PER_KV, tq, LANES), lambda b, g, i: (b, g, i, 0)),
                  pl.BlockSpec((1, 1, seq, LANES), lambda b, g, i: (b, g, 0, 0)),
                  pl.BlockSpec((1, 1, V_ROWS, seq), lambda b, g, i: (b, g, 0, 0))],
        out_specs=pl.BlockSpec((1, tq, Q_PER_KV * HEAD_DIM), lambda b, g, i: (b, i, g)),
        scratch_shapes=[pltpu.VMEM((Q_PER_KV, 1, tq), F32),
                        pltpu.VMEM((Q_PER_KV, V_ROWS, tq), F32)],
        compiler_params=_cparams(("arbitrary", "arbitrary", "arbitrary")),
        name="attn_online" if online else "attn_bounded",
    )(q, k, vt)


def _filter_kernel(z_ref, w1_ref, b1_ref, w2_ref, b2_ref, w3_ref, b3_ref, fr_ref, w4_ref,
                   dl_ref, o_ref):
    z = z_ref[...]
    fr = fr_ref[...]
    h = jnp.sin(fr * (_dot32(z, w1_ref[...]) + b1_ref[...]))
    h = jnp.sin(fr * (_dot32(h, w2_ref[...]) + b2_ref[...]))
    h = jnp.sin(fr * (_dot32(h, w3_ref[...]) + b3_ref[...]))
    h = _dot32(h, w4_ref[...])
    dl = dl_ref[...]
    for d in range(2):
        t = z[:, d * FILTER_ORDER:d * FILTER_ORDER + 1]
        sign = z[:, d * FILTER_ORDER + FILTER_EMB:d * FILTER_ORDER + FILTER_EMB + 1]
        _store_groups(o_ref.at[d], h[:, d * D_HYENA:(d + 1) * D_HYENA] * (jnp.exp(-t * dl) + MOD_SHIFT) * sign)


def _filters(zfeat, w1, b1, w2, b2, w3, b3, fr, w4, deltas):
    seq, fe = zfeat.shape
    tm = min(512, seq)
    c2 = lambda i: (0, 0)
    return pl.pallas_call(
        _filter_kernel,
        out_shape=jax.ShapeDtypeStruct((2, seq // FFT_N2, FFT_PITCH, D_HYENA), F32),
        grid=(seq // tm,),
        in_specs=[pl.BlockSpec((tm, fe), lambda i: (i, 0)),
                  pl.BlockSpec(w1.shape, c2), pl.BlockSpec(b1.shape, c2),
                  pl.BlockSpec(w2.shape, c2), pl.BlockSpec(b2.shape, c2),
                  pl.BlockSpec(w3.shape, c2), pl.BlockSpec(b3.shape, c2),
                  pl.BlockSpec(fr.shape, c2), pl.BlockSpec(w4.shape, c2),
                  pl.BlockSpec(deltas.shape, c2)],
        out_specs=pl.BlockSpec((2, tm // FFT_N2, FFT_PITCH, D_HYENA), lambda i: (0, i, 0, 0)),
        compiler_params=_cparams(("arbitrary",)),
        name="hyena_filters",
    )(zfeat, w1, b1, w2, b2, w3, b3, fr, w4, deltas)


def _zero_pad_rows(o_ref, groups):
    zeros = jnp.zeros((groups, o_ref.shape[1]), o_ref.dtype)
    for s in range(FFT_N2, FFT_PITCH):
        o_ref[pl.ds(s, groups, stride=FFT_PITCH), :] = zeros


def _fft_outer_kernel(g_ref, x_ref, o_ref):
    g = g_ref[...]
    m, kk = g.shape

    def body(jp, c):
        j = 2 * jp
        xj = jnp.concatenate([x_ref[pl.ds(j, kk, stride=FFT_PITCH), :],
                              x_ref[pl.ds(j + 1, kk, stride=FFT_PITCH), :]], axis=1)
        r = _dotbf(g, xj.astype(BF16))
        o_ref[pl.ds(j, m, stride=FFT_PITCH), :] = r[:, :LANES]
        o_ref[pl.ds(j + 1, m, stride=FFT_PITCH), :] = r[:, LANES:]
        return c

    lax.fori_loop(0, FFT_N2 // 2, body, 0, unroll=4)
    _zero_pad_rows(o_ref, m)


def _fft_outer(gmat, x):
    nb, rows, ch = x.shape
    m, kk = gmat.shape
    return pl.pallas_call(
        _fft_outer_kernel,
        out_shape=jax.ShapeDtypeStruct((nb, m * FFT_PITCH, ch), F32),
        grid=(nb, ch // LANES),
        in_specs=[pl.BlockSpec((m, kk), lambda b, c: (0, 0)),
                  pl.BlockSpec((None, rows, LANES), lambda b, c: (b, 0, c))],
        out_specs=pl.BlockSpec((None, m * FFT_PITCH, LANES), lambda b, c: (b, 0, c)),
        compiler_params=_cparams(("arbitrary", "arbitrary")),
        name="fft_outer",
    )(gmat, x)


FFT_K1_PER_STEP = 4


def _fft_filter_inner_kernel(mf_ref, a_ref, o_ref):
    n2, ch = FFT_N2, a_ref.shape[4]
    for r in range(a_ref.shape[2]):
        a = a_ref[0, :, r, 0:n2, :].reshape(2 * n2, ch)
        o_ref[r] = _dotbf(mf_ref[r], a.astype(BF16)).reshape(2, n2, ch)


def _fft_filter_inner(mf, a5):
    _, _, h1, pitch, ch = a5.shape
    n2 = FFT_N2
    kr = FFT_K1_PER_STEP
    return pl.pallas_call(
        _fft_filter_inner_kernel,
        out_shape=jax.ShapeDtypeStruct((h1, 2, n2, ch), F32),
        grid=(h1 // kr,),
        in_specs=[pl.BlockSpec((kr, 2 * n2, 2 * n2), lambda k: (k, 0, 0)),
                  pl.BlockSpec((1, 2, kr, pitch, ch), lambda k: (0, 0, k, 0, 0))],
        out_specs=pl.BlockSpec((kr, 2, n2, ch), lambda k: (k, 0, 0, 0)),
        compiler_params=_cparams(("arbitrary",)),
        name="fft_filter_inner",
    )(mf, a5)


def _fft_inner_kernel(mf_ref, mi_ref, kf_ref, a_ref, o_ref):
    n2, pitch, ch = FFT_N2, a_ref.shape[3], a_ref.shape[4]
    for r in range(a_ref.shape[2]):
        a = a_ref[0, :, r, 0:n2, :].reshape(2 * n2, ch)
        xs = _dotbf(mf_ref[r], a.astype(BF16))
        xr, xi = xs[:n2], xs[n2:]
        kr, ki = kf_ref[r, 0], kf_ref[r, 1]
        p = jnp.concatenate([xr * kr - xi * ki, xr * ki + xi * kr], axis=0)
        o_ref[0, :, r, 0:n2, :] = _dotbf(mi_ref[r], p.astype(BF16)).reshape(2, n2, ch)
        o_ref[0, :, r, n2:pitch, :] = jnp.zeros((2, pitch - n2, ch), F32)


def _fft_inner(mf, mi, kf, a5):
    nb, _, h1, pitch, ch = a5.shape
    n2 = FFT_N2
    kr = FFT_K1_PER_STEP
    return pl.pallas_call(
        _fft_inner_kernel,
        out_shape=jax.ShapeDtypeStruct(a5.shape, F32),
        grid=(h1 // kr, nb),
        in_specs=[pl.BlockSpec((kr, 2 * n2, 2 * n2), lambda k, b: (k, 0, 0)),
                  pl.BlockSpec((kr, 2 * n2, 2 * n2), lambda k, b: (k, 0, 0)),
                  pl.BlockSpec((kr, 2, n2, ch), lambda k, b: (k, 0, 0, 0)),
                  pl.BlockSpec((1, 2, kr, pitch, ch), lambda k, b: (b, 0, k, 0, 0))],
        out_specs=pl.BlockSpec((1, 2, kr, pitch, ch), lambda k, b: (b, 0, k, 0, 0)),
        compiler_params=_cparams(("arbitrary", "arbitrary")),
        name="fft_inner",
    )(mf, mi, kf, a5)


def _fft_final_kernel(g_ref, b_ref, z_ref, x2_ref, skip_ref, o_ref):
    g = g_ref[...]
    h1, m = g.shape
    skip = skip_ref[...]

    def body(jp, c):
        j = 2 * jp
        bj = jnp.concatenate([b_ref[pl.ds(j, m, stride=FFT_PITCH), :],
                              b_ref[pl.ds(j + 1, m, stride=FFT_PITCH), :]], axis=1)
        y = _dotbf(g, bj.astype(BF16))
        for d in range(2):
            zj = z_ref[pl.ds(j + d, h1, stride=FFT_PITCH), :]
            xj = x2_ref[pl.ds(j + d, h1, stride=FFT_PITCH), :]
            o_ref[pl.ds(j + d, h1, stride=FFT_PITCH), :] = xj * (y[:, d * LANES:(d + 1) * LANES] + zj * skip)
        return c

    lax.fori_loop(0, FFT_N2 // 2, body, 0, unroll=4)
    _zero_pad_rows(o_ref, h1)


def _fft_final(gc, bm, z, x2, skip):
    nb, rows, ch = z.shape
    h1, m = gc.shape
    blk = lambda r: pl.BlockSpec((None, r, LANES), lambda b, c: (b, 0, c))
    return pl.pallas_call(
        _fft_final_kernel,
        out_shape=jax.ShapeDtypeStruct((nb, rows, ch), F32),
        grid=(nb, ch // LANES),
        in_specs=[pl.BlockSpec((h1, m), lambda b, c: (0, 0)), blk(m * FFT_PITCH), blk(rows), blk(rows),
                  pl.BlockSpec((1, LANES), lambda b, c: (0, c))],
        out_specs=blk(rows),
        compiler_params=_cparams(("arbitrary", "arbitrary")),
        name="fft_final",
    )(gc, bm, z, x2, skip)


def _dft_tables(seq):
    n = 2 * seq
    n2 = FFT_N2
    n1 = n // n2
    h1 = n1 // 2
    i32 = jnp.int32
    k1 = jnp.arange(h1, dtype=i32)[:, None]
    a1 = jnp.arange(n1, dtype=i32)[None, :]
    th = (math.pi / n1) * ((a1 * (2 * k1 + 1)) % (2 * n1)).astype(F32)
    g_re, g_im = jnp.cos(th), -jnp.sin(th)
    g_full = jnp.concatenate([g_re, g_im], axis=0)
    g_half = g_full[:, :h1]
    g_out = (2.0 / n) * jnp.concatenate([g_re[:, :h1].T, g_im[:, :h1].T], axis=1)
    k2 = jnp.arange(n2, dtype=i32)[None, :, None]
    b2 = jnp.arange(n2, dtype=i32)[None, None, :]
    kk = jnp.arange(h1, dtype=i32)[:, None, None]
    ph = (math.pi / n) * ((b2 * (2 * kk + 1 + 2 * n1 * k2)) % (2 * n)).astype(F32)
    m_re, m_im = jnp.cos(ph), -jnp.sin(ph)
    mf = jnp.concatenate([jnp.concatenate([m_re, -m_im], axis=2),
                          jnp.concatenate([m_im, m_re], axis=2)], axis=1)
    mt_re, mt_im = jnp.transpose(m_re, (0, 2, 1)), jnp.transpose(m_im, (0, 2, 1))
    mi = jnp.concatenate([jnp.concatenate([mt_re, mt_im], axis=2),
                          jnp.concatenate([-mt_im, mt_re], axis=2)], axis=1)
    b16 = lambda a: a.astype(BF16)
    return dict(n1=n1, h1=h1, g_full=b16(g_full), g_half=b16(g_half), g_out=b16(g_out), mf=b16(mf), mi=b16(mi))


def _merge_kernel(attn_ref, hy_ref, g_ref, x_ref, gate_ref, shift_ref, scale_ref, gain_ref,
                  wa_ref, wh_ref, wo_ref, wrh_ref, wrl_ref, br_ref,
                  x1_ref, h2_ref, route_ref, cnt_ref):
    g = g_ref[...].astype(F32)
    hy = jnp.concatenate([hy_ref[grp, 0:FFT_N2, :] for grp in range(hy_ref.shape[0])], axis=0)
    merged = (g[:, :D_MODEL] * _dotbf(attn_ref[...], wa_ref[...])
              + g[:, D_MODEL:] * _dotbf(hy.astype(BF16), wh_ref[...]))
    mix = _dotbf(merged.astype(BF16), wo_ref[...])
    x1 = x_ref[...] + gate_ref[0] * mix
    x1_ref[...] = x1

    ms = jnp.mean(x1 * x1, axis=-1, keepdims=True)
    h2 = x1 * lax.rsqrt(ms + RMS_EPS) * gain_ref[...]
    h2 = h2 * (1.0 + scale_ref[0]) + shift_ref[0]
    _rows_to_tiles(h2_ref, _pack_halves(h2))

    hi = h2.astype(BF16)
    lo = (h2 - hi.astype(F32)).astype(BF16)
    wrh = wrh_ref[...]
    lg = _dotbf(hi, wrh) + _dotbf(lo, wrh) + _dotbf(hi, wrl_ref[...]) + br_ref[...]

    lane = lax.broadcasted_iota(jnp.int32, lg.shape, 1).astype(F32)
    is_grp = jnp.logical_and(lane >= N_EXPERTS, lane < N_EXPERTS + N_GROUPS)
    gm = jnp.where(is_grp, lg, NEG_BIG)
    gmax = jnp.max(gm, axis=-1, keepdims=True)
    gidx = jnp.min(jnp.where(gm == gmax, lane, 1e9), axis=-1, keepdims=True) - N_EXPERTS
    p_group = 1.0 / jnp.sum(jnp.where(is_grp, jnp.exp(gm - gmax), 0.0), axis=-1, keepdims=True)
    lo_lane = gidx * EXPERTS_PER_GROUP
    in_grp = jnp.logical_and(lane >= lo_lane, lane < lo_lane + EXPERTS_PER_GROUP)
    e1v = jnp.where(in_grp, lg, NEG_BIG)
    t1 = jnp.max(e1v, axis=-1, keepdims=True)
    i1 = jnp.min(jnp.where(e1v == t1, lane, 1e9), axis=-1, keepdims=True)
    e2v = jnp.where(lane == i1, NEG_BIG, e1v)
    t2 = jnp.max(e2v, axis=-1, keepdims=True)
    i2 = jnp.min(jnp.where(e2v == t2, lane, 1e9), axis=-1, keepdims=True)
    d = jnp.exp(t2 - t1)
    w1 = p_group / (1.0 + d)
    w2 = p_group * d / (1.0 + d)
    route_ref[...] = jnp.where(lane == 0, i1, jnp.where(lane == 1, i2,
                               jnp.where(lane == 2, w1, jnp.where(lane == 3, w2, 0.0))))

    onehot = (lane == i1).astype(F32) + (lane == i2).astype(F32)

    @pl.when(pl.program_id(0) == 0)
    def _():
        cnt_ref[...] = jnp.zeros(cnt_ref.shape, F32)

    cnt_ref[...] += jnp.sum(onehot, axis=0, keepdims=True)


def _merge(attn, hy, gates, x, gate1, shift2, scale2, gain, wa, wh, wo, wrh, wrl, br, nb, seq):
    t = nb * seq
    tm = min(512, seq)
    tps = seq // tm
    row = lambda i: (i, 0)
    per_b = lambda i: (i // tps, 0, 0)
    c2 = lambda i: (0, 0)
    return pl.pallas_call(
        _merge_kernel,
        out_shape=(jax.ShapeDtypeStruct((t, D_MODEL), F32),
                   jax.ShapeDtypeStruct((t * ROW_TILES, LANES), F32),
                   jax.ShapeDtypeStruct((t, LANES), F32),
                   jax.ShapeDtypeStruct((1, LANES), F32)),
        grid=(t // tm,),
        in_specs=[pl.BlockSpec((tm, ATTN_WIDTH), row),
                  pl.BlockSpec((tm // FFT_N2, FFT_PITCH, D_HYENA), lambda i: (i, 0, 0)),
                  pl.BlockSpec((tm, 2 * D_MODEL), row),
                  pl.BlockSpec((tm, D_MODEL), row),
                  pl.BlockSpec((1, 1, D_MODEL), per_b),
                  pl.BlockSpec((1, 1, D_MODEL), per_b),
                  pl.BlockSpec((1, 1, D_MODEL), per_b),
                  pl.BlockSpec((1, D_MODEL), c2),
                  pl.BlockSpec(wa.shape, c2), pl.BlockSpec(wh.shape, c2), pl.BlockSpec(wo.shape, c2),
                  pl.BlockSpec(wrh.shape, c2), pl.BlockSpec(wrl.shape, c2), pl.BlockSpec(br.shape, c2)],
        out_specs=(pl.BlockSpec((tm, D_MODEL), row),
                   pl.BlockSpec((tm * ROW_TILES, LANES), row),
                   pl.BlockSpec((tm, LANES), row),
                   pl.BlockSpec((1, LANES), c2)),
        compiler_params=_cparams(("arbitrary",)),
        name="merge_router",
    )(attn, hy, gates, x, gate1, shift2, scale2, gain, wa, wh, wo, wrh, wrl, br)


def _rank_kernel(route_ref, pstart_ref, tri_ref, dest_ref, carry_ref):
    @pl.when(pl.program_id(0) == 0)
    def _():
        carry_ref[...] = jnp.zeros(carry_ref.shape, F32)

    r = route_ref[...]
    lane = lax.broadcasted_iota(jnp.int32, r.shape, 1).astype(F32)
    oh1 = (lane == r[:, 0:1]).astype(F32)
    oh2 = (lane == r[:, 1:2]).astype(F32)
    tri = tri_ref[...]
    before1 = _dotbf(tri, oh1.astype(BF16))
    before2 = _dotbf(tri, oh2.astype(BF16))
    base1 = pstart_ref[...] + carry_ref[...]
    d1 = jnp.sum(oh1 * (base1 + before1), axis=-1, keepdims=True)
    base2 = base1 + jnp.sum(oh1, axis=0, keepdims=True)
    d2 = jnp.sum(oh2 * (base2 + before2), axis=-1, keepdims=True)
    carry_ref[...] = base2 + jnp.sum(oh2, axis=0, keepdims=True) - pstart_ref[...]
    dest_ref[...] = jnp.where(lane == 0, d1, jnp.where(lane == 1, d2, 0.0))


def _rank(route, pstart, tri):
    t = route.shape[0]
    tm = tri.shape[0]
    return pl.pallas_call(
        _rank_kernel,
        out_shape=jax.ShapeDtypeStruct((t, LANES), F32),
        grid=(t // tm,),
        in_specs=[pl.BlockSpec((tm, LANES), lambda i: (i, 0)),
                  pl.BlockSpec((1, LANES), lambda i: (0, 0)),
                  pl.BlockSpec((tm, tm), lambda i: (0, 0))],
        out_specs=pl.BlockSpec((tm, LANES), lambda i: (i, 0)),
        scratch_shapes=[pltpu.VMEM((1, LANES), F32)],
        compiler_params=_cparams(("arbitrary",)),
        name="moe_rank",
    )(route, pstart, tri)


def _row_tile(ref, r):
    return ref.at[pl.ds(pl.multiple_of(r * ROW_TILES, ROW_TILES), ROW_TILES)]


def _pack_halves(x):
    half = x.shape[1] // 2
    hi = lax.bitcast_convert_type(x[:, :half].astype(BF16).astype(F32), jnp.uint32)
    lo = lax.bitcast_convert_type(x[:, half:].astype(BF16).astype(F32), jnp.uint32)
    return lax.bitcast_convert_type(hi | (lo >> 16), F32)


def _unpack_halves(p):
    u = lax.bitcast_convert_type(p, jnp.uint32)
    hi = lax.bitcast_convert_type(u & jnp.uint32(0xFFFF0000), F32)
    lo = lax.bitcast_convert_type(u << 16, F32)
    return jnp.concatenate([hi, lo], axis=1)


def _rows_from_tiles(ref, n):
    return jnp.concatenate([ref[pl.ds(s, n, stride=ROW_TILES), :] for s in range(ROW_TILES)], axis=1)


def _rows_to_tiles(ref, x):
    n = x.shape[0]
    for s in range(ROW_TILES):
        ref[pl.ds(s, n, stride=ROW_TILES), :] = x[:, s * LANES:(s + 1) * LANES]


def _dispatch_kernel(d1_ref, d2_ref, h_ref, init_ref, xs_ref, sem):
    del init_ref
    tm = h_ref.shape[0] // ROW_TILES
    base = pl.program_id(0) * tm

    def copies(r):
        src = _row_tile(h_ref, r)
        return (pltpu.make_async_copy(src, _row_tile(xs_ref, d1_ref[base + r]), sem),
                pltpu.make_async_copy(src, _row_tile(xs_ref, d2_ref[base + r]), sem))

    def issue(r, c):
        a, b = copies(r)
        a.start(priority=0)
        b.start(priority=1)
        return c

    def drain(r, c):
        a, b = copies(r)
        a.wait()
        b.wait()
        return c

    lax.fori_loop(0, tm, issue, 0, unroll=8)
    lax.fori_loop(0, tm, drain, 0, unroll=8)


def _dispatch(d1, d2, h2, init):
    t = h2.shape[0] // ROW_TILES
    tm = min(256, t)
    return pl.pallas_call(
        _dispatch_kernel,
        out_shape=jax.ShapeDtypeStruct(init.shape, h2.dtype),
        grid_spec=pltpu.PrefetchScalarGridSpec(
            num_scalar_prefetch=2,
            grid=(t // tm,),
            in_specs=[pl.BlockSpec((tm * ROW_TILES, LANES), lambda i, a, b: (i, 0)),
                      pl.BlockSpec(memory_space=pl.ANY)],
            out_specs=pl.BlockSpec(memory_space=pl.ANY),
            scratch_shapes=[pltpu.SemaphoreType.DMA(())]),
        input_output_aliases={3: 0},
        compiler_params=_cparams(("arbitrary",)),
        name="moe_dispatch",
    )(d1, d2, h2, init)


def _expert_kernel(be_ref, nu_ref, xs_ref, wgu_ref, wd_ref, ys_ref):
    del be_ref

    @pl.when(pl.program_id(0) < nu_ref[0])
    def _():
        x = _unpack_halves(_rows_from_tiles(xs_ref, EXPERT_BLOCK)).astype(BF16)
        gu = _dotbf(x, wgu_ref[0])
        g = gu[:, :D_EXPERT]
        u = gu[:, D_EXPERT:]
        a = (g * _sigmoid(g) * u).astype(BF16)
        _rows_to_tiles(ys_ref, _pack_halves(_dotbf(a, wd_ref[0])))


def _experts(blk_expert, n_used, xs, wgu, wd):
    cap = xs.shape[0] // ROW_TILES
    d = wgu.shape[1]
    used = lambda i, nu: jnp.minimum(i, nu[0] - 1)
    blk = pl.BlockSpec((EXPERT_BLOCK * ROW_TILES, LANES), lambda i, be, nu: (used(i, nu), 0))
    return pl.pallas_call(
        _expert_kernel,
        out_shape=jax.ShapeDtypeStruct(xs.shape, F32),
        grid_spec=pltpu.PrefetchScalarGridSpec(
            num_scalar_prefetch=2,
            grid=(cap // EXPERT_BLOCK,),
            in_specs=[blk,
                      pl.BlockSpec((1, d, 2 * D_EXPERT), lambda i, be, nu: (be[used(i, nu)], 0, 0)),
                      pl.BlockSpec((1, D_EXPERT, d), lambda i, be, nu: (be[used(i, nu)], 0, 0))],
            out_specs=blk),
        input_output_aliases={2: 0},
        compiler_params=_cparams(("arbitrary",)),
        name="moe_experts",
    )(blk_expert, n_used, xs, wgu, wd)


def _combine_kernel(d1_ref, d2_ref, ys_ref, x_ref, route_ref, gate_ref, o_ref, y1_ref, y2_ref, sem):
    tm = x_ref.shape[0]
    i = pl.program_id(0)
    slot = i % 2

    def copies(step, s, r):
        base = step * tm
        return (pltpu.make_async_copy(_row_tile(ys_ref, d1_ref[base + r]), _row_tile(y1_ref.at[s], r), sem.at[s]),
                pltpu.make_async_copy(_row_tile(ys_ref, d2_ref[base + r]), _row_tile(y2_ref.at[s], r), sem.at[s]))

    def issue(step, s):
        def body(r, c):
            a, b = copies(step, s, r)
            a.start(priority=0)
            b.start(priority=1)
            return c

        lax.fori_loop(0, tm, body, 0, unroll=8)

    def drain(step, s):
        def body(r, c):
            a, b = copies(step, s, r)
            a.wait()
            b.wait()
            return c

        lax.fori_loop(0, tm, body, 0, unroll=8)

    @pl.when(i == 0)
    def _():
        issue(0, 0)

    @pl.when(i + 1 < pl.num_programs(0))
    def _():
        issue(i + 1, 1 - slot)

    drain(i, slot)
    r = route_ref[...]
    ffn = (_unpack_halves(_rows_from_tiles(y1_ref.at[slot], tm)) * r[:, 2:3]
           + _unpack_halves(_rows_from_tiles(y2_ref.at[slot], tm)) * r[:, 3:4])
    o_ref[...] = x_ref[...] + gate_ref[0] * ffn


def _combine(d1, d2, ys, x1, route, gate2, nb, seq):
    t, d = x1.shape
    tm = min(256, seq)
    tps = seq // tm
    return pl.pallas_call(
        _combine_kernel,
        out_shape=jax.ShapeDtypeStruct((t, d), F32),
        grid_spec=pltpu.PrefetchScalarGridSpec(
            num_scalar_prefetch=2,
            grid=(t // tm,),
            in_specs=[pl.BlockSpec(memory_space=pl.ANY),
                      pl.BlockSpec((tm, d), lambda i, a, b: (i, 0)),
                      pl.BlockSpec((tm, LANES), lambda i, a, b: (i, 0)),
                      pl.BlockSpec((1, 1, d), lambda i, a, b: (i // tps, 0, 0))],
            out_specs=pl.BlockSpec((tm, d), lambda i, a, b: (i, 0)),
            scratch_shapes=[pltpu.VMEM((2, tm * ROW_TILES, LANES), F32),
                            pltpu.VMEM((2, tm * ROW_TILES, LANES), F32),
                            pltpu.SemaphoreType.DMA((2,))]),
        compiler_params=_cparams(("arbitrary",)),
        name="moe_combine",
    )(d1, d2, ys, x1, route, gate2)


def _rope_tables(seq):
    rows = seq // GRID_W
    row = jnp.broadcast_to(jnp.arange(rows, dtype=F32)[:, None], (rows, GRID_W)).reshape(-1)
    col = jnp.broadcast_to(jnp.arange(GRID_W, dtype=F32)[None, :], (rows, GRID_W)).reshape(-1)
    inv_freq = ROPE_THETA ** (-jnp.arange(ROPE_PAIRS_PER_AXIS, dtype=F32) / ROPE_PAIRS_PER_AXIS)
    ang = jnp.concatenate([row[:, None] * inv_freq, col[:, None] * inv_freq], axis=-1)
    cos = jnp.repeat(jnp.cos(ang), 2, axis=-1)
    sin = jnp.repeat(jnp.sin(ang), 2, axis=-1)
    sign = jnp.tile(jnp.array([-1.0, 1.0], F32), HEAD_DIM // 2)
    return jnp.tile(cos, (1, N_HEADS)), jnp.tile(sin * sign, (1, N_HEADS))


def _filter_features(seq):
    t = jnp.linspace(0.0, 1.0, seq, dtype=F32)[:, None]
    w = (2.0 * math.pi / seq) * jnp.arange(seq, dtype=F32)[:, None]
    bands = jnp.linspace(1e-4, FILTER_BANDS - 1, FILTER_BANDS, dtype=F32)[None, :]
    z = jnp.concatenate([t, jnp.cos(bands * w), -jnp.sin(bands * w)], axis=-1)
    pad = jnp.zeros((seq, FILTER_ORDER - FILTER_EMB), F32)
    fwd = jnp.concatenate([z, pad], axis=-1).at[:, FILTER_EMB].set(1.0)
    rev = jnp.concatenate([z[:1], z[:0:-1]], axis=0)
    sign = jnp.concatenate([jnp.zeros((1,), F32), -jnp.ones((seq - 1,), F32)])
    bwd = jnp.concatenate([rev, pad], axis=-1).at[:, FILTER_EMB].set(sign)
    deltas = jnp.abs(jnp.linspace(MIN_DECAY, MAX_DECAY, D_HYENA, dtype=F32))[None, :]
    return jnp.concatenate([fwd, bwd], axis=-1), deltas


def _run_trunk(x, mod, p):
    nb, seq, d = x.shape
    t = nb * seq
    depth = mod.shape[0]
    cos, sin = _rope_tables(seq)
    zfeat, deltas = _filter_features(seq)
    tabs = _dft_tables(seq)
    h1 = tabs["h1"]
    tm_rank = min(512, t)
    tri = jnp.tril(jnp.ones((tm_rank, tm_rank), BF16), -1)
    cap = t * TOP_K + N_EXPERTS * EXPERT_BLOCK
    n_blk = cap // EXPERT_BLOCK
    lane_e = jnp.arange(LANES)

    xf = x.reshape(t, d)
    xs = jnp.zeros((cap * ROW_TILES, LANES), F32)
    for l in range(depth):
        m6 = mod[l].reshape(nb, 6, 1, d)
        shift1, scale1, gate1, shift2, scale2, gate2 = (m6[:, j] for j in range(6))

        q, k, vt, z, x2, gates = _inproj(xf, shift1, scale1, p["norm_mix"][l], p["w_in"][l], p["w_vt"][l],
                                         p["qg"][l], p["kg"][l], p["bd"], cos, sin,
                                         p["conv_w"][l], p["conv_b"][l], nb, seq)
        z = z.reshape(nb, h1 * FFT_PITCH, D_HYENA)
        x2 = x2.reshape(nb, h1 * FFT_PITCH, D_HYENA)
        bound = HEAD_DIM * jnp.max(jnp.abs(p["qg"][l])) * jnp.max(jnp.abs(p["kg"][l]))
        attn = lax.cond(bound <= SCORE_BOUND,
                        functools.partial(_attention, online=False),
                        functools.partial(_attention, online=True), q, k, vt).reshape(t, ATTN_WIDTH)

        kern = _filters(zfeat, p["filt_w1"][l], p["filt_b1"][l], p["filt_w2"][l], p["filt_b2"][l],
                        p["filt_w3"][l], p["filt_b3"][l], p["filt_freq"][l], p["filt_w4"][l], deltas)
        ka = _fft_outer(tabs["g_full"], kern.reshape(1, 2 * h1 * FFT_PITCH, D_HYENA))
        kf = _fft_filter_inner(tabs["mf"], ka.reshape(1, 2, h1, FFT_PITCH, D_HYENA))
        za = _fft_outer(tabs["g_half"], z)
        zb = _fft_inner(tabs["mf"], tabs["mi"], kf, za.reshape(nb, 2, h1, FFT_PITCH, D_HYENA))
        hy = _fft_final(tabs["g_out"], zb.reshape(nb, 2 * h1 * FFT_PITCH, D_HYENA), z, x2,
                        p["skip"][l]).reshape(nb * h1, FFT_PITCH, D_HYENA)

        x1, h2, route, counts = _merge(attn, hy, gates, xf, gate1, shift2, scale2, p["norm_ffn"][l],
                                       p["w_br_attn"][l], p["w_br_hyena"][l], p["w_out"][l],
                                       p["wr_hi"][l], p["wr_lo"][l], p["b_route"][l], nb, seq)

        cnt = counts[0].astype(jnp.int32)
        padded = jnp.where(lane_e < N_EXPERTS, (cnt + EXPERT_BLOCK - 1) // EXPERT_BLOCK * EXPERT_BLOCK, 0)
        pad_end = jnp.cumsum(padded)
        pstart = (pad_end - padded).astype(F32)[None, :]
        blk_start = jnp.arange(n_blk, dtype=jnp.int32) * EXPERT_BLOCK
        blk_expert = jnp.minimum(
            jnp.sum(blk_start[:, None] >= pad_end[None, :N_EXPERTS], axis=1), N_EXPERTS - 1).astype(jnp.int32)

        dest = _rank(route, pstart, tri)
        d1 = dest[:, 0].astype(jnp.int32)
        d2 = dest[:, 1].astype(jnp.int32)
        n_used = (pad_end[N_EXPERTS - 1:N_EXPERTS] // EXPERT_BLOCK).astype(jnp.int32)
        xs = _dispatch(d1, d2, h2, xs)
        xs = _experts(blk_expert, n_used, xs, p["w_gu"][l], p["w_down"][l])
        xf = _combine(d1, d2, xs, x1, route, gate2, nb, seq)
    return xf.reshape(nb, seq, d)


def kernel(x_prompt, x_sample, c_prompt, c_sample, w_ada, b_ada, norm_mix, norm_ffn, w_in, q_gain, k_gain, conv_w, conv_b, filt_w1, filt_b1, filt_w2, filt_b2, filt_w3, filt_b3, filt_freq, filt_w4, hyena_skip, w_br_attn, w_br_hyena, w_out, w_group, b_group, w_router, b_router, w_e_gate, w_e_up, w_e_down):
    depth = w_ada.shape[0]
    bp, bs = c_prompt.shape[0], c_sample.shape[0]
    rows = -(-(bp + bs) // SUBLANES) * SUBLANES
    c_pad = jnp.zeros((rows, D_MODEL), F32).at[:bp].set(c_prompt).at[bp:bp + bs].set(c_sample)
    mod = _ada(c_pad, w_ada, b_ada)

    scale = HEAD_DIM ** -0.5 * math.log2(math.e)
    head_id = np.arange(ATTN_WIDTH) // HEAD_DIM
    route_w = jnp.concatenate([w_router, w_group], axis=-1)
    route_w = jnp.pad(route_w, ((0, 0), (0, 0), (0, LANES - route_w.shape[-1])))
    wr_hi = route_w.astype(BF16)

    def block_diag(a, b):
        return jnp.concatenate([jnp.pad(a, ((0, 0), (0, 0), (0, b.shape[2]))),
                                jnp.pad(b, ((0, 0), (0, 0), (a.shape[2], 0)))], axis=1)

    twice = lambda w: block_diag(w, w)
    p = dict(
        norm_mix=norm_mix.reshape(depth, 1, D_MODEL),
        norm_ffn=norm_ffn.reshape(depth, 1, D_MODEL),
        w_in=w_in.astype(BF16),
        w_vt=jnp.swapaxes(w_in[:, :, COL_V:COL_U], 1, 2).astype(BF16),
        qg=(jnp.tile(q_gain, (1, N_HEADS)) * scale).reshape(depth, 1, ATTN_WIDTH),
        kg=jnp.tile(k_gain, (1, N_KV_HEADS)).reshape(depth, 1, KV_WIDTH),
        bd=jnp.asarray(head_id[:, None] == head_id[None, :], dtype=BF16),
        conv_w=conv_w, conv_b=conv_b.reshape(depth, 1, 3 * D_HYENA),
        filt_w1=twice(jnp.pad(filt_w1, ((0, 0), (0, FILTER_ORDER - FILTER_EMB), (0, 0)))),
        filt_b1=jnp.tile(filt_b1, (1, 2)).reshape(depth, 1, 2 * FILTER_ORDER),
        filt_w2=twice(filt_w2), filt_b2=jnp.tile(filt_b2, (1, 2)).reshape(depth, 1, 2 * FILTER_ORDER),
        filt_w3=twice(filt_w3), filt_b3=jnp.tile(filt_b3, (1, 2)).reshape(depth, 1, 2 * FILTER_ORDER),
        filt_freq=jnp.tile(filt_freq, (1, 2)).reshape(depth, 1, 2 * FILTER_ORDER),
        filt_w4=block_diag(filt_w4[:, :, :D_HYENA], filt_w4[:, :, D_HYENA:]),
        skip=hyena_skip.reshape(depth, 1, D_HYENA),
        w_br_attn=w_br_attn.astype(BF16), w_br_hyena=w_br_hyena.astype(BF16), w_out=w_out.astype(BF16),
        wr_hi=wr_hi, wr_lo=(route_w - wr_hi.astype(F32)).astype(BF16),
        b_route=jnp.pad(jnp.concatenate([b_router, b_group], axis=-1),
                        ((0, 0), (0, LANES - N_EXPERTS - N_GROUPS))).reshape(depth, 1, LANES),
        w_gu=jnp.concatenate([w_e_gate, w_e_up], axis=-1).astype(BF16),
        w_down=w_e_down.astype(BF16),
    )
    y_prompt = _run_trunk(x_prompt, mod[:, :bp], p)
    y_sample = _run_trunk(x_sample, mod[:, bp:bp + bs], p)
    return (y_prompt, y_sample)
```

```python
import functools
import math

import numpy as np
import jax
import jax.numpy as jnp
from jax import lax
from jax.experimental import pallas as pl
from jax.experimental.pallas import tpu as pltpu

F32 = jnp.float32
BF16 = jnp.bfloat16

D_MODEL = 1024
GRID_W = 64
N_HEADS = 8
N_KV_HEADS = 4
HEAD_DIM = 64
Q_PER_KV = N_HEADS // N_KV_HEADS
ATTN_WIDTH = N_HEADS * HEAD_DIM
KV_WIDTH = N_KV_HEADS * HEAD_DIM
ROPE_THETA = 10000.0
ROPE_PAIRS_PER_AXIS = HEAD_DIM // 4
D_HYENA = D_MODEL // 2
FILTER_EMB = 33
FILTER_BANDS = (FILTER_EMB - 1) // 2
FILTER_ORDER = 64
DECAY_TARGET = 1e-2
MAX_DECAY = math.log(DECAY_TARGET) / 0.3
MIN_DECAY = math.log(DECAY_TARGET) / 1.5
MOD_SHIFT = 0.05
N_GROUPS = 4
EXPERTS_PER_GROUP = 8
N_EXPERTS = N_GROUPS * EXPERTS_PER_GROUP
TOP_K = 2
D_EXPERT = D_MODEL // 4
RMS_EPS = 1e-6
IN_COLS = ATTN_WIDTH + 2 * KV_WIDTH + 3 * D_HYENA + 2 * D_MODEL
COL_K = ATTN_WIDTH
COL_V = ATTN_WIDTH + KV_WIDTH
COL_U = ATTN_WIDTH + 2 * KV_WIDTH
COL_G = COL_U + 3 * D_HYENA

LANES = 128
SUBLANES = 8
ROW_TILES = D_MODEL // 2 // LANES
VMEM_LIMIT = 56 * 1024 * 1024

V_ROWS = HEAD_DIM + 16
SCORE_BOUND = 60.0
FFT_N2 = 128
FFT_PITCH = FFT_N2 + SUBLANES
EXPERT_BLOCK = 512
NEG_BIG = -1e30

_HI = lax.Precision.HIGHEST


def _dot32(a, b):
    return jnp.dot(a, b, precision=_HI, preferred_element_type=F32)


def _dotbf(a, b):
    return jnp.dot(a, b, preferred_element_type=F32)


def _sigmoid(x):
    return 0.5 * jnp.tanh(0.5 * x) + 0.5


def _cparams(sem):
    return pltpu.CompilerParams(dimension_semantics=sem, vmem_limit_bytes=VMEM_LIMIT)


def _ada_kernel(c_ref, w_ref, b_ref, o_ref):
    c = c_ref[...]
    act = c * jax.nn.sigmoid(c)
    o_ref[0] = _dot32(act, w_ref[0]) + b_ref[0]


def _ada(c_pad, w_ada, b_ada):
    depth, d, n = w_ada.shape
    tn = 1536
    return pl.pallas_call(
        _ada_kernel,
        out_shape=jax.ShapeDtypeStruct((depth, c_pad.shape[0], n), F32),
        grid=(depth, n // tn),
        in_specs=[pl.BlockSpec(c_pad.shape, lambda l, j: (0, 0)),
                  pl.BlockSpec((1, d, tn), lambda l, j: (l, 0, j)),
                  pl.BlockSpec((1, 1, tn), lambda l, j: (l, 0, j))],
        out_specs=pl.BlockSpec((1, c_pad.shape[0], tn), lambda l, j: (l, 0, j)),
        compiler_params=_cparams(("arbitrary", "arbitrary")),
        name="ada_mod",
    )(c_pad, w_ada, b_ada.reshape(depth, 1, n))


def _swap_pairs(x):
    n = x.shape[-1]
    lane = lax.broadcasted_iota(jnp.int32, x.shape, 1)
    nxt = pltpu.roll(x, n - 1, 1)
    prv = pltpu.roll(x, 1, 1)
    return jnp.where(lane % 2 == 0, nxt, prv)


def _head_norm_rope(p, gain, bd, cos, sin_signed):
    sq = (p * p).astype(BF16)
    ms = _dotbf(sq, bd) * (1.0 / HEAD_DIM)
    pn = p * lax.rsqrt(ms + RMS_EPS) * gain
    return pn * cos + _swap_pairs(pn) * sin_signed


def _store_groups(ref, x):
    pad = jnp.zeros((FFT_PITCH - FFT_N2, x.shape[1]), x.dtype)
    for g in range(ref.shape[0]):
        ref[g, 0:FFT_N2, :] = x[g * FFT_N2:(g + 1) * FFT_N2]
        ref[g, FFT_N2:FFT_PITCH, :] = pad


def _inproj_kernel(x_ref, xp_ref, xn_ref, shift_ref, scale_ref, gain_ref, w_ref, wvt_ref, qg_ref, kg_ref,
                   bd_ref, cos_ref, sin_ref, cw_ref, cb_ref, q_ref, k_ref, vt_ref, z_ref, x2_ref, g_ref, *, tps):
    def normed(x):
        ms = jnp.mean(x * x, axis=-1, keepdims=True)
        h = x * lax.rsqrt(ms + RMS_EPS) * gain_ref[...]
        return (h * (1.0 + scale_ref[0]) + shift_ref[0]).astype(BF16)

    hb = normed(x_ref[...])
    tm = hb.shape[0]
    cos = cos_ref[...]
    sin = sin_ref[...]
    bd = bd_ref[...]

    q = _dotbf(hb, w_ref[:, 0:COL_K])
    q = _head_norm_rope(q, qg_ref[...], bd, cos, sin)
    zpad = jnp.zeros((tm, LANES - HEAD_DIM), BF16)
    for hd in range(N_HEADS):
        q_ref[0, hd, :, 0:HEAD_DIM] = q[:, hd * HEAD_DIM:(hd + 1) * HEAD_DIM].astype(BF16)
        q_ref[0, hd, :, HEAD_DIM:LANES] = zpad

    k = _dotbf(hb, w_ref[:, COL_K:COL_V])
    k = _head_norm_rope(k, kg_ref[...], bd[:KV_WIDTH, :KV_WIDTH], cos[:, :KV_WIDTH], sin[:, :KV_WIDTH])
    vt = lax.dot_general(wvt_ref[...], hb, (((1,), (1,)), ((), ())), preferred_element_type=F32)
    ones = jnp.ones((V_ROWS - HEAD_DIM, vt.shape[1]), BF16)
    for hd in range(N_KV_HEADS):
        k_ref[0, hd, :, 0:HEAD_DIM] = k[:, hd * HEAD_DIM:(hd + 1) * HEAD_DIM].astype(BF16)
        k_ref[0, hd, :, HEAD_DIM:LANES] = zpad
        vt_ref[0, hd, 0:HEAD_DIM, :] = vt[hd * HEAD_DIM:(hd + 1) * HEAD_DIM].astype(BF16)
        vt_ref[0, hd, HEAD_DIM:V_ROWS, :] = ones

    g_ref[...] = _sigmoid(_dotbf(hb, w_ref[:, COL_G:IN_COLS])).astype(BF16)

    halo = normed(jnp.concatenate([xp_ref[...], xn_ref[...]], axis=0))
    u_all = _dotbf(jnp.concatenate([hb, halo], axis=0), w_ref[:, COL_U:COL_G])
    u = u_all[:tm]
    pos = pl.program_id(0) % tps
    prev_row = u_all[tm + SUBLANES - 1:tm + SUBLANES] * (pos > 0).astype(F32)
    next_row = u_all[tm + SUBLANES:tm + SUBLANES + 1] * (pos < tps - 1).astype(F32)
    ridx = lax.broadcasted_iota(jnp.int32, u.shape, 0)
    u_prev = jnp.where(ridx == 0, prev_row, pltpu.roll(u, 1, 0))
    u_next = jnp.where(ridx == tm - 1, next_row, pltpu.roll(u, tm - 1, 0))
    w = cw_ref[...]
    c = cb_ref[...] + u_prev * w[0:1] + u * w[1:2] + u_next * w[2:3]
    _store_groups(z_ref.at[0], c[:, 2 * D_HYENA:3 * D_HYENA] * c[:, 0:D_HYENA])
    _store_groups(x2_ref.at[0], c[:, D_HYENA:2 * D_HYENA])


def _inproj(x, shift, scale, gain, w_in, w_vt, qg, kg, bd, cos, sin, conv_w, conv_b, nb, seq):
    t = nb * seq
    tm = min(512, seq)
    tps = seq // tm
    rpb = tm // SUBLANES
    row = lambda i: (i, 0)
    per_b = lambda i: (i // tps, 0, 0)
    const2 = lambda i: (0, 0)
    pos = lambda i: (i % tps, 0)
    head_out = lambda i: (i // tps, 0, i % tps, 0)
    return pl.pallas_call(
        functools.partial(_inproj_kernel, tps=tps),
        out_shape=(jax.ShapeDtypeStruct((nb, N_HEADS, seq, LANES), BF16),
                   jax.ShapeDtypeStruct((nb, N_KV_HEADS, seq, LANES), BF16),
                   jax.ShapeDtypeStruct((nb, N_KV_HEADS, V_ROWS, seq), BF16),
                   jax.ShapeDtypeStruct((nb, seq // FFT_N2, FFT_PITCH, D_HYENA), F32),
                   jax.ShapeDtypeStruct((nb, seq // FFT_N2, FFT_PITCH, D_HYENA), F32),
                   jax.ShapeDtypeStruct((t, 2 * D_MODEL), BF16)),
        grid=(t // tm,),
        in_specs=[pl.BlockSpec((tm, D_MODEL), row),
                  pl.BlockSpec((SUBLANES, D_MODEL), lambda i: (jnp.maximum(i * rpb - 1, 0), 0)),
                  pl.BlockSpec((SUBLANES, D_MODEL), lambda i: (jnp.minimum((i + 1) * rpb, t // SUBLANES - 1), 0)),
                  pl.BlockSpec((1, 1, D_MODEL), per_b),
                  pl.BlockSpec((1, 1, D_MODEL), per_b),
                  pl.BlockSpec((1, D_MODEL), const2),
                  pl.BlockSpec((D_MODEL, IN_COLS), const2),
                  pl.BlockSpec((KV_WIDTH, D_MODEL), const2),
                  pl.BlockSpec((1, ATTN_WIDTH), const2),
                  pl.BlockSpec((1, KV_WIDTH), const2),
                  pl.BlockSpec((ATTN_WIDTH, ATTN_WIDTH), const2),
                  pl.BlockSpec((tm, ATTN_WIDTH), pos),
                  pl.BlockSpec((tm, ATTN_WIDTH), pos),
                  pl.BlockSpec((3, 3 * D_HYENA), const2),
                  pl.BlockSpec((1, 3 * D_HYENA), const2)],
        out_specs=(pl.BlockSpec((1, N_HEADS, tm, LANES), head_out),
                   pl.BlockSpec((1, N_KV_HEADS, tm, LANES), head_out),
                   pl.BlockSpec((1, N_KV_HEADS, V_ROWS, tm), lambda i: (i // tps, 0, 0, i % tps)),
                   pl.BlockSpec((1, tm // FFT_N2, FFT_PITCH, D_HYENA), lambda i: (i // tps, i % tps, 0, 0)),
                   pl.BlockSpec((1, tm // FFT_N2, FFT_PITCH, D_HYENA), lambda i: (i // tps, i % tps, 0, 0)),
                   pl.BlockSpec((tm, 2 * D_MODEL), row)),
        compiler_params=_cparams(("arbitrary",)),
        name="in_proj",
    )(x, x, x, shift, scale, gain, w_in, w_vt, qg, kg, bd, cos, sin, conv_w, conv_b)


def _attn_kernel(q_ref, k_ref, vt_ref, o_ref, m_ref, acc_ref, *, tkc, online):
    seq = k_ref.shape[2]
    nchunk = seq // tkc
    acc_ref[...] = jnp.zeros(acc_ref.shape, F32)
    if online:
        m_ref[...] = jnp.full(m_ref.shape, NEG_BIG, F32)

    def body(c, carry):
        off = pl.multiple_of(c * tkc, tkc)
        kc = k_ref[0, 0, pl.ds(off, tkc), :]
        vc = vt_ref[0, 0, :, pl.ds(off, tkc)]
        for h in range(Q_PER_KV):
            st = lax.dot_general(kc, q_ref[0, h], (((1,), (1,)), ((), ())), preferred_element_type=F32)
            if online:
                m_prev = m_ref[h]
                m_new = jnp.maximum(m_prev, jnp.max(st, axis=0, keepdims=True))
                p = jnp.exp2(st - m_new).astype(BF16)
                acc_ref[h] = jnp.exp2(m_prev - m_new) * acc_ref[h] + _dotbf(vc, p)
                m_ref[h] = m_new
            else:
                acc_ref[h] += _dotbf(vc, jnp.exp2(st).astype(BF16))
        return carry

    lax.fori_loop(0, nchunk, body, 0, unroll=2 if nchunk % 2 == 0 else 1)
    outs = []
    for h in range(Q_PER_KV):
        a = acc_ref[h]
        outs.append((a[:HEAD_DIM] / a[HEAD_DIM:HEAD_DIM + 1]).T)
    o_ref[0] = jnp.concatenate(outs, axis=1).astype(BF16)


def _attention(q, k, vt, online):
    nb, _, seq, _ = q.shape
    tq = min(512 if online else 1024, seq)
    tkc = min(512 if online else 1024, seq)
    return pl.pallas_call(
        functools.partial(_attn_kernel, tkc=tkc, online=online),
        out_shape=jax.ShapeDtypeStruct((nb, seq, ATTN_WIDTH), BF16),
        grid=(nb, N_KV_HEADS, seq // tq),
        in_specs=[pl.BlockSpec((1, Q_PER_KV, tq, LANES), lambda b, g, i: (b, g, i, 0)),
                  pl.BlockSpec((1, 1, seq, LANES), lambda b, g, i: (b, g, 0, 0)),
                  pl.BlockSpec((1, 1, V_ROWS, seq), lambda b, g, i: (b, g, 0, 0))],
        out_specs=pl.BlockSpec((1, tq, Q_PER_KV * HEAD_DIM), lambda b, g, i: (b, i, g)),
        scratch_shapes=[pltpu.VMEM((Q_PER_KV, 1, tq), F32),
                        pltpu.VMEM((Q_PER_KV, V_ROWS, tq), F32)],
        compiler_params=_cparams(("arbitrary", "arbitrary", "arbitrary")),
        name="attn_online" if online else "attn_bounded",
    )(q, k, vt)


def _filter_kernel(z_ref, w1_ref, b1_ref, w2_ref, b2_ref, w3_ref, b3_ref, fr_ref, w4_ref,
                   dl_ref, o_ref):
    z = z_ref[...]
    fr = fr_ref[...]
    h = jnp.sin(fr * (_dot32(z, w1_ref[...]) + b1_ref[...]))
    h = jnp.sin(fr * (_dot32(h, w2_ref[...]) + b2_ref[...]))
    h = jnp.sin(fr * (_dot32(h, w3_ref[...]) + b3_ref[...]))
    h = _dot32(h, w4_ref[...])
    dl = dl_ref[...]
    for d in range(2):
        t = z[:, d * FILTER_ORDER:d * FILTER_ORDER + 1]
        sign = z[:, d * FILTER_ORDER + FILTER_EMB:d * FILTER_ORDER + FILTER_EMB + 1]
        _store_groups(o_ref.at[d], h[:, d * D_HYENA:(d + 1) * D_HYENA] * (jnp.exp(-t * dl) + MOD_SHIFT) * sign)


def _filters(zfeat, w1, b1, w2, b2, w3, b3, fr, w4, deltas):
    seq, fe = zfeat.shape
    tm = min(512, seq)
    c2 = lambda i: (0, 0)
    return pl.pallas_call(
        _filter_kernel,
        out_shape=jax.ShapeDtypeStruct((2, seq // FFT_N2, FFT_PITCH, D_HYENA), F32),
        grid=(seq // tm,),
        in_specs=[pl.BlockSpec((tm, fe), lambda i: (i, 0)),
                  pl.BlockSpec(w1.shape, c2), pl.BlockSpec(b1.shape, c2),
                  pl.BlockSpec(w2.shape, c2), pl.BlockSpec(b2.shape, c2),
                  pl.BlockSpec(w3.shape, c2), pl.BlockSpec(b3.shape, c2),
                  pl.BlockSpec(fr.shape, c2), pl.BlockSpec(w4.shape, c2),
                  pl.BlockSpec(deltas.shape, c2)],
        out_specs=pl.BlockSpec((2, tm // FFT_N2, FFT_PITCH, D_HYENA), lambda i: (0, i, 0, 0)),
        compiler_params=_cparams(("arbitrary",)),
        name="hyena_filters",
    )(zfeat, w1, b1, w2, b2, w3, b3, fr, w4, deltas)


def _zero_pad_rows(o_ref, groups):
    zeros = jnp.zeros((groups, o_ref.shape[1]), o_ref.dtype)
    for s in range(FFT_N2, FFT_PITCH):
        o_ref[pl.ds(s, groups, stride=FFT_PITCH), :] = zeros


def _fft_outer_kernel(g_ref, x_ref, o_ref):
    g = g_ref[...]
    m, kk = g.shape

    def body(jp, c):
        j = 2 * jp
        xj = jnp.concatenate([x_ref[pl.ds(j, kk, stride=FFT_PITCH), :],
                              x_ref[pl.ds(j + 1, kk, stride=FFT_PITCH), :]], axis=1)
        r = _dotbf(g, xj.astype(BF16))
        o_ref[pl.ds(j, m, stride=FFT_PITCH), :] = r[:, :LANES]
        o_ref[pl.ds(j + 1, m, stride=FFT_PITCH), :] = r[:, LANES:]
        return c

    lax.fori_loop(0, FFT_N2 // 2, body, 0, unroll=4)
    _zero_pad_rows(o_ref, m)


def _fft_outer(gmat, x):
    nb, rows, ch = x.shape
    m, kk = gmat.shape
    return pl.pallas_call(
        _fft_outer_kernel,
        out_shape=jax.ShapeDtypeStruct((nb, m * FFT_PITCH, ch), F32),
        grid=(nb, ch // LANES),
        in_specs=[pl.BlockSpec((m, kk), lambda b, c: (0, 0)),
                  pl.BlockSpec((None, rows, LANES), lambda b, c: (b, 0, c))],
        out_specs=pl.BlockSpec((None, m * FFT_PITCH, LANES), lambda b, c: (b, 0, c)),
        compiler_params=_cparams(("arbitrary", "arbitrary")),
        name="fft_outer",
    )(gmat, x)


FFT_K1_PER_STEP = 4


def _fft_filter_inner_kernel(mf_ref, a_ref, o_ref):
    n2, ch = FFT_N2, a_ref.shape[4]
    for r in range(a_ref.shape[2]):
        a = a_ref[0, :, r, 0:n2, :].reshape(2 * n2, ch)
        o_ref[r] = _dotbf(mf_ref[r], a.astype(BF16)).reshape(2, n2, ch)


def _fft_filter_inner(mf, a5):
    _, _, h1, pitch, ch = a5.shape
    n2 = FFT_N2
    kr = FFT_K1_PER_STEP
    return pl.pallas_call(
        _fft_filter_inner_kernel,
        out_shape=jax.ShapeDtypeStruct((h1, 2, n2, ch), F32),
        grid=(h1 // kr,),
        in_specs=[pl.BlockSpec((kr, 2 * n2, 2 * n2), lambda k: (k, 0, 0)),
                  pl.BlockSpec((1, 2, kr, pitch, ch), lambda k: (0, 0, k, 0, 0))],
        out_specs=pl.BlockSpec((kr, 2, n2, ch), lambda k: (k, 0, 0, 0)),
        compiler_params=_cparams(("arbitrary",)),
        name="fft_filter_inner",
    )(mf, a5)


def _fft_inner_kernel(mf_ref, mi_ref, kf_ref, a_ref, o_ref):
    n2, pitch, ch = FFT_N2, a_ref.shape[3], a_ref.shape[4]
    for r in range(a_ref.shape[2]):
        a = a_ref[0, :, r, 0:n2, :].reshape(2 * n2, ch)
        xs = _dotbf(mf_ref[r], a.astype(BF16))
        xr, xi = xs[:n2], xs[n2:]
        kr, ki = kf_ref[r, 0], kf_ref[r, 1]
        p = jnp.concatenate([xr * kr - xi * ki, xr * ki + xi * kr], axis=0)
        o_ref[0, :, r, 0:n2, :] = _dotbf(mi_ref[r], p.astype(BF16)).reshape(2, n2, ch)
        o_ref[0, :, r, n2:pitch, :] = jnp.zeros((2, pitch - n2, ch), F32)


def _fft_inner(mf, mi, kf, a5):
    nb, _, h1, pitch, ch = a5.shape
    n2 = FFT_N2
    kr = FFT_K1_PER_STEP
    return pl.pallas_call(
        _fft_inner_kernel,
        out_shape=jax.ShapeDtypeStruct(a5.shape, F32),
        grid=(h1 // kr, nb),
        in_specs=[pl.BlockSpec((kr, 2 * n2, 2 * n2), lambda k, b: (k, 0, 0)),
                  pl.BlockSpec((kr, 2 * n2, 2 * n2), lambda k, b: (k, 0, 0)),
                  pl.BlockSpec((kr, 2, n2, ch), lambda k, b: (k, 0, 0, 0)),
                  pl.BlockSpec((1, 2, kr, pitch, ch), lambda k, b: (b, 0, k, 0, 0))],
        out_specs=pl.BlockSpec((1, 2, kr, pitch, ch), lambda k, b: (b, 0, k, 0, 0)),
        compiler_params=_cparams(("arbitrary", "arbitrary")),
        name="fft_inner",
    )(mf, mi, kf, a5)


def _fft_final_kernel(g_ref, b_ref, z_ref, x2_ref, skip_ref, o_ref):
    g = g_ref[...]
    h1, m = g.shape
    skip = skip_ref[...]

    def body(jp, c):
        j = 2 * jp
        bj = jnp.concatenate([b_ref[pl.ds(j, m, stride=FFT_PITCH), :],
                              b_ref[pl.ds(j + 1, m, stride=FFT_PITCH), :]], axis=1)
        y = _dotbf(g, bj.astype(BF16))
        for d in range(2):
            zj = z_ref[pl.ds(j + d, h1, stride=FFT_PITCH), :]
            xj = x2_ref[pl.ds(j + d, h1, stride=FFT_PITCH), :]
            o_ref[pl.ds(j + d, h1, stride=FFT_PITCH), :] = xj * (y[:, d * LANES:(d + 1) * LANES] + zj * skip)
        return c

    lax.fori_loop(0, FFT_N2 // 2, body, 0, unroll=4)
    _zero_pad_rows(o_ref, h1)


def _fft_final(gc, bm, z, x2, skip):
    nb, rows, ch = z.shape
    h1, m = gc.shape
    blk = lambda r: pl.BlockSpec((None, r, LANES), lambda b, c: (b, 0, c))
    return pl.pallas_call(
        _fft_final_kernel,
        out_shape=jax.ShapeDtypeStruct((nb, rows, ch), F32),
        grid=(nb, ch // LANES),
        in_specs=[pl.BlockSpec((h1, m), lambda b, c: (0, 0)), blk(m * FFT_PITCH), blk(rows), blk(rows),
                  pl.BlockSpec((1, LANES), lambda b, c: (0, c))],
        out_specs=blk(rows),
        compiler_params=_cparams(("arbitrary", "arbitrary")),
        name="fft_final",
    )(gc, bm, z, x2, skip)


def _dft_tables(seq):
    n = 2 * seq
    n2 = FFT_N2
    n1 = n // n2
    h1 = n1 // 2
    i32 = jnp.int32
    k1 = jnp.arange(h1, dtype=i32)[:, None]
    a1 = jnp.arange(n1, dtype=i32)[None, :]
    th = (math.pi / n1) * ((a1 * (2 * k1 + 1)) % (2 * n1)).astype(F32)
    g_re, g_im = jnp.cos(th), -jnp.sin(th)
    g_full = jnp.concatenate([g_re, g_im], axis=0)
    g_half = g_full[:, :h1]
    g_out = (2.0 / n) * jnp.concatenate([g_re[:, :h1].T, g_im[:, :h1].T], axis=1)
    k2 = jnp.arange(n2, dtype=i32)[None, :, None]
    b2 = jnp.arange(n2, dtype=i32)[None, None, :]
    kk = jnp.arange(h1, dtype=i32)[:, None, None]
    ph = (math.pi / n) * ((b2 * (2 * kk + 1 + 2 * n1 * k2)) % (2 * n)).astype(F32)
    m_re, m_im = jnp.cos(ph), -jnp.sin(ph)
    mf = jnp.concatenate([jnp.concatenate([m_re, -m_im], axis=2),
                          jnp.concatenate([m_im, m_re], axis=2)], axis=1)
    mt_re, mt_im = jnp.transpose(m_re, (0, 2, 1)), jnp.transpose(m_im, (0, 2, 1))
    mi = jnp.concatenate([jnp.concatenate([mt_re, mt_im], axis=2),
                          jnp.concatenate([-mt_im, mt_re], axis=2)], axis=1)
    b16 = lambda a: a.astype(BF16)
    return dict(n1=n1, h1=h1, g_full=b16(g_full), g_half=b16(g_half), g_out=b16(g_out), mf=b16(mf), mi=b16(mi))


def _merge_kernel(attn_ref, hy_ref, g_ref, x_ref, gate_ref, shift_ref, scale_ref, gain_ref,
                  wa_ref, wh_ref, wo_ref, wrh_ref, wrl_ref, br_ref,
                  x1_ref, h2_ref, route_ref, cnt_ref):
    g = g_ref[...].astype(F32)
    hy = jnp.concatenate([hy_ref[grp, 0:FFT_N2, :] for grp in range(hy_ref.shape[0])], axis=0)
    merged = (g[:, :D_MODEL] * _dotbf(attn_ref[...], wa_ref[...])
              + g[:, D_MODEL:] * _dotbf(hy.astype(BF16), wh_ref[...]))
    mix = _dotbf(merged.astype(BF16), wo_ref[...])
    x1 = x_ref[...] + gate_ref[0] * mix
    x1_ref[...] = x1

    ms = jnp.mean(x1 * x1, axis=-1, keepdims=True)
    h2 = x1 * lax.rsqrt(ms + RMS_EPS) * gain_ref[...]
    h2 = h2 * (1.0 + scale_ref[0]) + shift_ref[0]
    _rows_to_tiles(h2_ref, _pack_halves(h2))

    hi = h2.astype(BF16)
    lo = (h2 - hi.astype(F32)).astype(BF16)
    wrh = wrh_ref[...]
    hw = _dotbf(hi, jnp.concatenate([wrh, wrl_ref[...]], axis=1))
    lg = hw[:, :LANES] + _dotbf(lo, wrh) + hw[:, LANES:] + br_ref[...]

    lane = lax.broadcasted_iota(jnp.int32, lg.shape, 1).astype(F32)
    is_grp = jnp.logical_and(lane >= N_EXPERTS, lane < N_EXPERTS + N_GROUPS)
    gm = jnp.where(is_grp, lg, NEG_BIG)
    gmax = jnp.max(gm, axis=-1, keepdims=True)
    gidx = jnp.min(jnp.where(gm == gmax, lane, 1e9), axis=-1, keepdims=True) - N_EXPERTS
    p_group = 1.0 / jnp.sum(jnp.where(is_grp, jnp.exp(gm - gmax), 0.0), axis=-1, keepdims=True)
    lo_lane = gidx * EXPERTS_PER_GROUP
    in_grp = jnp.logical_and(lane >= lo_lane, lane < lo_lane + EXPERTS_PER_GROUP)
    e1v = jnp.where(in_grp, lg, NEG_BIG)
    t1 = jnp.max(e1v, axis=-1, keepdims=True)
    i1 = jnp.min(jnp.where(e1v == t1, lane, 1e9), axis=-1, keepdims=True)
    e2v = jnp.where(lane == i1, NEG_BIG, e1v)
    t2 = jnp.max(e2v, axis=-1, keepdims=True)
    i2 = jnp.min(jnp.where(e2v == t2, lane, 1e9), axis=-1, keepdims=True)
    d = jnp.exp(t2 - t1)
    w1 = p_group / (1.0 + d)
    w2 = p_group * d / (1.0 + d)
    route_ref[...] = jnp.where(lane == 0, i1, jnp.where(lane == 1, i2,
                               jnp.where(lane == 2, w1, jnp.where(lane == 3, w2, 0.0))))

    onehot = (lane == i1).astype(F32) + (lane == i2).astype(F32)

    @pl.when(pl.program_id(0) == 0)
    def _():
        cnt_ref[...] = jnp.zeros(cnt_ref.shape, F32)

    cnt_ref[...] += jnp.sum(onehot, axis=0, keepdims=True)


def _merge(attn, hy, gates, x, gate1, shift2, scale2, gain, wa, wh, wo, wrh, wrl, br, nb, seq):
    t = nb * seq
    tm = min(512, seq)
    tps = seq // tm
    row = lambda i: (i, 0)
    per_b = lambda i: (i // tps, 0, 0)
    c2 = lambda i: (0, 0)
    return pl.pallas_call(
        _merge_kernel,
        out_shape=(jax.ShapeDtypeStruct((t, D_MODEL), F32),
                   jax.ShapeDtypeStruct((t * ROW_TILES, LANES), F32),
                   jax.ShapeDtypeStruct((t, LANES), F32),
                   jax.ShapeDtypeStruct((1, LANES), F32)),
        grid=(t // tm,),
        in_specs=[pl.BlockSpec((tm, ATTN_WIDTH), row),
                  pl.BlockSpec((tm // FFT_N2, FFT_PITCH, D_HYENA), lambda i: (i, 0, 0)),
                  pl.BlockSpec((tm, 2 * D_MODEL), row),
                  pl.BlockSpec((tm, D_MODEL), row),
                  pl.BlockSpec((1, 1, D_MODEL), per_b),
                  pl.BlockSpec((1, 1, D_MODEL), per_b),
                  pl.BlockSpec((1, 1, D_MODEL), per_b),
                  pl.BlockSpec((1, D_MODEL), c2),
                  pl.BlockSpec(wa.shape, c2), pl.BlockSpec(wh.shape, c2), pl.BlockSpec(wo.shape, c2),
                  pl.BlockSpec(wrh.shape, c2), pl.BlockSpec(wrl.shape, c2), pl.BlockSpec(br.shape, c2)],
        out_specs=(pl.BlockSpec((tm, D_MODEL), row),
                   pl.BlockSpec((tm * ROW_TILES, LANES), row),
                   pl.BlockSpec((tm, LANES), row),
                   pl.BlockSpec((1, LANES), c2)),
        compiler_params=_cparams(("arbitrary",)),
        name="merge_router",
    )(attn, hy, gates, x, gate1, shift2, scale2, gain, wa, wh, wo, wrh, wrl, br)


def _rank_kernel(route_ref, pstart_ref, tri_ref, dest_ref, carry_ref):
    @pl.when(pl.program_id(0) == 0)
    def _():
        carry_ref[...] = jnp.zeros(carry_ref.shape, F32)

    r = route_ref[...]
    lane = lax.broadcasted_iota(jnp.int32, r.shape, 1).astype(F32)
    oh1 = (lane == r[:, 0:1]).astype(F32)
    oh2 = (lane == r[:, 1:2]).astype(F32)
    before = _dotbf(tri_ref[...], jnp.concatenate([oh1, oh2], axis=1).astype(BF16))
    before1 = before[:, :LANES]
    before2 = before[:, LANES:]
    base1 = pstart_ref[...] + carry_ref[...]
    d1 = jnp.sum(oh1 * (base1 + before1), axis=-1, keepdims=True)
    base2 = base1 + jnp.sum(oh1, axis=0, keepdims=True)
    d2 = jnp.sum(oh2 * (base2 + before2), axis=-1, keepdims=True)
    carry_ref[...] = base2 + jnp.sum(oh2, axis=0, keepdims=True) - pstart_ref[...]
    dest_ref[...] = jnp.where(lane == 0, d1, jnp.where(lane == 1, d2, 0.0))


def _rank(route, pstart, tri):
    t = route.shape[0]
    tm = tri.shape[0]
    return pl.pallas_call(
        _rank_kernel,
        out_shape=jax.ShapeDtypeStruct((t, LANES), F32),
        grid=(t // tm,),
        in_specs=[pl.BlockSpec((tm, LANES), lambda i: (i, 0)),
                  pl.BlockSpec((1, LANES), lambda i: (0, 0)),
                  pl.BlockSpec((tm, tm), lambda i: (0, 0))],
        out_specs=pl.BlockSpec((tm, LANES), lambda i: (i, 0)),
        scratch_shapes=[pltpu.VMEM((1, LANES), F32)],
        compiler_params=_cparams(("arbitrary",)),
        name="moe_rank",
    )(route, pstart, tri)


def _row_tile(ref, r):
    return ref.at[pl.ds(pl.multiple_of(r * ROW_TILES, ROW_TILES), ROW_TILES)]


def _pack_halves(x):
    half = x.shape[1] // 2
    hi = lax.bitcast_convert_type(x[:, :half].astype(BF16).astype(F32), jnp.uint32)
    lo = lax.bitcast_convert_type(x[:, half:].astype(BF16).astype(F32), jnp.uint32)
    return lax.bitcast_convert_type(hi | (lo >> 16), F32)


def _unpack_halves(p):
    u = lax.bitcast_convert_type(p, jnp.uint32)
    hi = lax.bitcast_convert_type(u & jnp.uint32(0xFFFF0000), F32)
    lo = lax.bitcast_convert_type(u << 16, F32)
    return jnp.concatenate([hi, lo], axis=1)


def _rows_from_tiles(ref, n):
    return jnp.concatenate([ref[pl.ds(s, n, stride=ROW_TILES), :] for s in range(ROW_TILES)], axis=1)


def _rows_to_tiles(ref, x):
    n = x.shape[0]
    for s in range(ROW_TILES):
        ref[pl.ds(s, n, stride=ROW_TILES), :] = x[:, s * LANES:(s + 1) * LANES]


def _dispatch_kernel(d1_ref, d2_ref, h_ref, init_ref, xs_ref, sem):
    del init_ref
    tm = h_ref.shape[0] // ROW_TILES
    base = pl.program_id(0) * tm

    def copies(r):
        src = _row_tile(h_ref, r)
        return (pltpu.make_async_copy(src, _row_tile(xs_ref, d1_ref[base + r]), sem),
                pltpu.make_async_copy(src, _row_tile(xs_ref, d2_ref[base + r]), sem))

    def issue(r, c):
        a, b = copies(r)
        a.start(priority=0)
        b.start(priority=1)
        return c

    def drain(r, c):
        a, b = copies(r)
        a.wait()
        b.wait()
        return c

    lax.fori_loop(0, tm, issue, 0, unroll=8)
    lax.fori_loop(0, tm, drain, 0, unroll=8)


def _dispatch(d1, d2, h2, init):
    t = h2.shape[0] // ROW_TILES
    tm = min(256, t)
    return pl.pallas_call(
        _dispatch_kernel,
        out_shape=jax.ShapeDtypeStruct(init.shape, h2.dtype),
        grid_spec=pltpu.PrefetchScalarGridSpec(
            num_scalar_prefetch=2,
            grid=(t // tm,),
            in_specs=[pl.BlockSpec((tm * ROW_TILES, LANES), lambda i, a, b: (i, 0)),
                      pl.BlockSpec(memory_space=pl.ANY)],
            out_specs=pl.BlockSpec(memory_space=pl.ANY),
            scratch_shapes=[pltpu.SemaphoreType.DMA(())]),
        input_output_aliases={3: 0},
        compiler_params=_cparams(("arbitrary",)),
        name="moe_dispatch",
    )(d1, d2, h2, init)


def _expert_kernel(be_ref, nu_ref, xs_ref, wgu_ref, wd_ref, ys_ref):
    del be_ref

    @pl.when(pl.program_id(0) < nu_ref[0])
    def _():
        x = _unpack_halves(_rows_from_tiles(xs_ref, EXPERT_BLOCK)).astype(BF16)
        gu = _dotbf(x, wgu_ref[0])
        g = gu[:, :D_EXPERT]
        u = gu[:, D_EXPERT:]
        a = (g * _sigmoid(g) * u).astype(BF16)
        _rows_to_tiles(ys_ref, _pack_halves(_dotbf(a, wd_ref[0])))


def _experts(blk_expert, n_used, xs, wgu, wd):
    cap = xs.shape[0] // ROW_TILES
    d = wgu.shape[1]
    used = lambda i, nu: jnp.minimum(i, nu[0] - 1)
    blk = pl.BlockSpec((EXPERT_BLOCK * ROW_TILES, LANES), lambda i, be, nu: (used(i, nu), 0))
    return pl.pallas_call(
        _expert_kernel,
        out_shape=jax.ShapeDtypeStruct(xs.shape, F32),
        grid_spec=pltpu.PrefetchScalarGridSpec(
            num_scalar_prefetch=2,
            grid=(cap // EXPERT_BLOCK,),
            in_specs=[blk,
                      pl.BlockSpec((1, d, 2 * D_EXPERT), lambda i, be, nu: (be[used(i, nu)], 0, 0)),
                      pl.BlockSpec((1, D_EXPERT, d), lambda i, be, nu: (be[used(i, nu)], 0, 0))],
            out_specs=blk),
        input_output_aliases={2: 0},
        compiler_params=_cparams(("arbitrary",)),
        name="moe_experts",
    )(blk_expert, n_used, xs, wgu, wd)


def _combine_kernel(d1_ref, d2_ref, ys_ref, x_ref, route_ref, gate_ref, o_ref, y1_ref, y2_ref, sem):
    tm = x_ref.shape[0]
    i = pl.program_id(0)
    slot = i % 2

    def copies(step, s, r):
        base = step * tm
        return (pltpu.make_async_copy(_row_tile(ys_ref, d1_ref[base + r]), _row_tile(y1_ref.at[s], r), sem.at[s]),
                pltpu.make_async_copy(_row_tile(ys_ref, d2_ref[base + r]), _row_tile(y2_ref.at[s], r), sem.at[s]))

    def issue(step, s):
        def body(r, c):
            a, b = copies(step, s, r)
            a.start(priority=0)
            b.start(priority=1)
            return c

        lax.fori_loop(0, tm, body, 0, unroll=8)

    def drain(step, s):
        def body(r, c):
            a, b = copies(step, s, r)
            a.wait()
            b.wait()
            return c

        lax.fori_loop(0, tm, body, 0, unroll=8)

    @pl.when(i == 0)
    def _():
        issue(0, 0)

    @pl.when(i + 1 < pl.num_programs(0))
    def _():
        issue(i + 1, 1 - slot)

    drain(i, slot)
    r = route_ref[...]
    ffn = (_unpack_halves(_rows_from_tiles(y1_ref.at[slot], tm)) * r[:, 2:3]
           + _unpack_halves(_rows_from_tiles(y2_ref.at[slot], tm)) * r[:, 3:4])
    o_ref[...] = x_ref[...] + gate_ref[0] * ffn


def _combine(d1, d2, ys, x1, route, gate2, nb, seq):
    t, d = x1.shape
    tm = min(256, seq)
    tps = seq // tm
    return pl.pallas_call(
        _combine_kernel,
        out_shape=jax.ShapeDtypeStruct((t, d), F32),
        grid_spec=pltpu.PrefetchScalarGridSpec(
            num_scalar_prefetch=2,
            grid=(t // tm,),
            in_specs=[pl.BlockSpec(memory_space=pl.ANY),
                      pl.BlockSpec((tm, d), lambda i, a, b: (i, 0)),
                      pl.BlockSpec((tm, LANES), lambda i, a, b: (i, 0)),
                      pl.BlockSpec((1, 1, d), lambda i, a, b: (i // tps, 0, 0))],
            out_specs=pl.BlockSpec((tm, d), lambda i, a, b: (i, 0)),
            scratch_shapes=[pltpu.VMEM((2, tm * ROW_TILES, LANES), F32),
                            pltpu.VMEM((2, tm * ROW_TILES, LANES), F32),
                            pltpu.SemaphoreType.DMA((2,))]),
        compiler_params=_cparams(("arbitrary",)),
        name="moe_combine",
    )(d1, d2, ys, x1, route, gate2)


def _rope_tables(seq):
    rows = seq // GRID_W
    row = jnp.broadcast_to(jnp.arange(rows, dtype=F32)[:, None], (rows, GRID_W)).reshape(-1)
    col = jnp.broadcast_to(jnp.arange(GRID_W, dtype=F32)[None, :], (rows, GRID_W)).reshape(-1)
    inv_freq = ROPE_THETA ** (-jnp.arange(ROPE_PAIRS_PER_AXIS, dtype=F32) / ROPE_PAIRS_PER_AXIS)
    ang = jnp.concatenate([row[:, None] * inv_freq, col[:, None] * inv_freq], axis=-1)
    cos = jnp.repeat(jnp.cos(ang), 2, axis=-1)
    sin = jnp.repeat(jnp.sin(ang), 2, axis=-1)
    sign = jnp.tile(jnp.array([-1.0, 1.0], F32), HEAD_DIM // 2)
    return jnp.tile(cos, (1, N_HEADS)), jnp.tile(sin * sign, (1, N_HEADS))


def _filter_features(seq):
    t = jnp.linspace(0.0, 1.0, seq, dtype=F32)[:, None]
    w = (2.0 * math.pi / seq) * jnp.arange(seq, dtype=F32)[:, None]
    bands = jnp.linspace(1e-4, FILTER_BANDS - 1, FILTER_BANDS, dtype=F32)[None, :]
    z = jnp.concatenate([t, jnp.cos(bands * w), -jnp.sin(bands * w)], axis=-1)
    pad = jnp.zeros((seq, FILTER_ORDER - FILTER_EMB), F32)
    fwd = jnp.concatenate([z, pad], axis=-1).at[:, FILTER_EMB].set(1.0)
    rev = jnp.concatenate([z[:1], z[:0:-1]], axis=0)
    sign = jnp.concatenate([jnp.zeros((1,), F32), -jnp.ones((seq - 1,), F32)])
    bwd = jnp.concatenate([rev, pad], axis=-1).at[:, FILTER_EMB].set(sign)
    deltas = jnp.abs(jnp.linspace(MIN_DECAY, MAX_DECAY, D_HYENA, dtype=F32))[None, :]
    return jnp.concatenate([fwd, bwd], axis=-1), deltas


def _run_trunk(x, mod, p):
    nb, seq, d = x.shape
    t = nb * seq
    depth = mod.shape[0]
    cos, sin = _rope_tables(seq)
    zfeat, deltas = _filter_features(seq)
    tabs = _dft_tables(seq)
    h1 = tabs["h1"]
    tm_rank = min(512, t)
    tri = jnp.tril(jnp.ones((tm_rank, tm_rank), BF16), -1)
    cap = t * TOP_K + N_EXPERTS * EXPERT_BLOCK
    n_blk = cap // EXPERT_BLOCK
    lane_e = jnp.arange(LANES)

    xf = x.reshape(t, d)
    xs = jnp.zeros((cap * ROW_TILES, LANES), F32)
    for l in range(depth):
        m6 = mod[l].reshape(nb, 6, 1, d)
        shift1, scale1, gate1, shift2, scale2, gate2 = (m6[:, j] for j in range(6))

        q, k, vt, z, x2, gates = _inproj(xf, shift1, scale1, p["norm_mix"][l], p["w_in"][l], p["w_vt"][l],
                                         p["qg"][l], p["kg"][l], p["bd"], cos, sin,
                                         p["conv_w"][l], p["conv_b"][l], nb, seq)
        z = z.reshape(nb, h1 * FFT_PITCH, D_HYENA)
        x2 = x2.reshape(nb, h1 * FFT_PITCH, D_HYENA)
        bound = HEAD_DIM * jnp.max(jnp.abs(p["qg"][l])) * jnp.max(jnp.abs(p["kg"][l]))
        attn = lax.cond(bound <= SCORE_BOUND,
                        functools.partial(_attention, online=False),
                        functools.partial(_attention, online=True), q, k, vt).reshape(t, ATTN_WIDTH)

        kern = _filters(zfeat, p["filt_w1"][l], p["filt_b1"][l], p["filt_w2"][l], p["filt_b2"][l],
                        p["filt_w3"][l], p["filt_b3"][l], p["filt_freq"][l], p["filt_w4"][l], deltas)
        ka = _fft_outer(tabs["g_full"], kern.reshape(1, 2 * h1 * FFT_PITCH, D_HYENA))
        kf = _fft_filter_inner(tabs["mf"], ka.reshape(1, 2, h1, FFT_PITCH, D_HYENA))
        za = _fft_outer(tabs["g_half"], z)
        zb = _fft_inner(tabs["mf"], tabs["mi"], kf, za.reshape(nb, 2, h1, FFT_PITCH, D_HYENA))
        hy = _fft_final(tabs["g_out"], zb.reshape(nb, 2 * h1 * FFT_PITCH, D_HYENA), z, x2,
                        p["skip"][l]).reshape(nb * h1, FFT_PITCH, D_HYENA)

        x1, h2, route, counts = _merge(attn, hy, gates, xf, gate1, shift2, scale2, p["norm_ffn"][l],
                                       p["w_br_attn"][l], p["w_br_hyena"][l], p["w_out"][l],
                                       p["wr_hi"][l], p["wr_lo"][l], p["b_route"][l], nb, seq)

        cnt = counts[0].astype(jnp.int32)
        padded = jnp.where(lane_e < N_EXPERTS, (cnt + EXPERT_BLOCK - 1) // EXPERT_BLOCK * EXPERT_BLOCK, 0)
        pad_end = jnp.cumsum(padded)
        pstart = (pad_end - padded).astype(F32)[None, :]
        blk_start = jnp.arange(n_blk, dtype=jnp.int32) * EXPERT_BLOCK
        blk_expert = jnp.minimum(
            jnp.sum(blk_start[:, None] >= pad_end[None, :N_EXPERTS], axis=1), N_EXPERTS - 1).astype(jnp.int32)

        dest = _rank(route, pstart, tri)
        d1 = dest[:, 0].astype(jnp.int32)
        d2 = dest[:, 1].astype(jnp.int32)
        n_used = (pad_end[N_EXPERTS - 1:N_EXPERTS] // EXPERT_BLOCK).astype(jnp.int32)
        xs = _dispatch(d1, d2, h2, xs)
        xs = _experts(blk_expert, n_used, xs, p["w_gu"][l], p["w_down"][l])
        xf = _combine(d1, d2, xs, x1, route, gate2, nb, seq)
    return xf.reshape(nb, seq, d)


def kernel(x_prompt, x_sample, c_prompt, c_sample, w_ada, b_ada, norm_mix, norm_ffn, w_in, q_gain, k_gain, conv_w, conv_b, filt_w1, filt_b1, filt_w2, filt_b2, filt_w3, filt_b3, filt_freq, filt_w4, hyena_skip, w_br_attn, w_br_hyena, w_out, w_group, b_group, w_router, b_router, w_e_gate, w_e_up, w_e_down):
    depth = w_ada.shape[0]
    bp, bs = c_prompt.shape[0], c_sample.shape[0]
    rows = -(-(bp + bs) // SUBLANES) * SUBLANES
    c_pad = jnp.zeros((rows, D_MODEL), F32).at[:bp].set(c_prompt).at[bp:bp + bs].set(c_sample)
    mod = _ada(c_pad, w_ada, b_ada)

    scale = HEAD_DIM ** -0.5 * math.log2(math.e)
    head_id = np.arange(ATTN_WIDTH) // HEAD_DIM
    route_w = jnp.concatenate([w_router, w_group], axis=-1)
    route_w = jnp.pad(route_w, ((0, 0), (0, 0), (0, LANES - route_w.shape[-1])))
    wr_hi = route_w.astype(BF16)

    def block_diag(a, b):
        return jnp.concatenate([jnp.pad(a, ((0, 0), (0, 0), (0, b.shape[2]))),
                                jnp.pad(b, ((0, 0), (0, 0), (a.shape[2], 0)))], axis=1)

    twice = lambda w: block_diag(w, w)
    p = dict(
        norm_mix=norm_mix.reshape(depth, 1, D_MODEL),
        norm_ffn=norm_ffn.reshape(depth, 1, D_MODEL),
        w_in=w_in.astype(BF16),
        w_vt=jnp.swapaxes(w_in[:, :, COL_V:COL_U], 1, 2).astype(BF16),
        qg=(jnp.tile(q_gain, (1, N_HEADS)) * scale).reshape(depth, 1, ATTN_WIDTH),
        kg=jnp.tile(k_gain, (1, N_KV_HEADS)).reshape(depth, 1, KV_WIDTH),
        bd=jnp.asarray(head_id[:, None] == head_id[None, :], dtype=BF16),
        conv_w=conv_w, conv_b=conv_b.reshape(depth, 1, 3 * D_HYENA),
        filt_w1=twice(jnp.pad(filt_w1, ((0, 0), (0, FILTER_ORDER - FILTER_EMB), (0, 0)))),
        filt_b1=jnp.tile(filt_b1, (1, 2)).reshape(depth, 1, 2 * FILTER_ORDER),
        filt_w2=twice(filt_w2), filt_b2=jnp.tile(filt_b2, (1, 2)).reshape(depth, 1, 2 * FILTER_ORDER),
        filt_w3=twice(filt_w3), filt_b3=jnp.tile(filt_b3, (1, 2)).reshape(depth, 1, 2 * FILTER_ORDER),
        filt_freq=jnp.tile(filt_freq, (1, 2)).reshape(depth, 1, 2 * FILTER_ORDER),
        filt_w4=block_diag(filt_w4[:, :, :D_HYENA], filt_w4[:, :, D_HYENA:]),
        skip=hyena_skip.reshape(depth, 1, D_HYENA),
        w_br_attn=w_br_attn.astype(BF16), w_br_hyena=w_br_hyena.astype(BF16), w_out=w_out.astype(BF16),
        wr_hi=wr_hi, wr_lo=(route_w - wr_hi.astype(F32)).astype(BF16),
        b_route=jnp.pad(jnp.concatenate([b_router, b_group], axis=-1),
                        ((0, 0), (0, LANES - N_EXPERTS - N_GROUPS))).reshape(depth, 1, LANES),
        w_gu=jnp.concatenate([w_e_gate, w_e_up], axis=-1).astype(BF16),
        w_down=w_e_down.astype(BF16),
    )
    y_prompt = _run_trunk(x_prompt, mod[:, :bp], p)
    y_sample = _run_trunk(x_sample, mod[:, bp:bp + bs], p)
    return (y_prompt, y_sample)
```

```python
import functools
import math

import numpy as np
import jax
import jax.numpy as jnp
from jax import lax
from jax.experimental import pallas as pl
from jax.experimental.pallas import tpu as pltpu

F32 = jnp.float32
BF16 = jnp.bfloat16

D_MODEL = 1024
GRID_W = 64
N_HEADS = 8
N_KV_HEADS = 4
HEAD_DIM = 64
Q_PER_KV = N_HEADS // N_KV_HEADS
ATTN_WIDTH = N_HEADS * HEAD_DIM
KV_WIDTH = N_KV_HEADS * HEAD_DIM
ROPE_THETA = 10000.0
ROPE_PAIRS_PER_AXIS = HEAD_DIM // 4
D_HYENA = D_MODEL // 2
FILTER_EMB = 33
FILTER_BANDS = (FILTER_EMB - 1) // 2
FILTER_ORDER = 64
DECAY_TARGET = 1e-2
MAX_DECAY = math.log(DECAY_TARGET) / 0.3
MIN_DECAY = math.log(DECAY_TARGET) / 1.5
MOD_SHIFT = 0.05
N_GROUPS = 4
EXPERTS_PER_GROUP = 8
N_EXPERTS = N_GROUPS * EXPERTS_PER_GROUP
TOP_K = 2
D_EXPERT = D_MODEL // 4
RMS_EPS = 1e-6
IN_COLS = ATTN_WIDTH + 2 * KV_WIDTH + 3 * D_HYENA + 2 * D_MODEL
COL_K = ATTN_WIDTH
COL_V = ATTN_WIDTH + KV_WIDTH
COL_U = ATTN_WIDTH + 2 * KV_WIDTH
COL_G = COL_U + 3 * D_HYENA

LANES = 128
SUBLANES = 8
ROW_TILES = D_MODEL // 2 // LANES
VMEM_LIMIT = 56 * 1024 * 1024

V_ROWS = HEAD_DIM + 16
SCORE_BOUND = 60.0
FFT_N2 = 128
FFT_PITCH = FFT_N2 + SUBLANES
EXPERT_BLOCK = 512
NEG_BIG = -1e30

_HI = lax.Precision.HIGHEST


def _dot32(a, b):
    return jnp.dot(a, b, precision=_HI, preferred_element_type=F32)


def _dotbf(a, b):
    return jnp.dot(a, b, preferred_element_type=F32)


def _sigmoid(x):
    return 0.5 * jnp.tanh(0.5 * x) + 0.5


def _cparams(sem):
    return pltpu.CompilerParams(dimension_semantics=sem, vmem_limit_bytes=VMEM_LIMIT)


def _ada_kernel(c_ref, w_ref, b_ref, o_ref):
    c = c_ref[...]
    act = c * jax.nn.sigmoid(c)
    o_ref[0] = _dot32(act, w_ref[0]) + b_ref[0]


def _ada(c_pad, w_ada, b_ada):
    depth, d, n = w_ada.shape
    tn = 1536
    return pl.pallas_call(
        _ada_kernel,
        out_shape=jax.ShapeDtypeStruct((depth, c_pad.shape[0], n), F32),
        grid=(depth, n // tn),
        in_specs=[pl.BlockSpec(c_pad.shape, lambda l, j: (0, 0)),
                  pl.BlockSpec((1, d, tn), lambda l, j: (l, 0, j)),
                  pl.BlockSpec((1, 1, tn), lambda l, j: (l, 0, j))],
        out_specs=pl.BlockSpec((1, c_pad.shape[0], tn), lambda l, j: (l, 0, j)),
        compiler_params=_cparams(("arbitrary", "arbitrary")),
        name="ada_mod",
    )(c_pad, w_ada, b_ada.reshape(depth, 1, n))


def _swap_pairs(x):
    n = x.shape[-1]
    lane = lax.broadcasted_iota(jnp.int32, x.shape, 1)
    nxt = pltpu.roll(x, n - 1, 1)
    prv = pltpu.roll(x, 1, 1)
    return jnp.where(lane % 2 == 0, nxt, prv)


def _head_norm_rope(p, gain, bd, cos, sin_signed):
    sq = (p * p).astype(BF16)
    ms = _dotbf(sq, bd) * (1.0 / HEAD_DIM)
    pn = p * lax.rsqrt(ms + RMS_EPS) * gain
    return pn * cos + _swap_pairs(pn) * sin_signed


def _store_groups(ref, x):
    pad = jnp.zeros((FFT_PITCH - FFT_N2, x.shape[1]), x.dtype)
    for g in range(ref.shape[0]):
        ref[g, 0:FFT_N2, :] = x[g * FFT_N2:(g + 1) * FFT_N2]
        ref[g, FFT_N2:FFT_PITCH, :] = pad


def _inproj_kernel(x_ref, xp_ref, xn_ref, shift_ref, scale_ref, gain_ref, w_ref, wvt_ref, qg_ref, kg_ref,
                   bd_ref, cos_ref, sin_ref, cw_ref, cb_ref, q_ref, k_ref, vt_ref, z_ref, x2_ref, g_ref, *, tps):
    def normed(x):
        ms = jnp.mean(x * x, axis=-1, keepdims=True)
        h = x * lax.rsqrt(ms + RMS_EPS) * gain_ref[...]
        return (h * (1.0 + scale_ref[0]) + shift_ref[0]).astype(BF16)

    hb = normed(x_ref[...])
    tm = hb.shape[0]
    cos = cos_ref[...]
    sin = sin_ref[...]
    bd = bd_ref[...]

    q = _dotbf(hb, w_ref[:, 0:COL_K])
    q = _head_norm_rope(q, qg_ref[...], bd, cos, sin)
    zpad = jnp.zeros((tm, LANES - HEAD_DIM), BF16)
    for hd in range(N_HEADS):
        q_ref[0, hd, :, 0:HEAD_DIM] = q[:, hd * HEAD_DIM:(hd + 1) * HEAD_DIM].astype(BF16)
        q_ref[0, hd, :, HEAD_DIM:LANES] = zpad

    k = _dotbf(hb, w_ref[:, COL_K:COL_V])
    k = _head_norm_rope(k, kg_ref[...], bd[:KV_WIDTH, :KV_WIDTH], cos[:, :KV_WIDTH], sin[:, :KV_WIDTH])
    vt = lax.dot_general(wvt_ref[...], hb, (((1,), (1,)), ((), ())), preferred_element_type=F32)
    ones = jnp.ones((V_ROWS - HEAD_DIM, vt.shape[1]), BF16)
    for hd in range(N_KV_HEADS):
        k_ref[0, hd, :, 0:HEAD_DIM] = k[:, hd * HEAD_DIM:(hd + 1) * HEAD_DIM].astype(BF16)
        k_ref[0, hd, :, HEAD_DIM:LANES] = zpad
        vt_ref[0, hd, 0:HEAD_DIM, :] = vt[hd * HEAD_DIM:(hd + 1) * HEAD_DIM].astype(BF16)
        vt_ref[0, hd, HEAD_DIM:V_ROWS, :] = ones

    g_ref[...] = _sigmoid(_dotbf(hb, w_ref[:, COL_G:IN_COLS])).astype(BF16)

    halo = normed(jnp.concatenate([xp_ref[...], xn_ref[...]], axis=0))
    u_all = _dotbf(jnp.concatenate([hb, halo], axis=0), w_ref[:, COL_U:COL_G])
    u = u_all[:tm]
    pos = pl.program_id(0) % tps
    prev_row = u_all[tm + SUBLANES - 1:tm + SUBLANES] * (pos > 0).astype(F32)
    next_row = u_all[tm + SUBLANES:tm + SUBLANES + 1] * (pos < tps - 1).astype(F32)
    ridx = lax.broadcasted_iota(jnp.int32, u.shape, 0)
    u_prev = jnp.where(ridx == 0, prev_row, pltpu.roll(u, 1, 0))
    u_next = jnp.where(ridx == tm - 1, next_row, pltpu.roll(u, tm - 1, 0))
    w = cw_ref[...]
    c = cb_ref[...] + u_prev * w[0:1] + u * w[1:2] + u_next * w[2:3]
    _store_groups(z_ref.at[0], c[:, 2 * D_HYENA:3 * D_HYENA] * c[:, 0:D_HYENA])
    _store_groups(x2_ref.at[0], c[:, D_HYENA:2 * D_HYENA])


def _inproj(x, shift, scale, gain, w_in, w_vt, qg, kg, bd, cos, sin, conv_w, conv_b, nb, seq):
    t = nb * seq
    tm = min(512, seq)
    tps = seq // tm
    rpb = tm // SUBLANES
    row = lambda i: (i, 0)
    per_b = lambda i: (i // tps, 0, 0)
    const2 = lambda i: (0, 0)
    pos = lambda i: (i % tps, 0)
    head_out = lambda i: (i // tps, 0, i % tps, 0)
    return pl.pallas_call(
        functools.partial(_inproj_kernel, tps=tps),
        out_shape=(jax.ShapeDtypeStruct((nb, N_HEADS, seq, LANES), BF16),
                   jax.ShapeDtypeStruct((nb, N_KV_HEADS, seq, LANES), BF16),
                   jax.ShapeDtypeStruct((nb, N_KV_HEADS, V_ROWS, seq), BF16),
                   jax.ShapeDtypeStruct((nb, seq // FFT_N2, FFT_PITCH, D_HYENA), F32),
                   jax.ShapeDtypeStruct((nb, seq // FFT_N2, FFT_PITCH, D_HYENA), F32),
                   jax.ShapeDtypeStruct((t, 2 * D_MODEL), BF16)),
        grid=(t // tm,),
        in_specs=[pl.BlockSpec((tm, D_MODEL), row),
                  pl.BlockSpec((SUBLANES, D_MODEL), lambda i: (jnp.maximum(i * rpb - 1, 0), 0)),
                  pl.BlockSpec((SUBLANES, D_MODEL), lambda i: (jnp.minimum((i + 1) * rpb, t // SUBLANES - 1), 0)),
                  pl.BlockSpec((1, 1, D_MODEL), per_b),
                  pl.BlockSpec((1, 1, D_MODEL), per_b),
                  pl.BlockSpec((1, D_MODEL), const2),
                  pl.BlockSpec((D_MODEL, IN_COLS), const2),
                  pl.BlockSpec((KV_WIDTH, D_MODEL), const2),
                  pl.BlockSpec((1, ATTN_WIDTH), const2),
                  pl.BlockSpec((1, KV_WIDTH), const2),
                  pl.BlockSpec((ATTN_WIDTH, ATTN_WIDTH), const2),
                  pl.BlockSpec((tm, ATTN_WIDTH), pos),
                  pl.BlockSpec((tm, ATTN_WIDTH), pos),
                  pl.BlockSpec((3, 3 * D_HYENA), const2),
                  pl.BlockSpec((1, 3 * D_HYENA), const2)],
        out_specs=(pl.BlockSpec((1, N_HEADS, tm, LANES), head_out),
                   pl.BlockSpec((1, N_KV_HEADS, tm, LANES), head_out),
                   pl.BlockSpec((1, N_KV_HEADS, V_ROWS, tm), lambda i: (i // tps, 0, 0, i % tps)),
                   pl.BlockSpec((1, tm // FFT_N2, FFT_PITCH, D_HYENA), lambda i: (i // tps, i % tps, 0, 0)),
                   pl.BlockSpec((1, tm // FFT_N2, FFT_PITCH, D_HYENA), lambda i: (i // tps, i % tps, 0, 0)),
                   pl.BlockSpec((tm, 2 * D_MODEL), row)),
        compiler_params=_cparams(("arbitrary",)),
        name="in_proj",
    )(x, x, x, shift, scale, gain, w_in, w_vt, qg, kg, bd, cos, sin, conv_w, conv_b)


def _attn_kernel(q_ref, k_ref, vt_ref, o_ref, m_ref, acc_ref, *, tkc, online):
    seq = k_ref.shape[2]
    nchunk = seq // tkc
    acc_ref[...] = jnp.zeros(acc_ref.shape, F32)
    if online:
        m_ref[...] = jnp.full(m_ref.shape, NEG_BIG, F32)

    def body(c, carry):
        off = pl.multiple_of(c * tkc, tkc)
        kc = k_ref[0, 0, pl.ds(off, tkc), :]
        vc = vt_ref[0, 0, :, pl.ds(off, tkc)]
        for h in range(Q_PER_KV):
            st = lax.dot_general(kc, q_ref[0, h], (((1,), (1,)), ((), ())), preferred_element_type=F32)
            if online:
                m_prev = m_ref[h]
                m_new = jnp.maximum(m_prev, jnp.max(st, axis=0, keepdims=True))
                p = jnp.exp2(st - m_new).astype(BF16)
                acc_ref[h] = jnp.exp2(m_prev - m_new) * acc_ref[h] + _dotbf(vc, p)
                m_ref[h] = m_new
            else:
                acc_ref[h] += _dotbf(vc, jnp.exp2(st).astype(BF16))
        return carry

    lax.fori_loop(0, nchunk, body, 0, unroll=2 if nchunk % 2 == 0 else 1)
    outs = []
    for h in range(Q_PER_KV):
        a = acc_ref[h]
        outs.append((a[:HEAD_DIM] / a[HEAD_DIM:HEAD_DIM + 1]).T)
    o_ref[0] = jnp.concatenate(outs, axis=1).astype(BF16)


def _attention(q, k, vt, online):
    nb, _, seq, _ = q.shape
    tq = min(512 if online else 2048, seq)
    tkc = min(512 if online else 1024, seq)
    return pl.pallas_call(
        functools.partial(_attn_kernel, tkc=tkc, online=online),
        out_shape=jax.ShapeDtypeStruct((nb, seq, ATTN_WIDTH), BF16),
        grid=(nb, N_KV_HEADS, seq // tq),
        in_specs=[pl.BlockSpec((1, Q_PER_KV, tq, LANES), lambda b, g, i: (b, g, i, 0)),
                  pl.BlockSpec((1, 1, seq, LANES), lambda b, g, i: (b, g, 0, 0)),
                  pl.BlockSpec((1, 1, V_ROWS, seq), lambda b, g, i: (b, g, 0, 0))],
        out_specs=pl.BlockSpec((1, tq, Q_PER_KV * HEAD_DIM), lambda b, g, i: (b, i, g)),
        scratch_shapes=[pltpu.VMEM((Q_PER_KV, 1, tq), F32),
                        pltpu.VMEM((Q_PER_KV, V_ROWS, tq), F32)],
        compiler_params=_cparams(("arbitrary", "arbitrary", "arbitrary")),
        name="attn_online" if online else "attn_bounded",
    )(q, k, vt)


def _filter_kernel(z_ref, w1_ref, b1_ref, w2_ref, b2_ref, w3_ref, b3_ref, fr_ref, w4_ref,
                   dl_ref, o_ref):
    z = z_ref[...]
    fr = fr_ref[...]
    h = jnp.sin(fr * (_dot32(z, w1_ref[...]) + b1_ref[...]))
    h = jnp.sin(fr * (_dot32(h, w2_ref[...]) + b2_ref[...]))
    h = jnp.sin(fr * (_dot32(h, w3_ref[...]) + b3_ref[...]))
    h = _dot32(h, w4_ref[...])
    dl = dl_ref[...]
    for d in range(2):
        t = z[:, d * FILTER_ORDER:d * FILTER_ORDER + 1]
        sign = z[:, d * FILTER_ORDER + FILTER_EMB:d * FILTER_ORDER + FILTER_EMB + 1]
        _store_groups(o_ref.at[d], h[:, d * D_HYENA:(d + 1) * D_HYENA] * (jnp.exp(-t * dl) + MOD_SHIFT) * sign)


def _filters(zfeat, w1, b1, w2, b2, w3, b3, fr, w4, deltas):
    seq, fe = zfeat.shape
    tm = min(512, seq)
    c2 = lambda i: (0, 0)
    return pl.pallas_call(
        _filter_kernel,
        out_shape=jax.ShapeDtypeStruct((2, seq // FFT_N2, FFT_PITCH, D_HYENA), F32),
        grid=(seq // tm,),
        in_specs=[pl.BlockSpec((tm, fe), lambda i: (i, 0)),
                  pl.BlockSpec(w1.shape, c2), pl.BlockSpec(b1.shape, c2),
                  pl.BlockSpec(w2.shape, c2), pl.BlockSpec(b2.shape, c2),
                  pl.BlockSpec(w3.shape, c2), pl.BlockSpec(b3.shape, c2),
                  pl.BlockSpec(fr.shape, c2), pl.BlockSpec(w4.shape, c2),
                  pl.BlockSpec(deltas.shape, c2)],
        out_specs=pl.BlockSpec((2, tm // FFT_N2, FFT_PITCH, D_HYENA), lambda i: (0, i, 0, 0)),
        compiler_params=_cparams(("arbitrary",)),
        name="hyena_filters",
    )(zfeat, w1, b1, w2, b2, w3, b3, fr, w4, deltas)


def _zero_pad_rows(o_ref, groups):
    zeros = jnp.zeros((groups, o_ref.shape[1]), o_ref.dtype)
    for s in range(FFT_N2, FFT_PITCH):
        o_ref[pl.ds(s, groups, stride=FFT_PITCH), :] = zeros


def _fft_outer_kernel(g_ref, x_ref, o_ref):
    g = g_ref[...]
    m, kk = g.shape

    def body(jp, c):
        j = 2 * jp
        xj = jnp.concatenate([x_ref[pl.ds(j, kk, stride=FFT_PITCH), :],
                              x_ref[pl.ds(j + 1, kk, stride=FFT_PITCH), :]], axis=1)
        r = _dotbf(g, xj.astype(BF16))
        o_ref[pl.ds(j, m, stride=FFT_PITCH), :] = r[:, :LANES]
        o_ref[pl.ds(j + 1, m, stride=FFT_PITCH), :] = r[:, LANES:]
        return c

    lax.fori_loop(0, FFT_N2 // 2, body, 0, unroll=4)
    _zero_pad_rows(o_ref, m)


def _fft_outer(gmat, x):
    nb, rows, ch = x.shape
    m, kk = gmat.shape
    return pl.pallas_call(
        _fft_outer_kernel,
        out_shape=jax.ShapeDtypeStruct((nb, m * FFT_PITCH, ch), F32),
        grid=(nb, ch // LANES),
        in_specs=[pl.BlockSpec((m, kk), lambda b, c: (0, 0)),
                  pl.BlockSpec((None, rows, LANES), lambda b, c: (b, 0, c))],
        out_specs=pl.BlockSpec((None, m * FFT_PITCH, LANES), lambda b, c: (b, 0, c)),
        compiler_params=_cparams(("arbitrary", "arbitrary")),
        name="fft_outer",
    )(gmat, x)


FFT_K1_PER_STEP = 4


def _fft_filter_inner_kernel(mf_ref, a_ref, o_ref):
    n2, ch = FFT_N2, a_ref.shape[4]
    for r in range(a_ref.shape[2]):
        a = a_ref[0, :, r, 0:n2, :].reshape(2 * n2, ch)
        o_ref[r] = _dotbf(mf_ref[r], a.astype(BF16)).reshape(2, n2, ch)


def _fft_filter_inner(mf, a5):
    _, _, h1, pitch, ch = a5.shape
    n2 = FFT_N2
    kr = FFT_K1_PER_STEP
    return pl.pallas_call(
        _fft_filter_inner_kernel,
        out_shape=jax.ShapeDtypeStruct((h1, 2, n2, ch), F32),
        grid=(h1 // kr,),
        in_specs=[pl.BlockSpec((kr, 2 * n2, 2 * n2), lambda k: (k, 0, 0)),
                  pl.BlockSpec((1, 2, kr, pitch, ch), lambda k: (0, 0, k, 0, 0))],
        out_specs=pl.BlockSpec((kr, 2, n2, ch), lambda k: (k, 0, 0, 0)),
        compiler_params=_cparams(("arbitrary",)),
        name="fft_filter_inner",
    )(mf, a5)


def _fft_inner_kernel(mf_ref, mi_ref, kf_ref, a_ref, o_ref):
    n2, pitch, ch = FFT_N2, a_ref.shape[3], a_ref.shape[4]
    for r in range(a_ref.shape[2]):
        a = a_ref[0, :, r, 0:n2, :].reshape(2 * n2, ch)
        xs = _dotbf(mf_ref[r], a.astype(BF16))
        xr, xi = xs[:n2], xs[n2:]
        kr, ki = kf_ref[r, 0], kf_ref[r, 1]
        p = jnp.concatenate([xr * kr - xi * ki, xr * ki + xi * kr], axis=0)
        o_ref[0, :, r, 0:n2, :] = _dotbf(mi_ref[r], p.astype(BF16)).reshape(2, n2, ch)
        o_ref[0, :, r, n2:pitch, :] = jnp.zeros((2, pitch - n2, ch), F32)


def _fft_inner(mf, mi, kf, a5):
    nb, _, h1, pitch, ch = a5.shape
    n2 = FFT_N2
    kr = FFT_K1_PER_STEP
    return pl.pallas_call(
        _fft_inner_kernel,
        out_shape=jax.ShapeDtypeStruct(a5.shape, F32),
        grid=(h1 // kr, nb),
        in_specs=[pl.BlockSpec((kr, 2 * n2, 2 * n2), lambda k, b: (k, 0, 0)),
                  pl.BlockSpec((kr, 2 * n2, 2 * n2), lambda k, b: (k, 0, 0)),
                  pl.BlockSpec((kr, 2, n2, ch), lambda k, b: (k, 0, 0, 0)),
                  pl.BlockSpec((1, 2, kr, pitch, ch), lambda k, b: (b, 0, k, 0, 0))],
        out_specs=pl.BlockSpec((1, 2, kr, pitch, ch), lambda k, b: (b, 0, k, 0, 0)),
        compiler_params=_cparams(("arbitrary", "arbitrary")),
        name="fft_inner",
    )(mf, mi, kf, a5)


def _fft_final_kernel(g_ref, b_ref, z_ref, x2_ref, skip_ref, o_ref):
    g = g_ref[...]
    h1, m = g.shape
    skip = skip_ref[...]

    def body(jp, c):
        j = 2 * jp
        bj = jnp.concatenate([b_ref[pl.ds(j, m, stride=FFT_PITCH), :],
                              b_ref[pl.ds(j + 1, m, stride=FFT_PITCH), :]], axis=1)
        y = _dotbf(g, bj.astype(BF16))
        for d in range(2):
            zj = z_ref[pl.ds(j + d, h1, stride=FFT_PITCH), :]
            xj = x2_ref[pl.ds(j + d, h1, stride=FFT_PITCH), :]
            o_ref[pl.ds(j + d, h1, stride=FFT_PITCH), :] = xj * (y[:, d * LANES:(d + 1) * LANES] + zj * skip)
        return c

    lax.fori_loop(0, FFT_N2 // 2, body, 0, unroll=4)
    _zero_pad_rows(o_ref, h1)


def _fft_final(gc, bm, z, x2, skip):
    nb, rows, ch = z.shape
    h1, m = gc.shape
    blk = lambda r: pl.BlockSpec((None, r, LANES), lambda b, c: (b, 0, c))
    return pl.pallas_call(
        _fft_final_kernel,
        out_shape=jax.ShapeDtypeStruct((nb, rows, ch), F32),
        grid=(nb, ch // LANES),
        in_specs=[pl.BlockSpec((h1, m), lambda b, c: (0, 0)), blk(m * FFT_PITCH), blk(rows), blk(rows),
                  pl.BlockSpec((1, LANES), lambda b, c: (0, c))],
        out_specs=blk(rows),
        compiler_params=_cparams(("arbitrary", "arbitrary")),
        name="fft_final",
    )(gc, bm, z, x2, skip)


def _dft_tables(seq):
    n = 2 * seq
    n2 = FFT_N2
    n1 = n // n2
    h1 = n1 // 2
    i32 = jnp.int32
    k1 = jnp.arange(h1, dtype=i32)[:, None]
    a1 = jnp.arange(n1, dtype=i32)[None, :]
    th = (math.pi / n1) * ((a1 * (2 * k1 + 1)) % (2 * n1)).astype(F32)
    g_re, g_im = jnp.cos(th), -jnp.sin(th)
    g_full = jnp.concatenate([g_re, g_im], axis=0)
    g_half = g_full[:, :h1]
    g_out = (2.0 / n) * jnp.concatenate([g_re[:, :h1].T, g_im[:, :h1].T], axis=1)
    k2 = jnp.arange(n2, dtype=i32)[None, :, None]
    b2 = jnp.arange(n2, dtype=i32)[None, None, :]
    kk = jnp.arange(h1, dtype=i32)[:, None, None]
    ph = (math.pi / n) * ((b2 * (2 * kk + 1 + 2 * n1 * k2)) % (2 * n)).astype(F32)
    m_re, m_im = jnp.cos(ph), -jnp.sin(ph)
    mf = jnp.concatenate([jnp.concatenate([m_re, -m_im], axis=2),
                          jnp.concatenate([m_im, m_re], axis=2)], axis=1)
    mt_re, mt_im = jnp.transpose(m_re, (0, 2, 1)), jnp.transpose(m_im, (0, 2, 1))
    mi = jnp.concatenate([jnp.concatenate([mt_re, mt_im], axis=2),
                          jnp.concatenate([-mt_im, mt_re], axis=2)], axis=1)
    b16 = lambda a: a.astype(BF16)
    return dict(n1=n1, h1=h1, g_full=b16(g_full), g_half=b16(g_half), g_out=b16(g_out), mf=b16(mf), mi=b16(mi))


def _merge_kernel(attn_ref, hy_ref, g_ref, x_ref, gate_ref, shift_ref, scale_ref, gain_ref,
                  wa_ref, wh_ref, wo_ref, wrh_ref, wrl_ref, br_ref,
                  x1_ref, h2_ref, route_ref, cnt_ref):
    g = g_ref[...].astype(F32)
    hy = jnp.concatenate([hy_ref[grp, 0:FFT_N2, :] for grp in range(hy_ref.shape[0])], axis=0)
    merged = (g[:, :D_MODEL] * _dotbf(attn_ref[...], wa_ref[...])
              + g[:, D_MODEL:] * _dotbf(hy.astype(BF16), wh_ref[...]))
    mix = _dotbf(merged.astype(BF16), wo_ref[...])
    x1 = x_ref[...] + gate_ref[0] * mix
    x1_ref[...] = x1

    ms = jnp.mean(x1 * x1, axis=-1, keepdims=True)
    h2 = x1 * lax.rsqrt(ms + RMS_EPS) * gain_ref[...]
    h2 = h2 * (1.0 + scale_ref[0]) + shift_ref[0]
    _rows_to_tiles(h2_ref, _pack_halves(h2))

    hi = h2.astype(BF16)
    lo = (h2 - hi.astype(F32)).astype(BF16)
    wrh = wrh_ref[...]
    hw = _dotbf(hi, jnp.concatenate([wrh, wrl_ref[...]], axis=1))
    lg = hw[:, :LANES] + _dotbf(lo, wrh) + hw[:, LANES:] + br_ref[...]

    lane = lax.broadcasted_iota(jnp.int32, lg.shape, 1).astype(F32)
    is_grp = jnp.logical_and(lane >= N_EXPERTS, lane < N_EXPERTS + N_GROUPS)
    gm = jnp.where(is_grp, lg, NEG_BIG)
    gmax = jnp.max(gm, axis=-1, keepdims=True)
    gidx = jnp.min(jnp.where(gm == gmax, lane, 1e9), axis=-1, keepdims=True) - N_EXPERTS
    p_group = 1.0 / jnp.sum(jnp.where(is_grp, jnp.exp(gm - gmax), 0.0), axis=-1, keepdims=True)
    lo_lane = gidx * EXPERTS_PER_GROUP
    in_grp = jnp.logical_and(lane >= lo_lane, lane < lo_lane + EXPERTS_PER_GROUP)
    e1v = jnp.where(in_grp, lg, NEG_BIG)
    t1 = jnp.max(e1v, axis=-1, keepdims=True)
    i1 = jnp.min(jnp.where(e1v == t1, lane, 1e9), axis=-1, keepdims=True)
    e2v = jnp.where(lane == i1, NEG_BIG, e1v)
    t2 = jnp.max(e2v, axis=-1, keepdims=True)
    i2 = jnp.min(jnp.where(e2v == t2, lane, 1e9), axis=-1, keepdims=True)
    d = jnp.exp(t2 - t1)
    w1 = p_group / (1.0 + d)
    w2 = p_group * d / (1.0 + d)
    route_ref[...] = jnp.where(lane == 0, i1, jnp.where(lane == 1, i2,
                               jnp.where(lane == 2, w1, jnp.where(lane == 3, w2, 0.0))))

    onehot = (lane == i1).astype(F32) + (lane == i2).astype(F32)

    @pl.when(pl.program_id(0) == 0)
    def _():
        cnt_ref[...] = jnp.zeros(cnt_ref.shape, F32)

    cnt_ref[...] += jnp.sum(onehot, axis=0, keepdims=True)


def _merge(attn, hy, gates, x, gate1, shift2, scale2, gain, wa, wh, wo, wrh, wrl, br, nb, seq):
    t = nb * seq
    tm = min(512, seq)
    tps = seq // tm
    row = lambda i: (i, 0)
    per_b = lambda i: (i // tps, 0, 0)
    c2 = lambda i: (0, 0)
    return pl.pallas_call(
        _merge_kernel,
        out_shape=(jax.ShapeDtypeStruct((t, D_MODEL), F32),
                   jax.ShapeDtypeStruct((t * ROW_TILES, LANES), F32),
                   jax.ShapeDtypeStruct((t, LANES), F32),
                   jax.ShapeDtypeStruct((1, LANES), F32)),
        grid=(t // tm,),
        in_specs=[pl.BlockSpec((tm, ATTN_WIDTH), row),
                  pl.BlockSpec((tm // FFT_N2, FFT_PITCH, D_HYENA), lambda i: (i, 0, 0)),
                  pl.BlockSpec((tm, 2 * D_MODEL), row),
                  pl.BlockSpec((tm, D_MODEL), row),
                  pl.BlockSpec((1, 1, D_MODEL), per_b),
                  pl.BlockSpec((1, 1, D_MODEL), per_b),
                  pl.BlockSpec((1, 1, D_MODEL), per_b),
                  pl.BlockSpec((1, D_MODEL), c2),
                  pl.BlockSpec(wa.shape, c2), pl.BlockSpec(wh.shape, c2), pl.BlockSpec(wo.shape, c2),
                  pl.BlockSpec(wrh.shape, c2), pl.BlockSpec(wrl.shape, c2), pl.BlockSpec(br.shape, c2)],
        out_specs=(pl.BlockSpec((tm, D_MODEL), row),
                   pl.BlockSpec((tm * ROW_TILES, LANES), row),
                   pl.BlockSpec((tm, LANES), row),
                   pl.BlockSpec((1, LANES), c2)),
        compiler_params=_cparams(("arbitrary",)),
        name="merge_router",
    )(attn, hy, gates, x, gate1, shift2, scale2, gain, wa, wh, wo, wrh, wrl, br)


def _rank_kernel(route_ref, pstart_ref, tri_ref, dest_ref, carry_ref):
    @pl.when(pl.program_id(0) == 0)
    def _():
        carry_ref[...] = jnp.zeros(carry_ref.shape, F32)

    r = route_ref[...]
    lane = lax.broadcasted_iota(jnp.int32, r.shape, 1).astype(F32)
    oh1 = (lane == r[:, 0:1]).astype(F32)
    oh2 = (lane == r[:, 1:2]).astype(F32)
    before = _dotbf(tri_ref[...], jnp.concatenate([oh1, oh2], axis=1).astype(BF16))
    before1 = before[:, :LANES]
    before2 = before[:, LANES:]
    base1 = pstart_ref[...] + carry_ref[...]
    d1 = jnp.sum(oh1 * (base1 + before1), axis=-1, keepdims=True)
    base2 = base1 + jnp.sum(oh1, axis=0, keepdims=True)
    d2 = jnp.sum(oh2 * (base2 + before2), axis=-1, keepdims=True)
    carry_ref[...] = base2 + jnp.sum(oh2, axis=0, keepdims=True) - pstart_ref[...]
    dest_ref[...] = jnp.where(lane == 0, d1, jnp.where(lane == 1, d2, 0.0))


def _rank(route, pstart, tri):
    t = route.shape[0]
    tm = tri.shape[0]
    return pl.pallas_call(
        _rank_kernel,
        out_shape=jax.ShapeDtypeStruct((t, LANES), F32),
        grid=(t // tm,),
        in_specs=[pl.BlockSpec((tm, LANES), lambda i: (i, 0)),
                  pl.BlockSpec((1, LANES), lambda i: (0, 0)),
                  pl.BlockSpec((tm, tm), lambda i: (0, 0))],
        out_specs=pl.BlockSpec((tm, LANES), lambda i: (i, 0)),
        scratch_shapes=[pltpu.VMEM((1, LANES), F32)],
        compiler_params=_cparams(("arbitrary",)),
        name="moe_rank",
    )(route, pstart, tri)


def _row_tile(ref, r):
    return ref.at[pl.ds(pl.multiple_of(r * ROW_TILES, ROW_TILES), ROW_TILES)]


def _pack_halves(x):
    half = x.shape[1] // 2
    hi = lax.bitcast_convert_type(x[:, :half].astype(BF16).astype(F32), jnp.uint32)
    lo = lax.bitcast_convert_type(x[:, half:].astype(BF16).astype(F32), jnp.uint32)
    return lax.bitcast_convert_type(hi | (lo >> 16), F32)


def _unpack_halves(p):
    u = lax.bitcast_convert_type(p, jnp.uint32)
    hi = lax.bitcast_convert_type(u & jnp.uint32(0xFFFF0000), F32)
    lo = lax.bitcast_convert_type(u << 16, F32)
    return jnp.concatenate([hi, lo], axis=1)


def _rows_from_tiles(ref, n):
    return jnp.concatenate([ref[pl.ds(s, n, stride=ROW_TILES), :] for s in range(ROW_TILES)], axis=1)


def _rows_to_tiles(ref, x):
    n = x.shape[0]
    for s in range(ROW_TILES):
        ref[pl.ds(s, n, stride=ROW_TILES), :] = x[:, s * LANES:(s + 1) * LANES]


def _dispatch_kernel(d1_ref, d2_ref, h_ref, init_ref, xs_ref, sem):
    del init_ref
    tm = h_ref.shape[0] // ROW_TILES
    base = pl.program_id(0) * tm

    def copies(r):
        src = _row_tile(h_ref, r)
        return (pltpu.make_async_copy(src, _row_tile(xs_ref, d1_ref[base + r]), sem),
                pltpu.make_async_copy(src, _row_tile(xs_ref, d2_ref[base + r]), sem))

    def issue(r, c):
        a, b = copies(r)
        a.start(priority=0)
        b.start(priority=1)
        return c

    def drain(r, c):
        a, b = copies(r)
        a.wait()
        b.wait()
        return c

    lax.fori_loop(0, tm, issue, 0, unroll=8)
    lax.fori_loop(0, tm, drain, 0, unroll=8)


def _dispatch(d1, d2, h2, init):
    t = h2.shape[0] // ROW_TILES
    tm = min(256, t)
    return pl.pallas_call(
        _dispatch_kernel,
        out_shape=jax.ShapeDtypeStruct(init.shape, h2.dtype),
        grid_spec=pltpu.PrefetchScalarGridSpec(
            num_scalar_prefetch=2,
            grid=(t // tm,),
            in_specs=[pl.BlockSpec((tm * ROW_TILES, LANES), lambda i, a, b: (i, 0)),
                      pl.BlockSpec(memory_space=pl.ANY)],
            out_specs=pl.BlockSpec(memory_space=pl.ANY),
            scratch_shapes=[pltpu.SemaphoreType.DMA(())]),
        input_output_aliases={3: 0},
        compiler_params=_cparams(("arbitrary",)),
        name="moe_dispatch",
    )(d1, d2, h2, init)


def _expert_kernel(be_ref, nu_ref, xs_ref, wgu_ref, wd_ref, ys_ref):
    del be_ref

    @pl.when(pl.program_id(0) < nu_ref[0])
    def _():
        x = _unpack_halves(_rows_from_tiles(xs_ref, EXPERT_BLOCK)).astype(BF16)
        gu = _dotbf(x, wgu_ref[0])
        g = gu[:, :D_EXPERT]
        u = gu[:, D_EXPERT:]
        a = (g * _sigmoid(g) * u).astype(BF16)
        _rows_to_tiles(ys_ref, _pack_halves(_dotbf(a, wd_ref[0])))


def _experts(blk_expert, n_used, xs, wgu, wd):
    cap = xs.shape[0] // ROW_TILES
    d = wgu.shape[1]
    used = lambda i, nu: jnp.minimum(i, nu[0] - 1)
    blk = pl.BlockSpec((EXPERT_BLOCK * ROW_TILES, LANES), lambda i, be, nu: (used(i, nu), 0))
    return pl.pallas_call(
        _expert_kernel,
        out_shape=jax.ShapeDtypeStruct(xs.shape, F32),
        grid_spec=pltpu.PrefetchScalarGridSpec(
            num_scalar_prefetch=2,
            grid=(cap // EXPERT_BLOCK,),
            in_specs=[blk,
                      pl.BlockSpec((1, d, 2 * D_EXPERT), lambda i, be, nu: (be[used(i, nu)], 0, 0)),
                      pl.BlockSpec((1, D_EXPERT, d), lambda i, be, nu: (be[used(i, nu)], 0, 0))],
            out_specs=blk),
        input_output_aliases={2: 0},
        compiler_params=_cparams(("arbitrary",)),
        name="moe_experts",
    )(blk_expert, n_used, xs, wgu, wd)


def _combine_kernel(d1_ref, d2_ref, ys_ref, x_ref, route_ref, gate_ref, o_ref, y1_ref, y2_ref, sem):
    tm = x_ref.shape[0]
    i = pl.program_id(0)
    slot = i % 2

    def copies(step, s, r):
        base = step * tm
        return (pltpu.make_async_copy(_row_tile(ys_ref, d1_ref[base + r]), _row_tile(y1_ref.at[s], r), sem.at[s]),
                pltpu.make_async_copy(_row_tile(ys_ref, d2_ref[base + r]), _row_tile(y2_ref.at[s], r), sem.at[s]))

    def issue(step, s):
        def body(r, c):
            a, b = copies(step, s, r)
            a.start(priority=0)
            b.start(priority=1)
            return c

        lax.fori_loop(0, tm, body, 0, unroll=8)

    def drain(step, s):
        def body(r, c):
            a, b = copies(step, s, r)
            a.wait()
            b.wait()
            return c

        lax.fori_loop(0, tm, body, 0, unroll=8)

    @pl.when(i == 0)
    def _():
        issue(0, 0)

    @pl.when(i + 1 < pl.num_programs(0))
    def _():
        issue(i + 1, 1 - slot)

    drain(i, slot)
    r = route_ref[...]
    ffn = (_unpack_halves(_rows_from_tiles(y1_ref.at[slot], tm)) * r[:, 2:3]
           + _unpack_halves(_rows_from_tiles(y2_ref.at[slot], tm)) * r[:, 3:4])
    o_ref[...] = x_ref[...] + gate_ref[0] * ffn


def _combine(d1, d2, ys, x1, route, gate2, nb, seq):
    t, d = x1.shape
    tm = min(256, seq)
    tps = seq // tm
    return pl.pallas_call(
        _combine_kernel,
        out_shape=jax.ShapeDtypeStruct((t, d), F32),
        grid_spec=pltpu.PrefetchScalarGridSpec(
            num_scalar_prefetch=2,
            grid=(t // tm,),
            in_specs=[pl.BlockSpec(memory_space=pl.ANY),
                      pl.BlockSpec((tm, d), lambda i, a, b: (i, 0)),
                      pl.BlockSpec((tm, LANES), lambda i, a, b: (i, 0)),
                      pl.BlockSpec((1, 1, d), lambda i, a, b: (i // tps, 0, 0))],
            out_specs=pl.BlockSpec((tm, d), lambda i, a, b: (i, 0)),
            scratch_shapes=[pltpu.VMEM((2, tm * ROW_TILES, LANES), F32),
                            pltpu.VMEM((2, tm * ROW_TILES, LANES), F32),
                            pltpu.SemaphoreType.DMA((2,))]),
        compiler_params=_cparams(("arbitrary",)),
        name="moe_combine",
    )(d1, d2, ys, x1, route, gate2)


def _rope_tables(seq):
    rows = seq // GRID_W
    row = jnp.broadcast_to(jnp.arange(rows, dtype=F32)[:, None], (rows, GRID_W)).reshape(-1)
    col = jnp.broadcast_to(jnp.arange(GRID_W, dtype=F32)[None, :], (rows, GRID_W)).reshape(-1)
    inv_freq = ROPE_THETA ** (-jnp.arange(ROPE_PAIRS_PER_AXIS, dtype=F32) / ROPE_PAIRS_PER_AXIS)
    ang = jnp.concatenate([row[:, None] * inv_freq, col[:, None] * inv_freq], axis=-1)
    cos = jnp.repeat(jnp.cos(ang), 2, axis=-1)
    sin = jnp.repeat(jnp.sin(ang), 2, axis=-1)
    sign = jnp.tile(jnp.array([-1.0, 1.0], F32), HEAD_DIM // 2)
    return jnp.tile(cos, (1, N_HEADS)), jnp.tile(sin * sign, (1, N_HEADS))


def _filter_features(seq):
    t = jnp.linspace(0.0, 1.0, seq, dtype=F32)[:, None]
    w = (2.0 * math.pi / seq) * jnp.arange(seq, dtype=F32)[:, None]
    bands = jnp.linspace(1e-4, FILTER_BANDS - 1, FILTER_BANDS, dtype=F32)[None, :]
    z = jnp.concatenate([t, jnp.cos(bands * w), -jnp.sin(bands * w)], axis=-1)
    pad = jnp.zeros((seq, FILTER_ORDER - FILTER_EMB), F32)
    fwd = jnp.concatenate([z, pad], axis=-1).at[:, FILTER_EMB].set(1.0)
    rev = jnp.concatenate([z[:1], z[:0:-1]], axis=0)
    sign = jnp.concatenate([jnp.zeros((1,), F32), -jnp.ones((seq - 1,), F32)])
    bwd = jnp.concatenate([rev, pad], axis=-1).at[:, FILTER_EMB].set(sign)
    deltas = jnp.abs(jnp.linspace(MIN_DECAY, MAX_DECAY, D_HYENA, dtype=F32))[None, :]
    return jnp.concatenate([fwd, bwd], axis=-1), deltas


def _run_trunk(x, mod, p):
    nb, seq, d = x.shape
    t = nb * seq
    depth = mod.shape[0]
    cos, sin = _rope_tables(seq)
    zfeat, deltas = _filter_features(seq)
    tabs = _dft_tables(seq)
    h1 = tabs["h1"]
    tm_rank = min(512, t)
    tri = jnp.tril(jnp.ones((tm_rank, tm_rank), BF16), -1)
    cap = t * TOP_K + N_EXPERTS * EXPERT_BLOCK
    n_blk = cap // EXPERT_BLOCK
    lane_e = jnp.arange(LANES)

    xf = x.reshape(t, d)
    xs = jnp.zeros((cap * ROW_TILES, LANES), F32)
    for l in range(depth):
        m6 = mod[l].reshape(nb, 6, 1, d)
        shift1, scale1, gate1, shift2, scale2, gate2 = (m6[:, j] for j in range(6))

        q, k, vt, z, x2, gates = _inproj(xf, shift1, scale1, p["norm_mix"][l], p["w_in"][l], p["w_vt"][l],
                                         p["qg"][l], p["kg"][l], p["bd"], cos, sin,
                                         p["conv_w"][l], p["conv_b"][l], nb, seq)
        z = z.reshape(nb, h1 * FFT_PITCH, D_HYENA)
        x2 = x2.reshape(nb, h1 * FFT_PITCH, D_HYENA)
        bound = HEAD_DIM * jnp.max(jnp.abs(p["qg"][l])) * jnp.max(jnp.abs(p["kg"][l]))
        attn = lax.cond(bound <= SCORE_BOUND,
                        functools.partial(_attention, online=False),
                        functools.partial(_attention, online=True), q, k, vt).reshape(t, ATTN_WIDTH)

        kern = _filters(zfeat, p["filt_w1"][l], p["filt_b1"][l], p["filt_w2"][l], p["filt_b2"][l],
                        p["filt_w3"][l], p["filt_b3"][l], p["filt_freq"][l], p["filt_w4"][l], deltas)
        ka = _fft_outer(tabs["g_full"], kern.reshape(1, 2 * h1 * FFT_PITCH, D_HYENA))
        kf = _fft_filter_inner(tabs["mf"], ka.reshape(1, 2, h1, FFT_PITCH, D_HYENA))
        za = _fft_outer(tabs["g_half"], z)
        zb = _fft_inner(tabs["mf"], tabs["mi"], kf, za.reshape(nb, 2, h1, FFT_PITCH, D_HYENA))
        hy = _fft_final(tabs["g_out"], zb.reshape(nb, 2 * h1 * FFT_PITCH, D_HYENA), z, x2,
                        p["skip"][l]).reshape(nb * h1, FFT_PITCH, D_HYENA)

        x1, h2, route, counts = _merge(attn, hy, gates, xf, gate1, shift2, scale2, p["norm_ffn"][l],
                                       p["w_br_attn"][l], p["w_br_hyena"][l], p["w_out"][l],
                                       p["wr_hi"][l], p["wr_lo"][l], p["b_route"][l], nb, seq)

        cnt = counts[0].astype(jnp.int32)
        padded = jnp.where(lane_e < N_EXPERTS, (cnt + EXPERT_BLOCK - 1) // EXPERT_BLOCK * EXPERT_BLOCK, 0)
        pad_end = jnp.cumsum(padded)
        pstart = (pad_end - padded).astype(F32)[None, :]
        blk_start = jnp.arange(n_blk, dtype=jnp.int32) * EXPERT_BLOCK
        blk_expert = jnp.minimum(
            jnp.sum(blk_start[:, None] >= pad_end[None, :N_EXPERTS], axis=1), N_EXPERTS - 1).astype(jnp.int32)

        dest = _rank(route, pstart, tri)
        d1 = dest[:, 0].astype(jnp.int32)
        d2 = dest[:, 1].astype(jnp.int32)
        n_used = (pad_end[N_EXPERTS - 1:N_EXPERTS] // EXPERT_BLOCK).astype(jnp.int32)
        xs = _dispatch(d1, d2, h2, xs)
        xs = _experts(blk_expert, n_used, xs, p["w_gu"][l], p["w_down"][l])
        xf = _combine(d1, d2, xs, x1, route, gate2, nb, seq)
    return xf.reshape(nb, seq, d)


def kernel(x_prompt, x_sample, c_prompt, c_sample, w_ada, b_ada, norm_mix, norm_ffn, w_in, q_gain, k_gain, conv_w, conv_b, filt_w1, filt_b1, filt_w2, filt_b2, filt_w3, filt_b3, filt_freq, filt_w4, hyena_skip, w_br_attn, w_br_hyena, w_out, w_group, b_group, w_router, b_router, w_e_gate, w_e_up, w_e_down):
    depth = w_ada.shape[0]
    bp, bs = c_prompt.shape[0], c_sample.shape[0]
    rows = -(-(bp + bs) // SUBLANES) * SUBLANES
    c_pad = jnp.zeros((rows, D_MODEL), F32).at[:bp].set(c_prompt).at[bp:bp + bs].set(c_sample)
    mod = _ada(c_pad, w_ada, b_ada)

    scale = HEAD_DIM ** -0.5 * math.log2(math.e)
    head_id = np.arange(ATTN_WIDTH) // HEAD_DIM
    route_w = jnp.concatenate([w_router, w_group], axis=-1)
    route_w = jnp.pad(route_w, ((0, 0), (0, 0), (0, LANES - route_w.shape[-1])))
    wr_hi = route_w.astype(BF16)

    def block_diag(a, b):
        return jnp.concatenate([jnp.pad(a, ((0, 0), (0, 0), (0, b.shape[2]))),
                                jnp.pad(b, ((0, 0), (0, 0), (a.shape[2], 0)))], axis=1)

    twice = lambda w: block_diag(w, w)
    p = dict(
        norm_mix=norm_mix.reshape(depth, 1, D_MODEL),
        norm_ffn=norm_ffn.reshape(depth, 1, D_MODEL),
        w_in=w_in.astype(BF16),
        w_vt=jnp.swapaxes(w_in[:, :, COL_V:COL_U], 1, 2).astype(BF16),
        qg=(jnp.tile(q_gain, (1, N_HEADS)) * scale).reshape(depth, 1, ATTN_WIDTH),
        kg=jnp.tile(k_gain, (1, N_KV_HEADS)).reshape(depth, 1, KV_WIDTH),
        bd=jnp.asarray(head_id[:, None] == head_id[None, :], dtype=BF16),
        conv_w=conv_w, conv_b=conv_b.reshape(depth, 1, 3 * D_HYENA),
        filt_w1=twice(jnp.pad(filt_w1, ((0, 0), (0, FILTER_ORDER - FILTER_EMB), (0, 0)))),
        filt_b1=jnp.tile(filt_b1, (1, 2)).reshape(depth, 1, 2 * FILTER_ORDER),
        filt_w2=twice(filt_w2), filt_b2=jnp.tile(filt_b2, (1, 2)).reshape(depth, 1, 2 * FILTER_ORDER),
        filt_w3=twice(filt_w3), filt_b3=jnp.tile(filt_b3, (1, 2)).reshape(depth, 1, 2 * FILTER_ORDER),
        filt_freq=jnp.tile(filt_freq, (1, 2)).reshape(depth, 1, 2 * FILTER_ORDER),
        filt_w4=block_diag(filt_w4[:, :, :D_HYENA], filt_w4[:, :, D_HYENA:]),
        skip=hyena_skip.reshape(depth, 1, D_HYENA),
        w_br_attn=w_br_attn.astype(BF16), w_br_hyena=w_br_hyena.astype(BF16), w_out=w_out.astype(BF16),
        wr_hi=wr_hi, wr_lo=(route_w - wr_hi.astype(F32)).astype(BF16),
        b_route=jnp.pad(jnp.concatenate([b_router, b_group], axis=-1),
                        ((0, 0), (0, LANES - N_EXPERTS - N_GROUPS))).reshape(depth, 1, LANES),
        w_gu=jnp.concatenate([w_e_gate, w_e_up], axis=-1).astype(BF16),
        w_down=w_e_down.astype(BF16),
    )
    y_prompt = _run_trunk(x_prompt, mod[:, :bp], p)
    y_sample = _run_trunk(x_sample, mod[:, bp:bp + bs], p)
    return (y_prompt, y_sample)
```
